```python
import jax, jax.numpy as jnp
from jax import lax
import numpy as np

D_MODEL = 1024
BATCH = 4
SEQ = 8192
DEPTH = 1

RET_HEADS = 4
RET_DK = 128
RET_DV = 256
RET_CHUNK = 128
SWA_HEADS = 8
SWA_KV_HEADS = 2
SWA_DH = 64
WINDOW = 128
D_FF = 4 * D_MODEL
EPS = 1e-6

RET_QK = RET_HEADS * RET_DK
RET_V = RET_HEADS * RET_DV
SWA_Q = SWA_HEADS * SWA_DH
SWA_KV = SWA_KV_HEADS * SWA_DH
IN_SPLITS = (RET_QK, RET_QK, RET_V, RET_V, SWA_Q, SWA_KV, SWA_KV, D_MODEL, D_MODEL)
D_IN = sum(IN_SPLITS)
SPLIT_IDX = tuple(int(i) for i in np.cumsum(IN_SPLITS)[:-1])

kernel_name = "hybrid_retention_swa_sinks_block"


def rmsnorm(x, g):
    xf = x.astype(jnp.float32)
    y = xf * lax.rsqrt(jnp.mean(xf * xf, axis=-1, keepdims=True) + EPS)
    return (y * g.astype(jnp.float32)).astype(x.dtype)


def retention_chunkwise(q, k, v):
    B, T, H, dk = q.shape
    dv = v.shape[-1]
    C = RET_CHUNK
    N = T // C
    f32 = jnp.float32
    log_gamma = jnp.log(1.0 - jnp.exp2(-5.0 - jnp.arange(H, dtype=f32)))
    q = q.astype(f32).reshape(B, N, C, H, dk)
    k = k.astype(f32).reshape(B, N, C, H, dk) * (dk ** -0.5)
    v = v.astype(f32).reshape(B, N, C, H, dv)
    pos = jnp.arange(C, dtype=f32)
    diff = pos[:, None] - pos[None, :]
    causal = diff >= 0
    decay = jnp.where(causal[None], jnp.exp(log_gamma[:, None, None] * jnp.where(causal, diff, 0.0)[None]), 0.0)
    qk = jnp.einsum('bnihd,bnjhd->bnhij', q, k) * decay[None, None]
    o_inner = jnp.einsum('bnhij,bnjhe->bnihe', qk, v)
    k_dec = k * jnp.exp(log_gamma[None, :] * (C - 1.0 - pos)[:, None])[None, None, :, :, None]
    kv = jnp.einsum('bnjhd,bnjhe->nbhde', k_dec, v)
    chunk_decay = jnp.exp(log_gamma * C)[None, :, None, None]

    def step(s, kv_n):
        return chunk_decay * s + kv_n, s

    _, s_prev = lax.scan(step, jnp.zeros((B, H, dk, dv), f32), kv)
    q_dec = q * jnp.exp(log_gamma[None, :] * (pos + 1.0)[:, None])[None, None, :, :, None]
    o_cross = jnp.einsum('bnihd,nbhde->bnihe', q_dec, s_prev)
    o = (o_inner + o_cross).reshape(B, T, H, dv)
    mu = jnp.mean(o, axis=-1, keepdims=True)
    var = jnp.mean(jnp.square(o - mu), axis=-1, keepdims=True)
    return (o - mu) * lax.rsqrt(var + EPS)


def swa_gqa_sinks(q, k, v, sinks):
    B, T, Hq, dh = q.shape
    G = k.shape[2]
    R = Hq // G
    W = WINDOW
    N = T // W
    f32 = jnp.float32
    qb = q.reshape(B, N, W, G, R, dh)
    kb = k.reshape(B, N, W, G, dh)
    vb = v.reshape(B, N, W, G, dh)
    pad = ((0, 0), (1, 0), (0, 0), (0, 0), (0, 0))
    kk = jnp.concatenate([jnp.pad(kb, pad)[:, :-1], kb], axis=2)
    vv = jnp.concatenate([jnp.pad(vb, pad)[:, :-1], vb], axis=2)
    s = jnp.einsum('bnigrd,bnjgd->bgrnij', qb, kk).astype(f32) * (dh ** -0.5)
    qpos = jnp.arange(W)[:, None] + W
    kpos = jnp.arange(2 * W)[None, :]
    dist = qpos - kpos
    valid = (dist >= 0) & (dist < W)
    blk_valid = valid[None] & ((jnp.arange(N)[:, None, None] > 0) | (kpos >= W)[None])
    slopes = jnp.exp2(-8.0 / Hq * jnp.arange(1, Hq + 1, dtype=f32)).reshape(G, R)
    s = s - slopes[None, :, :, None, None, None] * dist.astype(f32)[None, None, None, None]
    s = jnp.where(blk_valid[None, None, None], s, -jnp.inf)
    sink = sinks.astype(f32).reshape(G, R)[None, :, :, None, None, None]
    m = jnp.maximum(jnp.max(s, axis=-1, keepdims=True), sink)
    p = jnp.exp(s - m)
    p = p / (jnp.sum(p, axis=-1, keepdims=True) + jnp.exp(sink - m))
    out = jnp.einsum('bgrnij,bnjgd->bnigrd', p, vv.astype(f32))
    return out.reshape(B, T, Hq * dh)


def setup_inputs(seed: int = 0) -> dict:
    key = jax.random.key(seed)
    ks = jax.random.split(key, 13)
    f32 = jnp.float32

    def dense(k, fan_in, fan_out):
        return jax.random.normal(k, (DEPTH, fan_in, fan_out), f32) * (fan_in ** -0.5)

    def gain(k):
        return 1.0 + 0.02 * jax.random.normal(k, (DEPTH, D_MODEL), f32)

    return {
        "x": jax.random.normal(ks[0], (BATCH, SEQ, D_MODEL), f32),
        "pre_mix_norm": gain(ks[1]),
        "w_in": dense(ks[2], D_MODEL, D_IN),
        "w_ret_out": dense(ks[3], RET_V, D_MODEL),
        "w_swa_out": dense(ks[4], SWA_Q, D_MODEL),
        "w_out": dense(ks[5], D_MODEL, D_MODEL),
        "sinks": 0.5 * jax.random.normal(ks[6], (DEPTH, SWA_HEADS), f32),
        "post_mix_norm": gain(ks[7]),
        "pre_mlp_norm": gain(ks[8]),
        "w_up": dense(ks[9], D_MODEL, D_FF),
        "w_down": dense(ks[10], D_FF, D_MODEL),
        "post_mlp_norm": gain(ks[11]),
    }


def reference(x, pre_mix_norm, w_in, w_ret_out, w_swa_out, w_out, sinks, post_mix_norm, pre_mlp_norm, w_up, w_down, post_mlp_norm):
    B, T, _ = x.shape
    dt = x.dtype
    for l in range(DEPTH):
        h = rmsnorm(x, pre_mix_norm[l])
        proj = h @ w_in[l]
        q_r, k_r, v_r, g_r, q_s, k_s, v_s, gate_r, gate_s = jnp.split(proj, SPLIT_IDX, axis=-1)
        ret = retention_chunkwise(q_r.reshape(B, T, RET_HEADS, RET_DK),
                                  k_r.reshape(B, T, RET_HEADS, RET_DK),
                                  v_r.reshape(B, T, RET_HEADS, RET_DV)).reshape(B, T, RET_V)
        ret = (jax.nn.silu(g_r.astype(jnp.float32)) * ret).astype(dt)
        y_r = ret @ w_ret_out[l]
        swa = swa_gqa_sinks(q_s.reshape(B, T, SWA_HEADS, SWA_DH),
                            k_s.reshape(B, T, SWA_KV_HEADS, SWA_DH),
                            v_s.reshape(B, T, SWA_KV_HEADS, SWA_DH),
                            sinks[l]).astype(dt)
        y_s = swa @ w_swa_out[l]
        merged = jax.nn.sigmoid(gate_r) * y_r + jax.nn.sigmoid(gate_s) * y_s
        x = x + rmsnorm(merged @ w_out[l], post_mix_norm[l])
        h = rmsnorm(x, pre_mlp_norm[l])
        u = jnp.square(jax.nn.relu(h @ w_up[l]))
        x = x + rmsnorm(u @ w_down[l], post_mlp_norm[l])
    return x
```

```python
import functools

import numpy as np
import jax
import jax.numpy as jnp
from jax import lax
from jax.experimental import pallas as pl
from jax.experimental.pallas import tpu as pltpu

D_MODEL = 1024
RET_HEADS = 4
RET_DK = 128
RET_DV = 256
CHUNK = 128
SWA_HEADS = 8
SWA_GROUPS = 2
SWA_REP = SWA_HEADS // SWA_GROUPS
SWA_DH = 64
D_FF = 4 * D_MODEL
EPS = 1e-6

RET_QK = RET_HEADS * RET_DK
RET_V = RET_HEADS * RET_DV
SWA_Q = SWA_HEADS * SWA_DH
SWA_KV = SWA_GROUPS * SWA_DH

OFF_QR = 0
OFF_KR = OFF_QR + RET_QK
OFF_VR = OFF_KR + RET_QK
OFF_GR = OFF_VR + RET_V
OFF_GATE_R = OFF_GR + RET_V
OFF_GATE_S = OFF_GATE_R + D_MODEL
OFF_QS = OFF_GATE_S + D_MODEL
OFF_KS = OFF_QS + SWA_Q
OFF_VS = OFF_KS + SWA_KV
D_IN = OFF_VS + SWA_KV

VMEM_LIMIT_BYTES = 56 * 1024 * 1024

LOG_GAMMA = [float(np.log(1.0 - 2.0 ** (-5.0 - h))) for h in range(RET_HEADS)]
ALIBI_SLOPES = [float(2.0 ** (-8.0 / SWA_HEADS * (h + 1))) for h in range(SWA_HEADS)]
K_SCALE = float(RET_DK ** -0.5)
S_SCALE = float(SWA_DH ** -0.5)

f32 = jnp.float32
bf16 = jnp.bfloat16


def _in_proj_column_order():
    ref_splits = (RET_QK, RET_QK, RET_V, RET_V, SWA_Q, SWA_KV, SWA_KV, D_MODEL, D_MODEL)
    starts = np.concatenate([[0], np.cumsum(ref_splits)[:-1]])
    q_r, k_r, v_r, g_r, q_s, k_s, v_s, gate_r, gate_s = [
        np.arange(s, s + n) for s, n in zip(starts, ref_splits)]
    q_s = q_s.reshape(SWA_GROUPS, SWA_REP, SWA_DH).transpose(1, 0, 2).reshape(-1)
    return np.concatenate([q_r, k_r, v_r, g_r, gate_r, gate_s, q_s, k_s, v_s])


def _swa_out_row_order():
    rows = np.arange(SWA_Q).reshape(SWA_GROUPS, SWA_REP, SWA_DH).transpose(1, 0, 2).reshape(-1)
    return rows


def _rms_scale(v):
    return lax.rsqrt(jnp.mean(v * v, axis=-1, keepdims=True) + EPS)


def _in_proj_kernel(x_ref, g_ref, w_ref, o_ref, *, col_chunk):
    x = x_ref[...]
    h = (x * _rms_scale(x) * g_ref[...]).astype(bf16)
    for c0 in range(0, D_IN, col_chunk):
        c1 = min(c0 + col_chunk, D_IN)
        o_ref[:, c0:c1] = jnp.dot(h, w_ref[:, c0:c1], preferred_element_type=f32).astype(bf16)


def _in_proj(x2d, gain, w_in, *, tm=512, col_chunk=1024):
    n = x2d.shape[0]
    return pl.pallas_call(
        functools.partial(_in_proj_kernel, col_chunk=col_chunk),
        out_shape=jax.ShapeDtypeStruct((n, D_IN), bf16),
        grid=(n // tm,),
        in_specs=[
            pl.BlockSpec((tm, D_MODEL), lambda i: (i, 0)),
            pl.BlockSpec((1, D_MODEL), lambda i: (0, 0)),
            pl.BlockSpec((D_MODEL, D_IN), lambda i: (0, 0), pipeline_mode=pl.Buffered(1)),
        ],
        out_specs=pl.BlockSpec((tm, D_IN), lambda i: (i, 0)),
        compiler_params=pltpu.CompilerParams(
            dimension_semantics=("arbitrary",), vmem_limit_bytes=VMEM_LIMIT_BYTES),
        name="in_proj",
    )(x2d, gain, w_in)


def _mixer_kernel(qr_ref, kr_ref, vr_ref, gr_ref, qs_ref, ks_ref, vs_ref, sink_ref,
                  ret_ref, swa_ref,
                  state_ref, kk_ref, vv_ref, dmat_ref, qdec_ref, kdec_ref, bias_ref):
    b = pl.program_id(0)
    c = pl.program_id(1)
    C = CHUNK

    @pl.when((b == 0) & (c == 0))
    def _init_tables():
        i = lax.broadcasted_iota(jnp.int32, (C, C), 0).astype(f32)
        j = lax.broadcasted_iota(jnp.int32, (C, C), 1).astype(f32)
        diff = i - j
        for h in range(RET_HEADS):
            lg = LOG_GAMMA[h]
            dmat_ref[h] = jnp.where(diff >= 0, jnp.exp(lg * jnp.maximum(diff, 0.0)), 0.0) * K_SCALE
            qdec_ref[h] = jnp.exp(lg * (i + 1.0))
            kdec_ref[h] = jnp.exp(lg * (C - 1.0 - i)) * K_SCALE
        row = lax.broadcasted_iota(jnp.int32, (SWA_REP * C, 2 * C), 0)
        kpos = lax.broadcasted_iota(jnp.int32, (SWA_REP * C, 2 * C), 1)
        rep = row // C
        dist = (row - rep * C + C) - kpos
        valid = (dist >= 0) & (dist < C)
        distf = dist.astype(f32)
        for g in range(SWA_GROUPS):
            slope = jnp.zeros((SWA_REP * C, 2 * C), f32)
            for r in range(SWA_REP):
                slope = jnp.where(rep == r, ALIBI_SLOPES[g * SWA_REP + r], slope)
            bias_ref[g] = jnp.where(valid, -slope * distf, -jnp.inf)

    @pl.when(c == 0)
    def _reset_sequence_state():
        state_ref[...] = jnp.zeros_like(state_ref)
        kk_ref[0:C, :] = jnp.zeros((C, SWA_KV), bf16)
        vv_ref[0:C, :] = jnp.zeros((C, SWA_KV), bf16)

    for h in range(RET_HEADS):
        q = qr_ref[:, h * RET_DK:(h + 1) * RET_DK]
        k = kr_ref[:, h * RET_DK:(h + 1) * RET_DK]
        v = vr_ref[:, h * RET_DV:(h + 1) * RET_DV]
        s_prev = state_ref[h]
        qk = lax.dot_general(q, k, (((1,), (1,)), ((), ())), preferred_element_type=f32)
        a = (qk * dmat_ref[h]).astype(bf16)
        qd = (q.astype(f32) * qdec_ref[h]).astype(bf16)
        o = (jnp.dot(a, v, preferred_element_type=f32)
             + jnp.dot(qd, s_prev.astype(bf16), preferred_element_type=f32))
        kd_t = (k.astype(f32) * kdec_ref[h]).T.astype(bf16)
        kv = jnp.dot(kd_t, v, preferred_element_type=f32)
        state_ref[h] = float(np.exp(LOG_GAMMA[h] * C)) * s_prev + kv
        mu = jnp.mean(o, axis=-1, keepdims=True)
        d = o - mu
        var = jnp.mean(d * d, axis=-1, keepdims=True)
        gate = gr_ref[:, h * RET_DV:(h + 1) * RET_DV].astype(f32)
        ret_ref[:, h * RET_DV:(h + 1) * RET_DV] = (
            gate * jax.nn.sigmoid(gate) * (d * lax.rsqrt(var + EPS))).astype(bf16)

    kk_ref[C:2 * C, :] = ks_ref[...]
    vv_ref[C:2 * C, :] = vs_ref[...]
    kk = kk_ref[...].astype(f32)
    vv = vv_ref[...]
    lane = lax.broadcasted_iota(jnp.int32, (2 * C, SWA_KV), 1)
    q_all = jnp.concatenate([qs_ref[:, j * C:(j + 1) * C] for j in range(SWA_REP)], axis=0)
    kcol = lax.broadcasted_iota(jnp.int32, (1, 2 * C), 1)
    first_block_mask = jnp.where((kcol < C) & (c == 0), -jnp.inf, 0.0)
    outs = []
    for g in range(SWA_GROUPS):
        in_group = (lane >= g * SWA_DH) & (lane < (g + 1) * SWA_DH)
        k_g = jnp.where(in_group, kk, 0.0).astype(bf16)
        s = lax.dot_general(q_all, k_g, (((1,), (1,)), ((), ())), preferred_element_type=f32)
        s = s * S_SCALE + bias_ref[g] + first_block_mask
        sink = sink_ref[g]
        m = jnp.maximum(jnp.max(s, axis=-1, keepdims=True), sink)
        p = jnp.exp(s - m)
        den = jnp.sum(p, axis=-1, keepdims=True) + jnp.exp(sink - m)
        o_g = jnp.dot(p.astype(bf16), vv, preferred_element_type=f32)
        outs.append(o_g * (1.0 / den))
    out_lane = lax.broadcasted_iota(jnp.int32, (SWA_REP * C, SWA_KV), 1)
    o_all = jnp.where(out_lane < SWA_DH, outs[0], outs[1]).astype(bf16)
    for j in range(SWA_REP):
        swa_ref[:, j * C:(j + 1) * C] = o_all[j * C:(j + 1) * C, :]
    kk_ref[0:C, :] = ks_ref[...]
    vv_ref[0:C, :] = vs_ref[...]


def _mixer(proj, sink_col, batch, seq):
    n = proj.shape[0]
    nc = seq // CHUNK
    C = CHUNK

    def rows(col_block):
        return lambda b, c: (b * nc + c, col_block)

    return pl.pallas_call(
        _mixer_kernel,
        out_shape=(jax.ShapeDtypeStruct((n, RET_V), bf16), jax.ShapeDtypeStruct((n, SWA_Q), bf16)),
        grid=(batch, nc),
        in_specs=[
            pl.BlockSpec((C, RET_QK), rows(OFF_QR // RET_QK)),
            pl.BlockSpec((C, RET_QK), rows(OFF_KR // RET_QK)),
            pl.BlockSpec((C, RET_V), rows(OFF_VR // RET_V)),
            pl.BlockSpec((C, RET_V), rows(OFF_GR // RET_V)),
            pl.BlockSpec((C, SWA_Q), rows(OFF_QS // SWA_Q)),
            pl.BlockSpec((C, SWA_KV), rows(OFF_KS // SWA_KV)),
            pl.BlockSpec((C, SWA_KV), rows(OFF_VS // SWA_KV)),
            pl.BlockSpec((SWA_GROUPS, SWA_REP * C, 1), lambda b, c: (0, 0, 0)),
        ],
        out_specs=(
            pl.BlockSpec((C, RET_V), rows(0)),
            pl.BlockSpec((C, SWA_Q), rows(0)),
        ),
        scratch_shapes=[
            pltpu.VMEM((RET_HEADS, RET_DK, RET_DV), f32),
            pltpu.VMEM((2 * C, SWA_KV), bf16),
            pltpu.VMEM((2 * C, SWA_KV), bf16),
            pltpu.VMEM((RET_HEADS, C, C), f32),
            pltpu.VMEM((RET_HEADS, C, C), f32),
            pltpu.VMEM((RET_HEADS, C, C), f32),
            pltpu.VMEM((SWA_GROUPS, SWA_REP * C, 2 * C), f32),
        ],
        compiler_params=pltpu.CompilerParams(
            dimension_semantics=("arbitrary", "arbitrary"), vmem_limit_bytes=VMEM_LIMIT_BYTES),
        name="mixer",
    )(proj, proj, proj, proj, proj, proj, proj, sink_col)


def _post_mix_kernel(x_ref, ret_ref, swa_ref, gr_ref, gs_ref, wro_ref, wso_ref, wo_ref, g_ref, o_ref):
    y_r = jnp.dot(ret_ref[...], wro_ref[...], preferred_element_type=f32)
    y_s = jnp.dot(swa_ref[...], wso_ref[...], preferred_element_type=f32)
    merged = (jax.nn.sigmoid(gr_ref[...].astype(f32)) * y_r
              + jax.nn.sigmoid(gs_ref[...].astype(f32)) * y_s)
    z = jnp.dot(merged.astype(bf16), wo_ref[...], preferred_element_type=f32)
    o_ref[...] = x_ref[...] + z * _rms_scale(z) * g_ref[...]


def _post_mix(x2d, ret, swa, proj, w_ro, w_so, w_o, gain, *, tm=512):
    n = x2d.shape[0]
    const = lambda i: (0, 0)
    return pl.pallas_call(
        _post_mix_kernel,
        out_shape=jax.ShapeDtypeStruct((n, D_MODEL), f32),
        grid=(n // tm,),
        in_specs=[
            pl.BlockSpec((tm, D_MODEL), lambda i: (i, 0)),
            pl.BlockSpec((tm, RET_V), lambda i: (i, 0)),
            pl.BlockSpec((tm, SWA_Q), lambda i: (i, 0)),
            pl.BlockSpec((tm, D_MODEL), lambda i: (i, OFF_GATE_R // D_MODEL)),
            pl.BlockSpec((tm, D_MODEL), lambda i: (i, OFF_GATE_S // D_MODEL)),
            pl.BlockSpec((RET_V, D_MODEL), const, pipeline_mode=pl.Buffered(1)),
            pl.BlockSpec((SWA_Q, D_MODEL), const, pipeline_mode=pl.Buffered(1)),
            pl.BlockSpec((D_MODEL, D_MODEL), const, pipeline_mode=pl.Buffered(1)),
            pl.BlockSpec((1, D_MODEL), const),
        ],
        out_specs=pl.BlockSpec((tm, D_MODEL), lambda i: (i, 0)),
        compiler_params=pltpu.CompilerParams(
            dimension_semantics=("arbitrary",), vmem_limit_bytes=VMEM_LIMIT_BYTES),
        name="post_mix",
    )(x2d, ret, swa, proj, proj, w_ro, w_so, w_o, gain)


def _mlp_kernel(x_ref, gpre_ref, wup_ref, wdn_ref, gpost_ref, o_ref, *, ff_chunk):
    x = x_ref[...]
    h = (x * _rms_scale(x) * gpre_ref[...]).astype(bf16)
    acc = None
    for c0 in range(0, D_FF, ff_chunk):
        u = jnp.dot(h, wup_ref[:, c0:c0 + ff_chunk], preferred_element_type=f32)
        u = jnp.maximum(u, 0.0)
        part = jnp.dot((u * u).astype(bf16), wdn_ref[c0:c0 + ff_chunk, :], preferred_element_type=f32)
        acc = part if acc is None else acc + part
    o_ref[...] = x + acc * _rms_scale(acc) * gpost_ref[...]


def _mlp(x2d, gpre, w_up, w_dn, gpost, *, tm=512, ff_chunk=1024):
    n = x2d.shape[0]
    const = lambda i: (0, 0)
    return pl.pallas_call(
        functools.partial(_mlp_kernel, ff_chunk=ff_chunk),
        out_shape=jax.ShapeDtypeStruct((n, D_MODEL), f32),
        grid=(n // tm,),
        in_specs=[
            pl.BlockSpec((tm, D_MODEL), lambda i: (i, 0)),
            pl.BlockSpec((1, D_MODEL), const),
            pl.BlockSpec((D_MODEL, D_FF), const, pipeline_mode=pl.Buffered(1)),
            pl.BlockSpec((D_FF, D_MODEL), const, pipeline_mode=pl.Buffered(1)),
            pl.BlockSpec((1, D_MODEL), const),
        ],
        out_specs=pl.BlockSpec((tm, D_MODEL), lambda i: (i, 0)),
        compiler_params=pltpu.CompilerParams(
            dimension_semantics=("arbitrary",), vmem_limit_bytes=VMEM_LIMIT_BYTES),
        name="mlp",
    )(x2d, gpre, w_up, w_dn, gpost)


def kernel(x, pre_mix_norm, w_in, w_ret_out, w_swa_out, w_out, sinks, post_mix_norm, pre_mlp_norm, w_up,
           w_down, post_mlp_norm):
    batch, seq, d = x.shape
    depth = w_in.shape[0]
    assert d == D_MODEL and seq % CHUNK == 0
    col_order = _in_proj_column_order()
    row_order = _swa_out_row_order()
    x2d = x.reshape(batch * seq, d)
    for l in range(depth):
        proj = _in_proj(x2d, pre_mix_norm[l][None, :], w_in[l][:, col_order].astype(bf16))
        sink_col = jnp.repeat(sinks[l].astype(f32).reshape(SWA_GROUPS, SWA_REP), CHUNK, axis=1)[..., None]
        ret, swa = _mixer(proj, sink_col, batch, seq)
        x2d = _post_mix(x2d, ret, swa, proj, w_ret_out[l].astype(bf16),
                        w_swa_out[l][row_order, :].astype(bf16), w_out[l].astype(bf16),
                        post_mix_norm[l][None, :])
        x2d = _mlp(x2d, pre_mlp_norm[l][None, :], w_up[l].astype(bf16), w_down[l].astype(bf16),
                   post_mlp_norm[l][None, :])
    return x2d.reshape(batch, seq, d)
```

```python
import functools

import numpy as np
import jax
import jax.numpy as jnp
from jax import lax
from jax.experimental import pallas as pl
from jax.experimental.pallas import tpu as pltpu

D_MODEL = 1024
RET_HEADS = 4
RET_DK = 128
RET_DV = 256
CHUNK = 128
SWA_HEADS = 8
SWA_GROUPS = 2
SWA_REP = SWA_HEADS // SWA_GROUPS
SWA_DH = 64
D_FF = 4 * D_MODEL
EPS = 1e-6

RET_QK = RET_HEADS * RET_DK
RET_V = RET_HEADS * RET_DV
SWA_Q = SWA_HEADS * SWA_DH
SWA_KV = SWA_GROUPS * SWA_DH

OFF_QR = 0
OFF_KR = OFF_QR + RET_QK
OFF_VR = OFF_KR + RET_QK
OFF_GR = OFF_VR + RET_V
OFF_GATE_R = OFF_GR + RET_V
OFF_GATE_S = OFF_GATE_R + D_MODEL
OFF_QS = OFF_GATE_S + D_MODEL
OFF_KS = OFF_QS + SWA_Q
OFF_VS = OFF_KS + SWA_KV
D_IN = OFF_VS + SWA_KV

VMEM_LIMIT_BYTES = 56 * 1024 * 1024

LOG_GAMMA = [float(np.log(1.0 - 2.0 ** (-5.0 - h))) for h in range(RET_HEADS)]
ALIBI_SLOPES = [float(2.0 ** (-8.0 / SWA_HEADS * (h + 1))) for h in range(SWA_HEADS)]
K_SCALE = float(RET_DK ** -0.5)
S_SCALE = float(SWA_DH ** -0.5)

f32 = jnp.float32
bf16 = jnp.bfloat16


def _in_proj_column_order():
    ref_splits = (RET_QK, RET_QK, RET_V, RET_V, SWA_Q, SWA_KV, SWA_KV, D_MODEL, D_MODEL)
    starts = np.concatenate([[0], np.cumsum(ref_splits)[:-1]])
    q_r, k_r, v_r, g_r, q_s, k_s, v_s, gate_r, gate_s = [
        np.arange(s, s + n) for s, n in zip(starts, ref_splits)]
    q_s = q_s.reshape(SWA_GROUPS, SWA_REP, SWA_DH).transpose(1, 0, 2).reshape(-1)
    return np.concatenate([q_r, k_r, v_r, g_r, gate_r, gate_s, q_s, k_s, v_s])


def _swa_out_row_order():
    rows = np.arange(SWA_Q).reshape(SWA_GROUPS, SWA_REP, SWA_DH).transpose(1, 0, 2).reshape(-1)
    return rows


def _rms_scale(v):
    return lax.rsqrt(jnp.mean(v * v, axis=-1, keepdims=True) + EPS)


def _in_proj_kernel(x_ref, g_ref, w_ref, o_ref, *, col_chunk):
    x = x_ref[...]
    h = (x * _rms_scale(x) * g_ref[...]).astype(bf16)
    for c0 in range(0, D_IN, col_chunk):
        c1 = min(c0 + col_chunk, D_IN)
        o_ref[:, c0:c1] = jnp.dot(h, w_ref[:, c0:c1], preferred_element_type=f32).astype(bf16)


def _in_proj(x2d, gain, w_in, *, tm=512, col_chunk=1024):
    n = x2d.shape[0]
    return pl.pallas_call(
        functools.partial(_in_proj_kernel, col_chunk=col_chunk),
        out_shape=jax.ShapeDtypeStruct((n, D_IN), bf16),
        grid=(n // tm,),
        in_specs=[
            pl.BlockSpec((tm, D_MODEL), lambda i: (i, 0)),
            pl.BlockSpec((1, D_MODEL), lambda i: (0, 0)),
            pl.BlockSpec((D_MODEL, D_IN), lambda i: (0, 0), pipeline_mode=pl.Buffered(1)),
        ],
        out_specs=pl.BlockSpec((tm, D_IN), lambda i: (i, 0)),
        compiler_params=pltpu.CompilerParams(
            dimension_semantics=("arbitrary",), vmem_limit_bytes=VMEM_LIMIT_BYTES),
        name="in_proj",
    )(x2d, gain, w_in)


def _mixer_kernel(qr_ref, kr_ref, vr_ref, gr_ref, qs_ref, ks_ref, vs_ref, sink_ref,
                  ret_ref, swa_ref,
                  state_ref, kk_ref, vv_ref, dmat_ref, qdec_ref, kdec_ref, bias_ref):
    b = pl.program_id(0)
    c = pl.program_id(1)
    C = CHUNK

    @pl.when((b == 0) & (c == 0))
    def _init_tables():
        i = lax.broadcasted_iota(jnp.int32, (C, C), 0).astype(f32)
        j = lax.broadcasted_iota(jnp.int32, (C, C), 1).astype(f32)
        diff = i - j
        for h in range(RET_HEADS):
            lg = LOG_GAMMA[h]
            dmat_ref[h] = jnp.where(diff >= 0, jnp.exp(lg * jnp.maximum(diff, 0.0)), 0.0) * K_SCALE
            qdec_ref[h] = jnp.exp(lg * (i + 1.0))
            kdec_ref[h] = jnp.exp(lg * (C - 1.0 - i)) * K_SCALE
        row = lax.broadcasted_iota(jnp.int32, (SWA_REP * C, 2 * C), 0)
        kpos = lax.broadcasted_iota(jnp.int32, (SWA_REP * C, 2 * C), 1)
        rep = row // C
        dist = (row - rep * C + C) - kpos
        valid = (dist >= 0) & (dist < C)
        distf = dist.astype(f32)
        for g in range(SWA_GROUPS):
            slope = jnp.zeros((SWA_REP * C, 2 * C), f32)
            for r in range(SWA_REP):
                slope = jnp.where(rep == r, ALIBI_SLOPES[g * SWA_REP + r], slope)
            bias_ref[g] = jnp.where(valid, -slope * distf, -jnp.inf)
            bias_ref[SWA_GROUPS + g] = jnp.where(valid & (kpos >= C), -slope * distf, -jnp.inf)
        vv_ref[:, SWA_KV:2 * SWA_KV] = jnp.ones((2 * C, SWA_KV), bf16)

    @pl.when(c == 0)
    def _reset_sequence_state():
        state_ref[...] = jnp.zeros_like(state_ref)
        kk_ref[0:C, :] = jnp.zeros((C, SWA_KV), bf16)
        vv_ref[0:C, 0:SWA_KV] = jnp.zeros((C, SWA_KV), bf16)

    for h in range(RET_HEADS):
        q = qr_ref[:, h * RET_DK:(h + 1) * RET_DK]
        k = kr_ref[:, h * RET_DK:(h + 1) * RET_DK]
        v = vr_ref[:, h * RET_DV:(h + 1) * RET_DV]
        s_prev = state_ref[h]
        qk = lax.dot_general(q, k, (((1,), (1,)), ((), ())), preferred_element_type=f32)
        a = (qk * dmat_ref[h]).astype(bf16)
        qd = (q.astype(f32) * qdec_ref[h]).astype(bf16)
        o = (jnp.dot(a, v, preferred_element_type=f32)
             + jnp.dot(qd, s_prev.astype(bf16), preferred_element_type=f32))
        kd_t = (k.astype(f32) * kdec_ref[h]).T.astype(bf16)
        kv = jnp.dot(kd_t, v, preferred_element_type=f32)
        state_ref[h] = float(np.exp(LOG_GAMMA[h] * C)) * s_prev + kv
        mu = jnp.mean(o, axis=-1, keepdims=True)
        d = o - mu
        var = jnp.mean(d * d, axis=-1, keepdims=True)
        gate = gr_ref[:, h * RET_DV:(h + 1) * RET_DV].astype(f32)
        ret_ref[:, h * RET_DV:(h + 1) * RET_DV] = (
            gate * jax.nn.sigmoid(gate) * (d * lax.rsqrt(var + EPS))).astype(bf16)

    kk_ref[C:2 * C, :] = ks_ref[...]
    vv_ref[C:2 * C, 0:SWA_KV] = vs_ref[...]
    kk = kk_ref[...].astype(f32)
    vv_ones = vv_ref[...]
    lane = lax.broadcasted_iota(jnp.int32, (2 * C, SWA_KV), 1)
    k_groups = []
    for g in range(SWA_GROUPS):
        in_group = (lane >= g * SWA_DH) & (lane < (g + 1) * SWA_DH)
        k_groups.append((jnp.where(in_group, kk, 0.0) * S_SCALE).astype(bf16))
    table0 = jnp.where(c == 0, SWA_GROUPS, 0)
    out_lane = lax.broadcasted_iota(jnp.int32, (C, SWA_KV), 1)
    for j in range(SWA_REP):
        q_pair = qs_ref[:, j * C:(j + 1) * C]
        num, den = [], []
        for g in range(SWA_GROUPS):
            sink = sink_ref[g * SWA_REP + j]
            s = lax.dot_general(q_pair, k_groups[g], (((1,), (1,)), ((), ())), preferred_element_type=f32)
            s = s + bias_ref[table0 + g, j * C:(j + 1) * C, :]
            m = jnp.maximum(jnp.max(s, axis=-1, keepdims=True), sink)
            p = jnp.exp(s - m)
            r = jnp.dot(p.astype(bf16), vv_ones, preferred_element_type=f32)
            num.append(r[:, 0:SWA_KV])
            den.append(r[:, SWA_KV:2 * SWA_KV] + jnp.exp(sink - m))
        first_group = out_lane < SWA_DH
        o_pair = jnp.where(first_group, num[0], num[1]) / jnp.where(first_group, den[0], den[1])
        swa_ref[:, j * C:(j + 1) * C] = o_pair.astype(bf16)
    kk_ref[0:C, :] = ks_ref[...]
    vv_ref[0:C, 0:SWA_KV] = vs_ref[...]


def _mixer(proj, sinks, batch, seq):
    n = proj.shape[0]
    nc = seq // CHUNK
    C = CHUNK

    def rows(col_block):
        return lambda b, c: (b * nc + c, col_block)

    return pl.pallas_call(
        _mixer_kernel,
        out_shape=(jax.ShapeDtypeStruct((n, RET_V), bf16), jax.ShapeDtypeStruct((n, SWA_Q), bf16)),
        grid=(batch, nc),
        in_specs=[
            pl.BlockSpec((C, RET_QK), rows(OFF_QR // RET_QK)),
            pl.BlockSpec((C, RET_QK), rows(OFF_KR // RET_QK)),
            pl.BlockSpec((C, RET_V), rows(OFF_VR // RET_V)),
            pl.BlockSpec((C, RET_V), rows(OFF_GR // RET_V)),
            pl.BlockSpec((C, SWA_Q), rows(OFF_QS // SWA_Q)),
            pl.BlockSpec((C, SWA_KV), rows(OFF_KS // SWA_KV)),
            pl.BlockSpec((C, SWA_KV), rows(OFF_VS // SWA_KV)),
            pl.BlockSpec(memory_space=pltpu.SMEM),
        ],
        out_specs=(
            pl.BlockSpec((C, RET_V), rows(0)),
            pl.BlockSpec((C, SWA_Q), rows(0)),
        ),
        scratch_shapes=[
            pltpu.VMEM((RET_HEADS, RET_DK, RET_DV), f32),
            pltpu.VMEM((2 * C, SWA_KV), bf16),
            pltpu.VMEM((2 * C, 2 * SWA_KV), bf16),
            pltpu.VMEM((RET_HEADS, C, C), f32),
            pltpu.VMEM((RET_HEADS, C, C), f32),
            pltpu.VMEM((RET_HEADS, C, C), f32),
            pltpu.VMEM((2 * SWA_GROUPS, SWA_REP * C, 2 * C), f32),
        ],
        compiler_params=pltpu.CompilerParams(
            dimension_semantics=("arbitrary", "arbitrary"), vmem_limit_bytes=VMEM_LIMIT_BYTES),
        name="mixer",
    )(proj, proj, proj, proj, proj, proj, proj, sinks)


def _post_mix_kernel(x_ref, ret_ref, swa_ref, gr_ref, gs_ref, wro_ref, wso_ref, wo_ref, g_ref, o_ref):
    y_r = jnp.dot(ret_ref[...], wro_ref[...], preferred_element_type=f32)
    y_s = jnp.dot(swa_ref[...], wso_ref[...], preferred_element_type=f32)
    merged = (jax.nn.sigmoid(gr_ref[...].astype(f32)) * y_r
              + jax.nn.sigmoid(gs_ref[...].astype(f32)) * y_s)
    z = jnp.dot(merged.astype(bf16), wo_ref[...], preferred_element_type=f32)
    o_ref[...] = x_ref[...] + z * _rms_scale(z) * g_ref[...]


def _post_mix(x2d, ret, swa, proj, w_ro, w_so, w_o, gain, *, tm=512):
    n = x2d.shape[0]
    const = lambda i: (0, 0)
    return pl.pallas_call(
        _post_mix_kernel,
        out_shape=jax.ShapeDtypeStruct((n, D_MODEL), f32),
        grid=(n // tm,),
        in_specs=[
            pl.BlockSpec((tm, D_MODEL), lambda i: (i, 0)),
            pl.BlockSpec((tm, RET_V), lambda i: (i, 0)),
            pl.BlockSpec((tm, SWA_Q), lambda i: (i, 0)),
            pl.BlockSpec((tm, D_MODEL), lambda i: (i, OFF_GATE_R // D_MODEL)),
            pl.BlockSpec((tm, D_MODEL), lambda i: (i, OFF_GATE_S // D_MODEL)),
            pl.BlockSpec((RET_V, D_MODEL), const, pipeline_mode=pl.Buffered(1)),
            pl.BlockSpec((SWA_Q, D_MODEL), const, pipeline_mode=pl.Buffered(1)),
            pl.BlockSpec((D_MODEL, D_MODEL), const, pipeline_mode=pl.Buffered(1)),
            pl.BlockSpec((1, D_MODEL), const),
        ],
        out_specs=pl.BlockSpec((tm, D_MODEL), lambda i: (i, 0)),
        compiler_params=pltpu.CompilerParams(
            dimension_semantics=("arbitrary",), vmem_limit_bytes=VMEM_LIMIT_BYTES),
        name="post_mix",
    )(x2d, ret, swa, proj, proj, w_ro, w_so, w_o, gain)


def _mlp_kernel(x_ref, gpre_ref, wup_ref, wdn_ref, gpost_ref, o_ref, *, ff_chunk):
    x = x_ref[...]
    h = (x * _rms_scale(x) * gpre_ref[...]).astype(bf16)
    acc = None
    for c0 in range(0, D_FF, ff_chunk):
        u = jnp.dot(h, wup_ref[:, c0:c0 + ff_chunk], preferred_element_type=f32)
        u = jnp.maximum(u, 0.0)
        part = jnp.dot((u * u).astype(bf16), wdn_ref[c0:c0 + ff_chunk, :], preferred_element_type=f32)
        acc = part if acc is None else acc + part
    o_ref[...] = x + acc * _rms_scale(acc) * gpost_ref[...]


def _mlp(x2d, gpre, w_up, w_dn, gpost, *, tm=512, ff_chunk=1024):
    n = x2d.shape[0]
    const = lambda i: (0, 0)
    return pl.pallas_call(
        functools.partial(_mlp_kernel, ff_chunk=ff_chunk),
        out_shape=jax.ShapeDtypeStruct((n, D_MODEL), f32),
        grid=(n // tm,),
        in_specs=[
            pl.BlockSpec((tm, D_MODEL), lambda i: (i, 0)),
            pl.BlockSpec((1, D_MODEL), const),
            pl.BlockSpec((D_MODEL, D_FF), const, pipeline_mode=pl.Buffered(1)),
            pl.BlockSpec((D_FF, D_MODEL), const, pipeline_mode=pl.Buffered(1)),
            pl.BlockSpec((1, D_MODEL), const),
        ],
        out_specs=pl.BlockSpec((tm, D_MODEL), lambda i: (i, 0)),
        compiler_params=pltpu.CompilerParams(
            dimension_semantics=("arbitrary",), vmem_limit_bytes=VMEM_LIMIT_BYTES),
        name="mlp",
    )(x2d, gpre, w_up, w_dn, gpost)


def kernel(x, pre_mix_norm, w_in, w_ret_out, w_swa_out, w_out, sinks, post_mix_norm, pre_mlp_norm, w_up,
           w_down, post_mlp_norm):
    batch, seq, d = x.shape
    depth = w_in.shape[0]
    assert d == D_MODEL and seq % CHUNK == 0
    col_order = _in_proj_column_order()
    row_order = _swa_out_row_order()
    x2d = x.reshape(batch * seq, d)
    for l in range(depth):
        proj = _in_proj(x2d, pre_mix_norm[l][None, :], w_in[l][:, col_order].astype(bf16))
        ret, swa = _mixer(proj, sinks[l].astype(f32), batch, seq)
        x2d = _post_mix(x2d, ret, swa, proj, w_ret_out[l].astype(bf16),
                        w_swa_out[l][row_order, :].astype(bf16), w_out[l].astype(bf16),
                        post_mix_norm[l][None, :])
        x2d = _mlp(x2d, pre_mlp_norm[l][None, :], w_up[l].astype(bf16), w_down[l].astype(bf16),
                   post_mlp_norm[l][None, :])
    return x2d.reshape(batch, seq, d)
```

```python
import functools

import numpy as np
import jax
import jax.numpy as jnp
from jax import lax
from jax.experimental import pallas as pl
from jax.experimental.pallas import tpu as pltpu

D_MODEL = 1024
RET_HEADS = 4
RET_DK = 128
RET_DV = 256
CHUNK = 128
SWA_HEADS = 8
SWA_GROUPS = 2
SWA_REP = SWA_HEADS // SWA_GROUPS
SWA_DH = 64
D_FF = 4 * D_MODEL
EPS = 1e-6

RET_QK = RET_HEADS * RET_DK
RET_V = RET_HEADS * RET_DV
SWA_Q = SWA_HEADS * SWA_DH
SWA_KV = SWA_GROUPS * SWA_DH

W_QR = 0
W_KR = W_QR + RET_QK
W_VR = W_KR + RET_QK
W_GR = W_VR + RET_V
W_QS = W_GR + RET_V
W_KS = W_QS + SWA_Q
W_VS = W_KS + SWA_KV
W_GATES = W_VS + SWA_KV
D_IN = W_GATES + 2 * D_MODEL

M_QR = 0
M_KR = M_QR + RET_QK
M_VR = M_KR + RET_QK
M_SG = M_VR + RET_V
M_QD = M_SG + RET_V
M_QS = M_QD + RET_QK
M_KG = M_QS + SWA_Q
M_VS = M_KG + SWA_GROUPS * SWA_KV
D_MIXIN = M_VS + SWA_KV

TOK_TILE = 512
CHUNKS_PER_TILE = TOK_TILE // CHUNK

VMEM_LIMIT_BYTES = 56 * 1024 * 1024

LOG_GAMMA = [float(np.log(1.0 - 2.0 ** (-5.0 - h))) for h in range(RET_HEADS)]
ALIBI_SLOPES = [float(2.0 ** (-8.0 / SWA_HEADS * (h + 1))) for h in range(SWA_HEADS)]
K_SCALE = float(RET_DK ** -0.5)
S_SCALE = float(SWA_DH ** -0.5)

f32 = jnp.float32
bf16 = jnp.bfloat16


def _in_proj_column_order():
    ref_splits = (RET_QK, RET_QK, RET_V, RET_V, SWA_Q, SWA_KV, SWA_KV, D_MODEL, D_MODEL)
    starts = np.concatenate([[0], np.cumsum(ref_splits)[:-1]])
    q_r, k_r, v_r, g_r, q_s, k_s, v_s, gate_r, gate_s = [
        np.arange(s, s + n) for s, n in zip(starts, ref_splits)]
    q_s = q_s.reshape(SWA_GROUPS, SWA_REP, SWA_DH).transpose(1, 0, 2).reshape(-1)
    return np.concatenate([q_r, k_r, v_r, g_r, q_s, k_s, v_s, gate_r, gate_s])


def _swa_out_row_order():
    return np.arange(SWA_Q).reshape(SWA_GROUPS, SWA_REP, SWA_DH).transpose(1, 0, 2).reshape(-1)


def _rms_scale(v):
    return lax.rsqrt(jnp.mean(v * v, axis=-1, keepdims=True) + EPS)


def _in_proj_kernel(x_ref, g_ref, w_ref, mix_ref, kdt_ref, gates_ref, qdec_ref, kdec_ref):
    C = CHUNK

    @pl.when(pl.program_id(0) == 0)
    def _init_decay_tables():
        pos = (lax.broadcasted_iota(jnp.int32, (TOK_TILE, RET_DK), 0) % C).astype(f32)
        for h in range(RET_HEADS):
            qdec_ref[:, h * RET_DK:(h + 1) * RET_DK] = jnp.exp(LOG_GAMMA[h] * (pos + 1.0))
            kdec_ref[:, h * RET_DK:(h + 1) * RET_DK] = jnp.exp(LOG_GAMMA[h] * (C - 1.0 - pos)) * K_SCALE

    x = x_ref[...]
    row_scale = _rms_scale(x)
    xg = (x * g_ref[...]).astype(bf16)

    def proj(c0, width):
        return jnp.dot(xg, w_ref[:, c0:c0 + width], preferred_element_type=f32) * row_scale

    for c0 in range(0, 2 * D_MODEL, D_MODEL):
        gates_ref[:, c0:c0 + D_MODEL] = jax.nn.sigmoid(proj(W_GATES + c0, D_MODEL)).astype(bf16)
    g = proj(W_GR, RET_V)
    mix_ref[:, M_SG:M_SG + RET_V] = (g * jax.nn.sigmoid(g)).astype(bf16)
    q = proj(W_QR, RET_QK)
    mix_ref[:, M_QR:M_QR + RET_QK] = q.astype(bf16)
    mix_ref[:, M_QD:M_QD + RET_QK] = (q * qdec_ref[...]).astype(bf16)
    k = proj(W_KR, RET_QK)
    mix_ref[:, M_KR:M_KR + RET_QK] = k.astype(bf16)
    kd = k * kdec_ref[...]
    for ci in range(CHUNKS_PER_TILE):
        for hd in range(RET_HEADS):
            kdt_ref[ci, hd * RET_DK:(hd + 1) * RET_DK, :] = (
                kd[ci * C:(ci + 1) * C, hd * RET_DK:(hd + 1) * RET_DK].T.astype(bf16))
    mix_ref[:, M_QS:M_QS + SWA_Q] = proj(W_QS, SWA_Q).astype(bf16)
    kv_s = proj(W_KS, 2 * SWA_KV)
    ks = kv_s[:, 0:SWA_KV] * S_SCALE
    lane = lax.broadcasted_iota(jnp.int32, (TOK_TILE, SWA_KV), 1)
    for grp in range(SWA_GROUPS):
        in_group = (lane >= grp * SWA_DH) & (lane < (grp + 1) * SWA_DH)
        mix_ref[:, M_KG + grp * SWA_KV:M_KG + (grp + 1) * SWA_KV] = jnp.where(in_group, ks, 0.0).astype(bf16)
    mix_ref[:, M_VS:M_VS + SWA_KV] = kv_s[:, SWA_KV:2 * SWA_KV].astype(bf16)
    mix_ref[:, M_VR:M_VR + RET_V] = proj(W_VR, RET_V).astype(bf16)


def _in_proj(x2d, gain, w_in):
    n = x2d.shape[0]
    nt = n // TOK_TILE
    const = lambda i: (0, 0)
    return pl.pallas_call(
        _in_proj_kernel,
        out_shape=(jax.ShapeDtypeStruct((n, D_MIXIN), bf16),
                   jax.ShapeDtypeStruct((n // CHUNK, RET_QK, CHUNK), bf16),
                   jax.ShapeDtypeStruct((n, 2 * D_MODEL), bf16)),
        grid=(nt,),
        in_specs=[
            pl.BlockSpec((TOK_TILE, D_MODEL), lambda i: (i, 0)),
            pl.BlockSpec((1, D_MODEL), const),
            pl.BlockSpec((D_MODEL, D_IN), const, pipeline_mode=pl.Buffered(1)),
        ],
        out_specs=(
            pl.BlockSpec((TOK_TILE, D_MIXIN), lambda i: (i, 0)),
            pl.BlockSpec((CHUNKS_PER_TILE, RET_QK, CHUNK), lambda i: (i, 0, 0)),
            pl.BlockSpec((TOK_TILE, 2 * D_MODEL), lambda i: (i, 0)),
        ),
        scratch_shapes=[
            pltpu.VMEM((TOK_TILE, RET_QK), f32),
            pltpu.VMEM((TOK_TILE, RET_QK), f32),
        ],
        compiler_params=pltpu.CompilerParams(
            dimension_semantics=("arbitrary",), vmem_limit_bytes=VMEM_LIMIT_BYTES),
        name="in_proj",
    )(x2d, gain, w_in)


def _mixer_kernel(qr_ref, kr_ref, vr_ref, sg_ref, qd_ref, qs_ref, kg0_ref, kg1_ref, vs_ref, kdt_ref, sink_ref,
                  ret_ref, swa_ref,
                  state_ref, kk_ref, vv_ref, dmat_ref, bias_ref):
    b = pl.program_id(0)
    c = pl.program_id(1)
    C = CHUNK

    @pl.when((b == 0) & (c == 0))
    def _init_tables():
        i = lax.broadcasted_iota(jnp.int32, (C, C), 0).astype(f32)
        j = lax.broadcasted_iota(jnp.int32, (C, C), 1).astype(f32)
        diff = i - j
        for h in range(RET_HEADS):
            dmat_ref[h] = jnp.where(diff >= 0, jnp.exp(LOG_GAMMA[h] * jnp.maximum(diff, 0.0)), 0.0) * K_SCALE
        row = lax.broadcasted_iota(jnp.int32, (SWA_REP * C, 2 * C), 0)
        kpos = lax.broadcasted_iota(jnp.int32, (SWA_REP * C, 2 * C), 1)
        rep = row // C
        dist = (row - rep * C + C) - kpos
        valid = (dist >= 0) & (dist < C)
        distf = dist.astype(f32)
        for g in range(SWA_GROUPS):
            slope = jnp.zeros((SWA_REP * C, 2 * C), f32)
            for r in range(SWA_REP):
                slope = jnp.where(rep == r, ALIBI_SLOPES[g * SWA_REP + r], slope)
            bias_ref[g] = jnp.where(valid, -slope * distf, -jnp.inf)
            bias_ref[SWA_GROUPS + g] = jnp.where(valid & (kpos >= C), -slope * distf, -jnp.inf)
        vv_ref[:, SWA_KV:2 * SWA_KV] = jnp.ones((2 * C, SWA_KV), bf16)

    @pl.when(c == 0)
    def _reset_sequence_state():
        state_ref[...] = jnp.zeros_like(state_ref)
        kk_ref[:, 0:C, :] = jnp.zeros((SWA_GROUPS, C, SWA_KV), bf16)
        vv_ref[0:C, 0:SWA_KV] = jnp.zeros((C, SWA_KV), bf16)

    qks, kvs = [], []
    for h in range(RET_HEADS):
        q = qr_ref[:, h * RET_DK:(h + 1) * RET_DK]
        k = kr_ref[:, h * RET_DK:(h + 1) * RET_DK]
        v = vr_ref[:, h * RET_DV:(h + 1) * RET_DV]
        qks.append(lax.dot_general(q, k, (((1,), (1,)), ((), ())), preferred_element_type=f32))
        kvs.append(jnp.dot(kdt_ref[0, h * RET_DK:(h + 1) * RET_DK, :], v, preferred_element_type=f32))
    outs = []
    for h in range(RET_HEADS):
        v = vr_ref[:, h * RET_DV:(h + 1) * RET_DV]
        s_prev = state_ref[h]
        a = (qks[h] * dmat_ref[h]).astype(bf16)
        outs.append(jnp.dot(jnp.concatenate([a, qd_ref[:, h * RET_DK:(h + 1) * RET_DK]], axis=1),
                            jnp.concatenate([v, s_prev.astype(bf16)], axis=0), preferred_element_type=f32))
        state_ref[h] = float(np.exp(LOG_GAMMA[h] * C)) * s_prev + kvs[h]
    for h in range(RET_HEADS):
        o = outs[h]
        mu = jnp.mean(o, axis=-1, keepdims=True)
        d = o - mu
        var = jnp.mean(d * d, axis=-1, keepdims=True)
        ret_ref[:, h * RET_DV:(h + 1) * RET_DV] = (
            sg_ref[:, h * RET_DV:(h + 1) * RET_DV].astype(f32) * (d * lax.rsqrt(var + EPS))).astype(bf16)

    kk_ref[0, C:2 * C, :] = kg0_ref[...]
    kk_ref[1, C:2 * C, :] = kg1_ref[...]
    vv_ref[C:2 * C, 0:SWA_KV] = vs_ref[...]
    vv_ones = vv_ref[...]
    table0 = jnp.where(c == 0, SWA_GROUPS, 0)
    first_group = lax.broadcasted_iota(jnp.int32, (C, SWA_KV), 1) < SWA_DH
    units = [(j, g) for j in range(SWA_REP) for g in range(SWA_GROUPS)]
    scores = {}
    for j, g in units:
        s = lax.dot_general(qs_ref[:, j * C:(j + 1) * C], kk_ref[g], (((1,), (1,)), ((), ())),
                            preferred_element_type=f32)
        scores[j, g] = s + bias_ref[table0 + g, j * C:(j + 1) * C, :]
    num, den = {}, {}
    for j, g in units:
        sink = sink_ref[g * SWA_REP + j]
        s = scores[j, g]
        m = jnp.maximum(jnp.max(s, axis=-1, keepdims=True), sink)
        p = jnp.exp(s - m)
        r = jnp.dot(p.astype(bf16), vv_ones, preferred_element_type=f32)
        num[j, g] = r[:, 0:SWA_KV]
        den[j, g] = r[:, SWA_KV:2 * SWA_KV] + jnp.exp(sink - m)
    for j in range(SWA_REP):
        o_pair = (jnp.where(first_group, num[j, 0], num[j, 1])
                  / jnp.where(first_group, den[j, 0], den[j, 1]))
        swa_ref[:, j * C:(j + 1) * C] = o_pair.astype(bf16)
    kk_ref[0, 0:C, :] = kg0_ref[...]
    kk_ref[1, 0:C, :] = kg1_ref[...]
    vv_ref[0:C, 0:SWA_KV] = vs_ref[...]


def _mixer(mix, kdt, sinks, batch, seq):
    n = mix.shape[0]
    nc = seq // CHUNK
    C = CHUNK

    def rows(col_block):
        return lambda b, c: (b * nc + c, col_block)

    return pl.pallas_call(
        _mixer_kernel,
        out_shape=(jax.ShapeDtypeStruct((n, RET_V), bf16), jax.ShapeDtypeStruct((n, SWA_Q), bf16)),
        grid=(batch, nc),
        in_specs=[
            pl.BlockSpec((C, RET_QK), rows(M_QR // RET_QK)),
            pl.BlockSpec((C, RET_QK), rows(M_KR // RET_QK)),
            pl.BlockSpec((C, RET_V), rows(M_VR // RET_V)),
            pl.BlockSpec((C, RET_V), rows(M_SG // RET_V)),
            pl.BlockSpec((C, RET_QK), rows(M_QD // RET_QK)),
            pl.BlockSpec((C, SWA_Q), rows(M_QS // SWA_Q)),
            pl.BlockSpec((C, SWA_KV), rows(M_KG // SWA_KV)),
            pl.BlockSpec((C, SWA_KV), rows(M_KG // SWA_KV + 1)),
            pl.BlockSpec((C, SWA_KV), rows(M_VS // SWA_KV)),
            pl.BlockSpec((1, RET_QK, C), lambda b, c: (b * nc + c, 0, 0)),
            pl.BlockSpec(memory_space=pltpu.SMEM),
        ],
        out_specs=(
            pl.BlockSpec((C, RET_V), rows(0)),
            pl.BlockSpec((C, SWA_Q), rows(0)),
        ),
        scratch_shapes=[
            pltpu.VMEM((RET_HEADS, RET_DK, RET_DV), f32),
            pltpu.VMEM((SWA_GROUPS, 2 * C, SWA_KV), bf16),
            pltpu.VMEM((2 * C, 2 * SWA_KV), bf16),
            pltpu.VMEM((RET_HEADS, C, C), f32),
            pltpu.VMEM((2 * SWA_GROUPS, SWA_REP * C, 2 * C), f32),
        ],
        compiler_params=pltpu.CompilerParams(
            dimension_semantics=("arbitrary", "arbitrary"), vmem_limit_bytes=VMEM_LIMIT_BYTES),
        name="mixer",
    )(mix, mix, mix, mix, mix, mix, mix, mix, mix, kdt, sinks)


def _post_mix_kernel(x_ref, ret_ref, swa_ref, gr_ref, gs_ref, wro_ref, wso_ref, wo_ref, g_ref, o_ref):
    y_r = jnp.dot(ret_ref[...], wro_ref[...], preferred_element_type=f32)
    y_s = jnp.dot(swa_ref[...], wso_ref[...], preferred_element_type=f32)
    merged = gr_ref[...].astype(f32) * y_r + gs_ref[...].astype(f32) * y_s
    z = jnp.dot(merged.astype(bf16), wo_ref[...], preferred_element_type=f32)
    o_ref[...] = x_ref[...] + z * _rms_scale(z) * g_ref[...]


def _post_mix(x2d, ret, swa, gates, w_ro, w_so, w_o, gain, *, tm=TOK_TILE):
    n = x2d.shape[0]
    const = lambda i: (0, 0)
    return pl.pallas_call(
        _post_mix_kernel,
        out_shape=jax.ShapeDtypeStruct((n, D_MODEL), f32),
        grid=(n // tm,),
        in_specs=[
            pl.BlockSpec((tm, D_MODEL), lambda i: (i, 0)),
            pl.BlockSpec((tm, RET_V), lambda i: (i, 0)),
            pl.BlockSpec((tm, SWA_Q), lambda i: (i, 0)),
            pl.BlockSpec((tm, D_MODEL), lambda i: (i, 0)),
            pl.BlockSpec((tm, D_MODEL), lambda i: (i, 1)),
            pl.BlockSpec((RET_V, D_MODEL), const, pipeline_mode=pl.Buffered(1)),
            pl.BlockSpec((SWA_Q, D_MODEL), const, pipeline_mode=pl.Buffered(1)),
            pl.BlockSpec((D_MODEL, D_MODEL), const, pipeline_mode=pl.Buffered(1)),
            pl.BlockSpec((1, D_MODEL), const),
        ],
        out_specs=pl.BlockSpec((tm, D_MODEL), lambda i: (i, 0)),
        compiler_params=pltpu.CompilerParams(
            dimension_semantics=("arbitrary",), vmem_limit_bytes=VMEM_LIMIT_BYTES),
        name="post_mix",
    )(x2d, ret, swa, gates, gates, w_ro, w_so, w_o, gain)


def _mlp_kernel(x_ref, gpre_ref, wup_ref, wdn_ref, gpost_ref, o_ref, *, ff_chunk):
    x = x_ref[...]
    h = (x * _rms_scale(x) * gpre_ref[...]).astype(bf16)
    acc = None
    for c0 in range(0, D_FF, ff_chunk):
        u = jnp.dot(h, wup_ref[:, c0:c0 + ff_chunk], preferred_element_type=f32)
        u = jnp.maximum(u, 0.0)
        part = jnp.dot((u * u).astype(bf16), wdn_ref[c0:c0 + ff_chunk, :], preferred_element_type=f32)
        acc = part if acc is None else acc + part
    o_ref[...] = x + acc * _rms_scale(acc) * gpost_ref[...]


def _mlp(x2d, gpre, w_up, w_dn, gpost, *, tm=TOK_TILE, ff_chunk=1024):
    n = x2d.shape[0]
    const = lambda i: (0, 0)
    return pl.pallas_call(
        functools.partial(_mlp_kernel, ff_chunk=ff_chunk),
        out_shape=jax.ShapeDtypeStruct((n, D_MODEL), f32),
        grid=(n // tm,),
        in_specs=[
            pl.BlockSpec((tm, D_MODEL), lambda i: (i, 0)),
            pl.BlockSpec((1, D_MODEL), const),
            pl.BlockSpec((D_MODEL, D_FF), const, pipeline_mode=pl.Buffered(1)),
            pl.BlockSpec((D_FF, D_MODEL), const, pipeline_mode=pl.Buffered(1)),
            pl.BlockSpec((1, D_MODEL), const),
        ],
        out_specs=pl.BlockSpec((tm, D_MODEL), lambda i: (i, 0)),
        compiler_params=pltpu.CompilerParams(
            dimension_semantics=("arbitrary",), vmem_limit_bytes=VMEM_LIMIT_BYTES),
        name="mlp",
    )(x2d, gpre, w_up, w_dn, gpost)


def kernel(x, pre_mix_norm, w_in, w_ret_out, w_swa_out, w_out, sinks, post_mix_norm, pre_mlp_norm, w_up,
           w_down, post_mlp_norm):
    batch, seq, d = x.shape
    depth = w_in.shape[0]
    assert d == D_MODEL and seq % TOK_TILE == 0
    col_order = _in_proj_column_order()
    row_order = _swa_out_row_order()
    x2d = x.reshape(batch * seq, d)
    for l in range(depth):
        mix, kdt, gates = _in_proj(x2d, pre_mix_norm[l][None, :], w_in[l][:, col_order].astype(bf16))
        ret, swa = _mixer(mix, kdt, sinks[l].astype(f32), batch, seq)
        x2d = _post_mix(x2d, ret, swa, gates, w_ret_out[l].astype(bf16),
                        w_swa_out[l][row_order, :].astype(bf16), w_out[l].astype(bf16),
                        post_mix_norm[l][None, :])
        x2d = _mlp(x2d, pre_mlp_norm[l][None, :], w_up[l].astype(bf16), w_down[l].astype(bf16),
                   post_mlp_norm[l][None, :])
    return x2d.reshape(batch, seq, d)
```

```python
import functools

import numpy as np
import jax
import jax.numpy as jnp
from jax import lax
from jax.experimental import pallas as pl
from jax.experimental.pallas import tpu as pltpu

D_MODEL = 1024
RET_HEADS = 4
RET_DK = 128
RET_DV = 256
CHUNK = 128
SWA_HEADS = 8
SWA_GROUPS = 2
SWA_REP = SWA_HEADS // SWA_GROUPS
SWA_DH = 64
D_FF = 4 * D_MODEL
EPS = 1e-6

RET_QK = RET_HEADS * RET_DK
RET_V = RET_HEADS * RET_DV
SWA_Q = SWA_HEADS * SWA_DH
SWA_KV = SWA_GROUPS * SWA_DH

W_QR = 0
W_KR = W_QR + RET_QK
W_VR = W_KR + RET_QK
W_GR = W_VR + RET_V
W_QS = W_GR + RET_V
W_KS = W_QS + SWA_Q
W_VS = W_KS + SWA_KV
W_GATES = W_VS + SWA_KV
D_IN = W_GATES + 2 * D_MODEL

M_QR = 0
M_KR = M_QR + RET_QK
M_VR = M_KR + RET_QK
M_SG = M_VR + RET_V
M_QD = M_SG + RET_V
M_QS = M_QD + RET_QK
M_KG = M_QS + SWA_Q
M_VS = M_KG + SWA_GROUPS * SWA_KV
D_MIXIN = M_VS + SWA_KV

TOK_TILE = 512
CHUNKS_PER_TILE = TOK_TILE // CHUNK

VMEM_LIMIT_BYTES = 56 * 1024 * 1024

LOG_GAMMA = [float(np.log(1.0 - 2.0 ** (-5.0 - h))) for h in range(RET_HEADS)]
ALIBI_SLOPES = [float(2.0 ** (-8.0 / SWA_HEADS * (h + 1))) for h in range(SWA_HEADS)]
K_SCALE = float(RET_DK ** -0.5)
S_SCALE = float(SWA_DH ** -0.5)

f32 = jnp.float32
bf16 = jnp.bfloat16


def _in_proj_column_order():
    ref_splits = (RET_QK, RET_QK, RET_V, RET_V, SWA_Q, SWA_KV, SWA_KV, D_MODEL, D_MODEL)
    starts = np.concatenate([[0], np.cumsum(ref_splits)[:-1]])
    q_r, k_r, v_r, g_r, q_s, k_s, v_s, gate_r, gate_s = [
        np.arange(s, s + n) for s, n in zip(starts, ref_splits)]
    q_s = q_s.reshape(SWA_GROUPS, SWA_REP, SWA_DH).transpose(1, 0, 2).reshape(-1)
    return np.concatenate([q_r, k_r, v_r, g_r, q_s, k_s, v_s, gate_r, gate_s])


def _swa_out_row_order():
    return np.arange(SWA_Q).reshape(SWA_GROUPS, SWA_REP, SWA_DH).transpose(1, 0, 2).reshape(-1)


def _rms_scale(v):
    return lax.rsqrt(jnp.mean(v * v, axis=-1, keepdims=True) + EPS)


def _in_proj_kernel(x_ref, g_ref, w_ref, mix_ref, kdt_ref, gates_ref, qdec_ref, kdec_ref):
    C = CHUNK

    @pl.when(pl.program_id(0) == 0)
    def _init_decay_tables():
        pos = (lax.broadcasted_iota(jnp.int32, (TOK_TILE, RET_DK), 0) % C).astype(f32)
        for h in range(RET_HEADS):
            qdec_ref[:, h * RET_DK:(h + 1) * RET_DK] = jnp.exp(LOG_GAMMA[h] * (pos + 1.0))
            kdec_ref[:, h * RET_DK:(h + 1) * RET_DK] = jnp.exp(LOG_GAMMA[h] * (C - 1.0 - pos)) * K_SCALE

    x = x_ref[...]
    row_scale = _rms_scale(x)
    xg = (x * g_ref[...]).astype(bf16)

    def proj(c0, width):
        return jnp.dot(xg, w_ref[:, c0:c0 + width], preferred_element_type=f32) * row_scale

    for c0 in range(0, 2 * D_MODEL, D_MODEL):
        gates_ref[:, c0:c0 + D_MODEL] = jax.nn.sigmoid(proj(W_GATES + c0, D_MODEL)).astype(bf16)
    g = proj(W_GR, RET_V)
    mix_ref[:, M_SG:M_SG + RET_V] = (g * jax.nn.sigmoid(g)).astype(bf16)
    q = proj(W_QR, RET_QK)
    mix_ref[:, M_QR:M_QR + RET_QK] = q.astype(bf16)
    mix_ref[:, M_QD:M_QD + RET_QK] = (q * qdec_ref[...]).astype(bf16)
    k = proj(W_KR, RET_QK)
    mix_ref[:, M_KR:M_KR + RET_QK] = k.astype(bf16)
    kd = k * kdec_ref[...]
    for ci in range(CHUNKS_PER_TILE):
        for hd in range(RET_HEADS):
            kdt_ref[ci, hd * RET_DK:(hd + 1) * RET_DK, :] = (
                kd[ci * C:(ci + 1) * C, hd * RET_DK:(hd + 1) * RET_DK].T.astype(bf16))
    mix_ref[:, M_QS:M_QS + SWA_Q] = proj(W_QS, SWA_Q).astype(bf16)
    kv_s = proj(W_KS, 2 * SWA_KV)
    ks = kv_s[:, 0:SWA_KV] * S_SCALE
    lane = lax.broadcasted_iota(jnp.int32, (TOK_TILE, SWA_KV), 1)
    for grp in range(SWA_GROUPS):
        in_group = (lane >= grp * SWA_DH) & (lane < (grp + 1) * SWA_DH)
        mix_ref[:, M_KG + grp * SWA_KV:M_KG + (grp + 1) * SWA_KV] = jnp.where(in_group, ks, 0.0).astype(bf16)
    mix_ref[:, M_VS:M_VS + SWA_KV] = kv_s[:, SWA_KV:2 * SWA_KV].astype(bf16)
    mix_ref[:, M_VR:M_VR + RET_V] = proj(W_VR, RET_V).astype(bf16)


def _in_proj(x2d, gain, w_in):
    n = x2d.shape[0]
    nt = n // TOK_TILE
    const = lambda i: (0, 0)
    return pl.pallas_call(
        _in_proj_kernel,
        out_shape=(jax.ShapeDtypeStruct((n, D_MIXIN), bf16),
                   jax.ShapeDtypeStruct((n // CHUNK, RET_QK, CHUNK), bf16),
                   jax.ShapeDtypeStruct((n, 2 * D_MODEL), bf16)),
        grid=(nt,),
        in_specs=[
            pl.BlockSpec((TOK_TILE, D_MODEL), lambda i: (i, 0)),
            pl.BlockSpec((1, D_MODEL), const),
            pl.BlockSpec((D_MODEL, D_IN), const, pipeline_mode=pl.Buffered(1)),
        ],
        out_specs=(
            pl.BlockSpec((TOK_TILE, D_MIXIN), lambda i: (i, 0)),
            pl.BlockSpec((CHUNKS_PER_TILE, RET_QK, CHUNK), lambda i: (i, 0, 0)),
            pl.BlockSpec((TOK_TILE, 2 * D_MODEL), lambda i: (i, 0)),
        ),
        scratch_shapes=[
            pltpu.VMEM((TOK_TILE, RET_QK), f32),
            pltpu.VMEM((TOK_TILE, RET_QK), f32),
        ],
        compiler_params=pltpu.CompilerParams(
            dimension_semantics=("arbitrary",), vmem_limit_bytes=VMEM_LIMIT_BYTES),
        name="in_proj",
    )(x2d, gain, w_in)


def _mixer_kernel(qr_ref, kr_ref, vr_ref, sg_ref, qd_ref, qs_ref, kg0_ref, kg1_ref, vs_ref, kdt_ref, sink_ref,
                  ret_ref, swa_ref,
                  state_ref, kprev_ref, vprev_ref, dmat_ref, bias_ref):
    b = pl.program_id(0)
    c = pl.program_id(1)
    C = CHUNK

    @pl.when((b == 0) & (c == 0))
    def _init_tables():
        i = lax.broadcasted_iota(jnp.int32, (C, C), 0).astype(f32)
        j = lax.broadcasted_iota(jnp.int32, (C, C), 1).astype(f32)
        diff = i - j
        for h in range(RET_HEADS):
            dmat_ref[h] = jnp.where(diff >= 0, jnp.exp(LOG_GAMMA[h] * jnp.maximum(diff, 0.0)), 0.0) * K_SCALE
        row = lax.broadcasted_iota(jnp.int32, (SWA_REP * C, 2 * C), 0)
        kpos = lax.broadcasted_iota(jnp.int32, (SWA_REP * C, 2 * C), 1)
        rep = row // C
        dist = (row - rep * C + C) - kpos
        valid = (dist >= 0) & (dist < C)
        distf = dist.astype(f32)
        for g in range(SWA_GROUPS):
            slope = jnp.zeros((SWA_REP * C, 2 * C), f32)
            for r in range(SWA_REP):
                slope = jnp.where(rep == r, ALIBI_SLOPES[g * SWA_REP + r], slope)
            bias_ref[g] = jnp.where(valid, -slope * distf, -jnp.inf)
            bias_ref[SWA_GROUPS + g] = jnp.where(valid & (kpos >= C), -slope * distf, -jnp.inf)

    @pl.when(c == 0)
    def _reset_sequence_state():
        state_ref[...] = jnp.zeros_like(state_ref)
        kprev_ref[...] = jnp.zeros_like(kprev_ref)
        vprev_ref[...] = jnp.zeros_like(vprev_ref)

    kg_refs = (kg0_ref, kg1_ref)
    first_group = lax.broadcasted_iota(jnp.int32, (C, SWA_KV), 1) < SWA_DH
    ones_cols = jnp.ones((2 * C, SWA_KV), bf16)
    units = [(j, g) for j in range(SWA_REP) for g in range(SWA_GROUPS)]
    live = [dict() for _ in range(CHUNKS_PER_TILE)]

    def rows(t):
        return slice(t * C, (t + 1) * C)

    def head(h, width):
        return slice(h * width, (h + 1) * width)

    def first_matmuls(t):
        st = live[t]
        st["qk"] = [lax.dot_general(qr_ref[rows(t), head(h, RET_DK)], kr_ref[rows(t), head(h, RET_DK)],
                                    (((1,), (1,)), ((), ())), preferred_element_type=f32)
                    for h in range(RET_HEADS)]
        st["kv"] = [jnp.dot(kdt_ref[t, head(h, RET_DK), :], vr_ref[rows(t), head(h, RET_DV)],
                            preferred_element_type=f32) for h in range(RET_HEADS)]
        if t == 0:
            keys = [jnp.concatenate([kprev_ref[g], kg_refs[g][0:C, :]], axis=0) for g in range(SWA_GROUPS)]
            vals = jnp.concatenate([vprev_ref[...], vs_ref[0:C, :]], axis=0)
            table0 = jnp.where(c == 0, SWA_GROUPS, 0)
        else:
            keys = [kg_refs[g][(t - 1) * C:(t + 1) * C, :] for g in range(SWA_GROUPS)]
            vals = vs_ref[(t - 1) * C:(t + 1) * C, :]
            table0 = 0
        st["vv_ones"] = jnp.concatenate([vals, ones_cols], axis=1)
        st["s"] = {}
        for j, g in units:
            s = lax.dot_general(qs_ref[rows(t), head(j, C)], keys[g], (((1,), (1,)), ((), ())),
                                preferred_element_type=f32)
            st["s"][j, g] = s + bias_ref[table0 + g, head(j, C), :]

    def first_vector_work(t):
        st = live[t]
        st["a"], st["state_bf16"] = [], []
        for h in range(RET_HEADS):
            s_prev = state_ref[h]
            st["a"].append((st["qk"][h] * dmat_ref[h]).astype(bf16))
            st["state_bf16"].append(s_prev.astype(bf16))
            state_ref[h] = float(np.exp(LOG_GAMMA[h] * C)) * s_prev + st["kv"][h]
        st["p"], st["sink_term"] = {}, {}
        for j, g in units:
            sink = sink_ref[g * SWA_REP + j]
            s = st["s"][j, g]
            m = jnp.maximum(jnp.max(s, axis=-1, keepdims=True), sink)
            st["p"][j, g] = jnp.exp(s - m).astype(bf16)
            st["sink_term"][j, g] = jnp.exp(sink - m)
        del st["qk"], st["kv"], st["s"]

    def second_matmuls(t):
        st = live[t]
        st["o"] = [jnp.dot(jnp.concatenate([st["a"][h], qd_ref[rows(t), head(h, RET_DK)]], axis=1),
                           jnp.concatenate([vr_ref[rows(t), head(h, RET_DV)], st["state_bf16"][h]], axis=0),
                           preferred_element_type=f32) for h in range(RET_HEADS)]
        st["r"] = {u: jnp.dot(st["p"][u], st["vv_ones"], preferred_element_type=f32) for u in units}
        del st["a"], st["state_bf16"], st["p"], st["vv_ones"]

    def second_vector_work(t):
        st = live[t]
        for h in range(RET_HEADS):
            o = st["o"][h]
            mu = jnp.mean(o, axis=-1, keepdims=True)
            d = o - mu
            var = jnp.mean(d * d, axis=-1, keepdims=True)
            ret_ref[rows(t), head(h, RET_DV)] = (
                sg_ref[rows(t), head(h, RET_DV)].astype(f32) * (d * lax.rsqrt(var + EPS))).astype(bf16)
        for j in range(SWA_REP):
            num = [st["r"][j, g][:, 0:SWA_KV] for g in range(SWA_GROUPS)]
            den = [st["r"][j, g][:, SWA_KV:2 * SWA_KV] + st["sink_term"][j, g] for g in range(SWA_GROUPS)]
            o_pair = jnp.where(first_group, num[0], num[1]) / jnp.where(first_group, den[0], den[1])
            swa_ref[rows(t), head(j, C)] = o_pair.astype(bf16)
        st.clear()

    for t in range(CHUNKS_PER_TILE + 1):
        if t < CHUNKS_PER_TILE:
            first_matmuls(t)
        if t >= 1:
            first_vector_work(t - 1)
            second_matmuls(t - 1)
            second_vector_work(t - 1)
    last = slice(TOK_TILE - C, TOK_TILE)
    for g in range(SWA_GROUPS):
        kprev_ref[g] = kg_refs[g][last, :]
    vprev_ref[...] = vs_ref[last, :]


def _mixer(mix, kdt, sinks, batch, seq):
    n = mix.shape[0]
    nt = seq // TOK_TILE
    C = CHUNK

    def rows(col_block):
        return lambda b, c: (b * nt + c, col_block)

    return pl.pallas_call(
        _mixer_kernel,
        out_shape=(jax.ShapeDtypeStruct((n, RET_V), bf16), jax.ShapeDtypeStruct((n, SWA_Q), bf16)),
        grid=(batch, nt),
        in_specs=[
            pl.BlockSpec((TOK_TILE, RET_QK), rows(M_QR // RET_QK)),
            pl.BlockSpec((TOK_TILE, RET_QK), rows(M_KR // RET_QK)),
            pl.BlockSpec((TOK_TILE, RET_V), rows(M_VR // RET_V)),
            pl.BlockSpec((TOK_TILE, RET_V), rows(M_SG // RET_V)),
            pl.BlockSpec((TOK_TILE, RET_QK), rows(M_QD // RET_QK)),
            pl.BlockSpec((TOK_TILE, SWA_Q), rows(M_QS // SWA_Q)),
            pl.BlockSpec((TOK_TILE, SWA_KV), rows(M_KG // SWA_KV)),
            pl.BlockSpec((TOK_TILE, SWA_KV), rows(M_KG // SWA_KV + 1)),
            pl.BlockSpec((TOK_TILE, SWA_KV), rows(M_VS // SWA_KV)),
            pl.BlockSpec((CHUNKS_PER_TILE, RET_QK, C), lambda b, c: (b * nt + c, 0, 0)),
            pl.BlockSpec(memory_space=pltpu.SMEM),
        ],
        out_specs=(
            pl.BlockSpec((TOK_TILE, RET_V), rows(0)),
            pl.BlockSpec((TOK_TILE, SWA_Q), rows(0)),
        ),
        scratch_shapes=[
            pltpu.VMEM((RET_HEADS, RET_DK, RET_DV), f32),
            pltpu.VMEM((SWA_GROUPS, C, SWA_KV), bf16),
            pltpu.VMEM((C, SWA_KV), bf16),
            pltpu.VMEM((RET_HEADS, C, C), f32),
            pltpu.VMEM((2 * SWA_GROUPS, SWA_REP * C, 2 * C), f32),
        ],
        compiler_params=pltpu.CompilerParams(
            dimension_semantics=("arbitrary", "arbitrary"), vmem_limit_bytes=VMEM_LIMIT_BYTES),
        name="mixer",
    )(mix, mix, mix, mix, mix, mix, mix, mix, mix, kdt, sinks)


def _post_mix_kernel(x_ref, ret_ref, swa_ref, gr_ref, gs_ref, wro_ref, wso_ref, wo_ref, g_ref, o_ref):
    y_r = jnp.dot(ret_ref[...], wro_ref[...], preferred_element_type=f32)
    y_s = jnp.dot(swa_ref[...], wso_ref[...], preferred_element_type=f32)
    merged = gr_ref[...].astype(f32) * y_r + gs_ref[...].astype(f32) * y_s
    z = jnp.dot(merged.astype(bf16), wo_ref[...], preferred_element_type=f32)
    o_ref[...] = x_ref[...] + z * _rms_scale(z) * g_ref[...]


def _post_mix(x2d, ret, swa, gates, w_ro, w_so, w_o, gain, *, tm=TOK_TILE):
    n = x2d.shape[0]
    const = lambda i: (0, 0)
    return pl.pallas_call(
        _post_mix_kernel,
        out_shape=jax.ShapeDtypeStruct((n, D_MODEL), f32),
        grid=(n // tm,),
        in_specs=[
            pl.BlockSpec((tm, D_MODEL), lambda i: (i, 0)),
            pl.BlockSpec((tm, RET_V), lambda i: (i, 0)),
            pl.BlockSpec((tm, SWA_Q), lambda i: (i, 0)),
            pl.BlockSpec((tm, D_MODEL), lambda i: (i, 0)),
            pl.BlockSpec((tm, D_MODEL), lambda i: (i, 1)),
            pl.BlockSpec((RET_V, D_MODEL), const, pipeline_mode=pl.Buffered(1)),
            pl.BlockSpec((SWA_Q, D_MODEL), const, pipeline_mode=pl.Buffered(1)),
            pl.BlockSpec((D_MODEL, D_MODEL), const, pipeline_mode=pl.Buffered(1)),
            pl.BlockSpec((1, D_MODEL), const),
        ],
        out_specs=pl.BlockSpec((tm, D_MODEL), lambda i: (i, 0)),
        compiler_params=pltpu.CompilerParams(
            dimension_semantics=("arbitrary",), vmem_limit_bytes=VMEM_LIMIT_BYTES),
        name="post_mix",
    )(x2d, ret, swa, gates, gates, w_ro, w_so, w_o, gain)


def _mlp_kernel(x_ref, gpre_ref, wup_ref, wdn_ref, gpost_ref, o_ref, *, ff_chunk):
    x = x_ref[...]
    h = (x * _rms_scale(x) * gpre_ref[...]).astype(bf16)
    acc = None
    for c0 in range(0, D_FF, ff_chunk):
        u = jnp.dot(h, wup_ref[:, c0:c0 + ff_chunk], preferred_element_type=f32)
        u = jnp.maximum(u, 0.0)
        part = jnp.dot((u * u).astype(bf16), wdn_ref[c0:c0 + ff_chunk, :], preferred_element_type=f32)
        acc = part if acc is None else acc + part
    o_ref[...] = x + acc * _rms_scale(acc) * gpost_ref[...]


def _mlp(x2d, gpre, w_up, w_dn, gpost, *, tm=TOK_TILE, ff_chunk=1024):
    n = x2d.shape[0]
    const = lambda i: (0, 0)
    return pl.pallas_call(
        functools.partial(_mlp_kernel, ff_chunk=ff_chunk),
        out_shape=jax.ShapeDtypeStruct((n, D_MODEL), f32),
        grid=(n // tm,),
        in_specs=[
            pl.BlockSpec((tm, D_MODEL), lambda i: (i, 0)),
            pl.BlockSpec((1, D_MODEL), const),
            pl.BlockSpec((D_MODEL, D_FF), const, pipeline_mode=pl.Buffered(1)),
            pl.BlockSpec((D_FF, D_MODEL), const, pipeline_mode=pl.Buffered(1)),
            pl.BlockSpec((1, D_MODEL), const),
        ],
        out_specs=pl.BlockSpec((tm, D_MODEL), lambda i: (i, 0)),
        compiler_params=pltpu.CompilerParams(
            dimension_semantics=("arbitrary",), vmem_limit_bytes=VMEM_LIMIT_BYTES),
        name="mlp",
    )(x2d, gpre, w_up, w_dn, gpost)


def kernel(x, pre_mix_norm, w_in, w_ret_out, w_swa_out, w_out, sinks, post_mix_norm, pre_mlp_norm, w_up,
           w_down, post_mlp_norm):
    batch, seq, d = x.shape
    depth = w_in.shape[0]
    assert d == D_MODEL and seq % TOK_TILE == 0
    col_order = _in_proj_column_order()
    row_order = _swa_out_row_order()
    x2d = x.reshape(batch * seq, d)
    for l in range(depth):
        mix, kdt, gates = _in_proj(x2d, pre_mix_norm[l][None, :], w_in[l][:, col_order].astype(bf16))
        ret, swa = _mixer(mix, kdt, sinks[l].astype(f32), batch, seq)
        x2d = _post_mix(x2d, ret, swa, gates, w_ret_out[l].astype(bf16),
                        w_swa_out[l][row_order, :].astype(bf16), w_out[l].astype(bf16),
                        post_mix_norm[l][None, :])
        x2d = _mlp(x2d, pre_mlp_norm[l][None, :], w_up[l].astype(bf16), w_down[l].astype(bf16),
                   post_mlp_norm[l][None, :])
    return x2d.reshape(batch, seq, d)
```

```python
import functools

import numpy as np
import jax
import jax.numpy as jnp
from jax import lax
from jax.experimental import pallas as pl
from jax.experimental.pallas import tpu as pltpu

D_MODEL = 1024
RET_HEADS = 4
RET_DK = 128
RET_DV = 256
CHUNK = 128
SWA_HEADS = 8
SWA_GROUPS = 2
SWA_REP = SWA_HEADS // SWA_GROUPS
SWA_DH = 64
D_FF = 4 * D_MODEL
EPS = 1e-6

RET_QK = RET_HEADS * RET_DK
RET_V = RET_HEADS * RET_DV
SWA_Q = SWA_HEADS * SWA_DH
SWA_KV = SWA_GROUPS * SWA_DH

W_QR = 0
W_KR = W_QR + RET_QK
W_VR = W_KR + RET_QK
W_GR = W_VR + RET_V
W_QS = W_GR + RET_V
W_KS = W_QS + SWA_Q
W_VS = W_KS + SWA_KV
W_GATES = W_VS + SWA_KV
D_IN = W_GATES + 2 * D_MODEL

M_QR = 0
M_KR = M_QR + RET_QK
M_VR = M_KR + RET_QK
M_SG = M_VR + RET_V
M_QD = M_SG + RET_V
M_QS = M_QD + RET_QK
M_KG = M_QS + SWA_Q
M_VS = M_KG + SWA_GROUPS * SWA_KV
D_MIXIN = M_VS + SWA_KV

TOK_TILE = 512
CHUNKS_PER_TILE = TOK_TILE // CHUNK

VMEM_LIMIT_BYTES = 56 * 1024 * 1024

LOG_GAMMA = [float(np.log(1.0 - 2.0 ** (-5.0 - h))) for h in range(RET_HEADS)]
ALIBI_SLOPES = [float(2.0 ** (-8.0 / SWA_HEADS * (h + 1))) for h in range(SWA_HEADS)]
K_SCALE = float(RET_DK ** -0.5)
S_SCALE = float(SWA_DH ** -0.5)
LOG2E = float(np.log2(np.e))

f32 = jnp.float32
bf16 = jnp.bfloat16


def _in_proj_column_order():
    ref_splits = (RET_QK, RET_QK, RET_V, RET_V, SWA_Q, SWA_KV, SWA_KV, D_MODEL, D_MODEL)
    starts = np.concatenate([[0], np.cumsum(ref_splits)[:-1]])
    q_r, k_r, v_r, g_r, q_s, k_s, v_s, gate_r, gate_s = [
        np.arange(s, s + n) for s, n in zip(starts, ref_splits)]
    q_s = q_s.reshape(SWA_GROUPS, SWA_REP, SWA_DH).transpose(1, 0, 2).reshape(-1)
    return np.concatenate([q_r, k_r, v_r, g_r, q_s, k_s, v_s, gate_r, gate_s])


def _swa_out_row_order():
    return np.arange(SWA_Q).reshape(SWA_GROUPS, SWA_REP, SWA_DH).transpose(1, 0, 2).reshape(-1)


def _rms_scale(v):
    return lax.rsqrt(jnp.mean(v * v, axis=-1, keepdims=True) + EPS)


def _in_proj_kernel(x_ref, g_ref, w_ref, mix_ref, kdt_ref, gates_ref, qdec_ref, kdec_ref):
    C = CHUNK

    @pl.when(pl.program_id(0) == 0)
    def _init_decay_tables():
        pos = (lax.broadcasted_iota(jnp.int32, (TOK_TILE, RET_DK), 0) % C).astype(f32)
        for h in range(RET_HEADS):
            qdec_ref[:, h * RET_DK:(h + 1) * RET_DK] = jnp.exp(LOG_GAMMA[h] * (pos + 1.0))
            kdec_ref[:, h * RET_DK:(h + 1) * RET_DK] = jnp.exp(LOG_GAMMA[h] * (C - 1.0 - pos)) * K_SCALE

    x = x_ref[...]
    row_scale = _rms_scale(x)
    xg = (x * g_ref[...]).astype(bf16)

    def proj(c0, width):
        return jnp.dot(xg, w_ref[:, c0:c0 + width], preferred_element_type=f32) * row_scale

    for c0 in range(0, 2 * D_MODEL, D_MODEL):
        gates_ref[:, c0:c0 + D_MODEL] = jax.nn.sigmoid(proj(W_GATES + c0, D_MODEL)).astype(bf16)
    g = proj(W_GR, RET_V)
    mix_ref[:, M_SG:M_SG + RET_V] = (g * jax.nn.sigmoid(g)).astype(bf16)
    q = proj(W_QR, RET_QK)
    mix_ref[:, M_QR:M_QR + RET_QK] = q.astype(bf16)
    mix_ref[:, M_QD:M_QD + RET_QK] = (q * qdec_ref[...]).astype(bf16)
    k = proj(W_KR, RET_QK)
    mix_ref[:, M_KR:M_KR + RET_QK] = k.astype(bf16)
    kd = k * kdec_ref[...]
    for ci in range(CHUNKS_PER_TILE):
        for hd in range(RET_HEADS):
            kdt_ref[ci, hd * RET_DK:(hd + 1) * RET_DK, :] = (
                kd[ci * C:(ci + 1) * C, hd * RET_DK:(hd + 1) * RET_DK].T.astype(bf16))
    mix_ref[:, M_QS:M_QS + SWA_Q] = proj(W_QS, SWA_Q).astype(bf16)
    kv_s = proj(W_KS, 2 * SWA_KV)
    ks = kv_s[:, 0:SWA_KV] * (S_SCALE * LOG2E)
    lane = lax.broadcasted_iota(jnp.int32, (TOK_TILE, SWA_KV), 1)
    for grp in range(SWA_GROUPS):
        in_group = (lane >= grp * SWA_DH) & (lane < (grp + 1) * SWA_DH)
        mix_ref[:, M_KG + grp * SWA_KV:M_KG + (grp + 1) * SWA_KV] = jnp.where(in_group, ks, 0.0).astype(bf16)
    mix_ref[:, M_VS:M_VS + SWA_KV] = kv_s[:, SWA_KV:2 * SWA_KV].astype(bf16)
    mix_ref[:, M_VR:M_VR + RET_V] = proj(W_VR, RET_V).astype(bf16)


def _in_proj(x2d, gain, w_in):
    n = x2d.shape[0]
    nt = n // TOK_TILE
    const = lambda i: (0, 0)
    return pl.pallas_call(
        _in_proj_kernel,
        out_shape=(jax.ShapeDtypeStruct((n, D_MIXIN), bf16),
                   jax.ShapeDtypeStruct((n // CHUNK, RET_QK, CHUNK), bf16),
                   jax.ShapeDtypeStruct((n, 2 * D_MODEL), bf16)),
        grid=(nt,),
        in_specs=[
            pl.BlockSpec((TOK_TILE, D_MODEL), lambda i: (i, 0)),
            pl.BlockSpec((1, D_MODEL), const),
            pl.BlockSpec((D_MODEL, D_IN), const, pipeline_mode=pl.Buffered(1)),
        ],
        out_specs=(
            pl.BlockSpec((TOK_TILE, D_MIXIN), lambda i: (i, 0)),
            pl.BlockSpec((CHUNKS_PER_TILE, RET_QK, CHUNK), lambda i: (i, 0, 0)),
            pl.BlockSpec((TOK_TILE, 2 * D_MODEL), lambda i: (i, 0)),
        ),
        scratch_shapes=[
            pltpu.VMEM((TOK_TILE, RET_QK), f32),
            pltpu.VMEM((TOK_TILE, RET_QK), f32),
        ],
        compiler_params=pltpu.CompilerParams(
            dimension_semantics=("arbitrary",), vmem_limit_bytes=VMEM_LIMIT_BYTES),
        name="in_proj",
    )(x2d, gain, w_in)


def _mixer_kernel(qr_ref, kr_ref, vr_ref, sg_ref, qd_ref, qs_ref, kg0_ref, kg1_ref, vs_ref, kdt_ref, sink_ref,
                  ret_ref, swa_ref,
                  state_ref, kprev_ref, vprev_ref, dmat_ref, bias_ref):
    b = pl.program_id(0)
    c = pl.program_id(1)
    C = CHUNK

    @pl.when((b == 0) & (c == 0))
    def _init_tables():
        i = lax.broadcasted_iota(jnp.int32, (C, C), 0).astype(f32)
        j = lax.broadcasted_iota(jnp.int32, (C, C), 1).astype(f32)
        diff = i - j
        for h in range(RET_HEADS):
            dmat_ref[h] = jnp.where(diff >= 0, jnp.exp(LOG_GAMMA[h] * jnp.maximum(diff, 0.0)), 0.0) * K_SCALE
        row = lax.broadcasted_iota(jnp.int32, (SWA_REP * C, 2 * C), 0)
        kpos = lax.broadcasted_iota(jnp.int32, (SWA_REP * C, 2 * C), 1)
        rep = row // C
        dist = (row - rep * C + C) - kpos
        valid = (dist >= 0) & (dist < C)
        distf = dist.astype(f32)
        for g in range(SWA_GROUPS):
            slope = jnp.zeros((SWA_REP * C, 2 * C), f32)
            sink = jnp.zeros((SWA_REP * C, 2 * C), f32)
            for r in range(SWA_REP):
                slope = jnp.where(rep == r, ALIBI_SLOPES[g * SWA_REP + r], slope)
                sink = jnp.where(rep == r, sink_ref[g * SWA_REP + r], sink)
            alibi = -slope * distf
            bias_ref[g] = jnp.where(kpos == 0, sink, jnp.where(valid, alibi, -jnp.inf)) * LOG2E
            bias_ref[SWA_GROUPS + g] = jnp.where(
                kpos == 0, sink, jnp.where(valid & (kpos >= C), alibi, -jnp.inf)) * LOG2E

    @pl.when(c == 0)
    def _reset_sequence_state():
        state_ref[...] = jnp.zeros_like(state_ref)
        kprev_ref[...] = jnp.zeros_like(kprev_ref)
        vprev_ref[...] = jnp.zeros_like(vprev_ref)

    def zero_first_row(block):
        rows_per_vreg = 16
        top = block[0:rows_per_vreg, :]
        is_first = lax.broadcasted_iota(jnp.int32, top.shape, 0) == 0
        return jnp.concatenate([jnp.where(is_first, jnp.zeros_like(top), top), block[rows_per_vreg:, :]], axis=0)

    kg_refs = (kg0_ref, kg1_ref)
    first_group = lax.broadcasted_iota(jnp.int32, (C, SWA_KV), 1) < SWA_DH
    ones_cols = jnp.ones((2 * C, SWA_KV), bf16)
    units = [(j, g) for j in range(SWA_REP) for g in range(SWA_GROUPS)]
    live = [dict() for _ in range(CHUNKS_PER_TILE)]

    def rows(t):
        return slice(t * C, (t + 1) * C)

    def head(h, width):
        return slice(h * width, (h + 1) * width)

    def first_matmuls(t):
        st = live[t]
        st["qk"] = [lax.dot_general(qr_ref[rows(t), head(h, RET_DK)], kr_ref[rows(t), head(h, RET_DK)],
                                    (((1,), (1,)), ((), ())), preferred_element_type=f32)
                    for h in range(RET_HEADS)]
        st["kv"] = [jnp.dot(kdt_ref[t, head(h, RET_DK), :], vr_ref[rows(t), head(h, RET_DV)],
                            preferred_element_type=f32) for h in range(RET_HEADS)]
        if t == 0:
            keys = [jnp.concatenate([kprev_ref[g], kg_refs[g][0:C, :]], axis=0) for g in range(SWA_GROUPS)]
            vals = jnp.concatenate([vprev_ref[...], vs_ref[0:C, :]], axis=0)
            table0 = jnp.where(c == 0, SWA_GROUPS, 0)
        else:
            keys = [kg_refs[g][(t - 1) * C:(t + 1) * C, :] for g in range(SWA_GROUPS)]
            vals = vs_ref[(t - 1) * C:(t + 1) * C, :]
            table0 = 0
        keys = [zero_first_row(k) for k in keys]
        st["vv_ones"] = jnp.concatenate([zero_first_row(vals), ones_cols], axis=1)
        st["s"] = {}
        for j, g in units:
            s = lax.dot_general(qs_ref[rows(t), head(j, C)], keys[g], (((1,), (1,)), ((), ())),
                                preferred_element_type=f32)
            st["s"][j, g] = s + bias_ref[table0 + g, head(j, C), :]

    def first_vector_work(t):
        st = live[t]
        st["a"], st["state_bf16"] = [], []
        for h in range(RET_HEADS):
            s_prev = state_ref[h]
            st["a"].append((st["qk"][h] * dmat_ref[h]).astype(bf16))
            st["state_bf16"].append(s_prev.astype(bf16))
            state_ref[h] = float(np.exp(LOG_GAMMA[h] * C)) * s_prev + st["kv"][h]
        st["p"] = {}
        for u in units:
            s = st["s"][u]
            st["p"][u] = jnp.exp2(s - jnp.max(s, axis=-1, keepdims=True)).astype(bf16)
        del st["qk"], st["kv"], st["s"]

    def second_matmuls(t):
        st = live[t]
        st["o"] = [jnp.dot(jnp.concatenate([st["a"][h], qd_ref[rows(t), head(h, RET_DK)]], axis=1),
                           jnp.concatenate([vr_ref[rows(t), head(h, RET_DV)], st["state_bf16"][h]], axis=0),
                           preferred_element_type=f32) for h in range(RET_HEADS)]
        st["r"] = {u: jnp.dot(st["p"][u], st["vv_ones"], preferred_element_type=f32) for u in units}
        del st["a"], st["state_bf16"], st["p"], st["vv_ones"]

    def second_vector_work(t):
        st = live[t]
        for h in range(RET_HEADS):
            o = st["o"][h]
            mu = jnp.mean(o, axis=-1, keepdims=True)
            d = o - mu
            var = jnp.mean(d * d, axis=-1, keepdims=True)
            ret_ref[rows(t), head(h, RET_DV)] = (
                sg_ref[rows(t), head(h, RET_DV)].astype(f32) * (d * lax.rsqrt(var + EPS))).astype(bf16)
        for j in range(SWA_REP):
            num = [st["r"][j, g][:, 0:SWA_KV] for g in range(SWA_GROUPS)]
            den = [st["r"][j, g][:, SWA_KV:2 * SWA_KV] for g in range(SWA_GROUPS)]
            o_pair = jnp.where(first_group, num[0], num[1]) / jnp.where(first_group, den[0], den[1])
            swa_ref[rows(t), head(j, C)] = o_pair.astype(bf16)
        st.clear()

    for t in range(CHUNKS_PER_TILE + 1):
        if t < CHUNKS_PER_TILE:
            first_matmuls(t)
        if t >= 1:
            first_vector_work(t - 1)
            second_matmuls(t - 1)
            second_vector_work(t - 1)
    last = slice(TOK_TILE - C, TOK_TILE)
    for g in range(SWA_GROUPS):
        kprev_ref[g] = kg_refs[g][last, :]
    vprev_ref[...] = vs_ref[last, :]


def _mixer(mix, kdt, sinks, batch, seq):
    n = mix.shape[0]
    nt = seq // TOK_TILE
    C = CHUNK

    def rows(col_block):
        return lambda b, c: (b * nt + c, col_block)

    return pl.pallas_call(
        _mixer_kernel,
        out_shape=(jax.ShapeDtypeStruct((n, RET_V), bf16), jax.ShapeDtypeStruct((n, SWA_Q), bf16)),
        grid=(batch, nt),
        in_specs=[
            pl.BlockSpec((TOK_TILE, RET_QK), rows(M_QR // RET_QK)),
            pl.BlockSpec((TOK_TILE, RET_QK), rows(M_KR // RET_QK)),
            pl.BlockSpec((TOK_TILE, RET_V), rows(M_VR // RET_V)),
            pl.BlockSpec((TOK_TILE, RET_V), rows(M_SG // RET_V)),
            pl.BlockSpec((TOK_TILE, RET_QK), rows(M_QD // RET_QK)),
            pl.BlockSpec((TOK_TILE, SWA_Q), rows(M_QS // SWA_Q)),
            pl.BlockSpec((TOK_TILE, SWA_KV), rows(M_KG // SWA_KV)),
            pl.BlockSpec((TOK_TILE, SWA_KV), rows(M_KG // SWA_KV + 1)),
            pl.BlockSpec((TOK_TILE, SWA_KV), rows(M_VS // SWA_KV)),
            pl.BlockSpec((CHUNKS_PER_TILE, RET_QK, C), lambda b, c: (b * nt + c, 0, 0)),
            pl.BlockSpec(memory_space=pltpu.SMEM),
        ],
        out_specs=(
            pl.BlockSpec((TOK_TILE, RET_V), rows(0)),
            pl.BlockSpec((TOK_TILE, SWA_Q), rows(0)),
        ),
        scratch_shapes=[
            pltpu.VMEM((RET_HEADS, RET_DK, RET_DV), f32),
            pltpu.VMEM((SWA_GROUPS, C, SWA_KV), bf16),
            pltpu.VMEM((C, SWA_KV), bf16),
            pltpu.VMEM((RET_HEADS, C, C), f32),
            pltpu.VMEM((2 * SWA_GROUPS, SWA_REP * C, 2 * C), f32),
        ],
        compiler_params=pltpu.CompilerParams(
            dimension_semantics=("arbitrary", "arbitrary"), vmem_limit_bytes=VMEM_LIMIT_BYTES),
        name="mixer",
    )(mix, mix, mix, mix, mix, mix, mix, mix, mix, kdt, sinks)


def _post_mix_kernel(x_ref, ret_ref, swa_ref, gr_ref, gs_ref, wro_ref, wso_ref, wo_ref, g_ref, o_ref):
    y_r = jnp.dot(ret_ref[...], wro_ref[...], preferred_element_type=f32)
    y_s = jnp.dot(swa_ref[...], wso_ref[...], preferred_element_type=f32)
    merged = gr_ref[...].astype(f32) * y_r + gs_ref[...].astype(f32) * y_s
    z = jnp.dot(merged.astype(bf16), wo_ref[...], preferred_element_type=f32)
    o_ref[...] = x_ref[...] + z * _rms_scale(z) * g_ref[...]


def _post_mix(x2d, ret, swa, gates, w_ro, w_so, w_o, gain, *, tm=TOK_TILE):
    n = x2d.shape[0]
    const = lambda i: (0, 0)
    return pl.pallas_call(
        _post_mix_kernel,
        out_shape=jax.ShapeDtypeStruct((n, D_MODEL), f32),
        grid=(n // tm,),
        in_specs=[
            pl.BlockSpec((tm, D_MODEL), lambda i: (i, 0)),
            pl.BlockSpec((tm, RET_V), lambda i: (i, 0)),
            pl.BlockSpec((tm, SWA_Q), lambda i: (i, 0)),
            pl.BlockSpec((tm, D_MODEL), lambda i: (i, 0)),
            pl.BlockSpec((tm, D_MODEL), lambda i: (i, 1)),
            pl.BlockSpec((RET_V, D_MODEL), const, pipeline_mode=pl.Buffered(1)),
            pl.BlockSpec((SWA_Q, D_MODEL), const, pipeline_mode=pl.Buffered(1)),
            pl.BlockSpec((D_MODEL, D_MODEL), const, pipeline_mode=pl.Buffered(1)),
            pl.BlockSpec((1, D_MODEL), const),
        ],
        out_specs=pl.BlockSpec((tm, D_MODEL), lambda i: (i, 0)),
        compiler_params=pltpu.CompilerParams(
            dimension_semantics=("arbitrary",), vmem_limit_bytes=VMEM_LIMIT_BYTES),
        name="post_mix",
    )(x2d, ret, swa, gates, gates, w_ro, w_so, w_o, gain)


def _mlp_kernel(x_ref, gpre_ref, wup_ref, wdn_ref, gpost_ref, o_ref, *, ff_chunk):
    x = x_ref[...]
    h = (x * _rms_scale(x) * gpre_ref[...]).astype(bf16)
    acc = None
    for c0 in range(0, D_FF, ff_chunk):
        u = jnp.dot(h, wup_ref[:, c0:c0 + ff_chunk], preferred_element_type=f32)
        u = jnp.maximum(u, 0.0)
        part = jnp.dot((u * u).astype(bf16), wdn_ref[c0:c0 + ff_chunk, :], preferred_element_type=f32)
        acc = part if acc is None else acc + part
    o_ref[...] = x + acc * _rms_scale(acc) * gpost_ref[...]


def _mlp(x2d, gpre, w_up, w_dn, gpost, *, tm=TOK_TILE, ff_chunk=1024):
    n = x2d.shape[0]
    const = lambda i: (0, 0)
    return pl.pallas_call(
        functools.partial(_mlp_kernel, ff_chunk=ff_chunk),
        out_shape=jax.ShapeDtypeStruct((n, D_MODEL), f32),
        grid=(n // tm,),
        in_specs=[
            pl.BlockSpec((tm, D_MODEL), lambda i: (i, 0)),
            pl.BlockSpec((1, D_MODEL), const),
            pl.BlockSpec((D_MODEL, D_FF), const, pipeline_mode=pl.Buffered(1)),
            pl.BlockSpec((D_FF, D_MODEL), const, pipeline_mode=pl.Buffered(1)),
            pl.BlockSpec((1, D_MODEL), const),
        ],
        out_specs=pl.BlockSpec((tm, D_MODEL), lambda i: (i, 0)),
        compiler_params=pltpu.CompilerParams(
            dimension_semantics=("arbitrary",), vmem_limit_bytes=VMEM_LIMIT_BYTES),
        name="mlp",
    )(x2d, gpre, w_up, w_dn, gpost)


def kernel(x, pre_mix_norm, w_in, w_ret_out, w_swa_out, w_out, sinks, post_mix_norm, pre_mlp_norm, w_up,
           w_down, post_mlp_norm):
    batch, seq, d = x.shape
    depth = w_in.shape[0]
    assert d == D_MODEL and seq % TOK_TILE == 0
    col_order = _in_proj_column_order()
    row_order = _swa_out_row_order()
    x2d = x.reshape(batch * seq, d)
    for l in range(depth):
        mix, kdt, gates = _in_proj(x2d, pre_mix_norm[l][None, :], w_in[l][:, col_order].astype(bf16))
        ret, swa = _mixer(mix, kdt, sinks[l].astype(f32), batch, seq)
        x2d = _post_mix(x2d, ret, swa, gates, w_ret_out[l].astype(bf16),
                        w_swa_out[l][row_order, :].astype(bf16), w_out[l].astype(bf16),
                        post_mix_norm[l][None, :])
        x2d = _mlp(x2d, pre_mlp_norm[l][None, :], w_up[l].astype(bf16), w_down[l].astype(bf16),
                   post_mlp_norm[l][None, :])
    return x2d.reshape(batch, seq, d)
```

```python
import functools

import numpy as np
import jax
import jax.numpy as jnp
from jax import lax
from jax.experimental import pallas as pl
from jax.experimental.pallas import tpu as pltpu

D_MODEL = 1024
RET_HEADS = 4
RET_DK = 128
RET_DV = 256
CHUNK = 128
SWA_HEADS = 8
SWA_GROUPS = 2
SWA_REP = SWA_HEADS // SWA_GROUPS
SWA_DH = 64
D_FF = 4 * D_MODEL
EPS = 1e-6

RET_QK = RET_HEADS * RET_DK
RET_V = RET_HEADS * RET_DV
SWA_Q = SWA_HEADS * SWA_DH
SWA_KV = SWA_GROUPS * SWA_DH

W_QR = 0
W_KR = W_QR + RET_QK
W_VR = W_KR + RET_QK
W_GR = W_VR + RET_V
W_QS = W_GR + RET_V
W_KS = W_QS + SWA_Q
W_VS = W_KS + SWA_KV
W_GATES = W_VS + SWA_KV
D_IN = W_GATES + 2 * D_MODEL

M_QR = 0
M_KR = M_QR + RET_QK
M_VR = M_KR + RET_QK
M_SG = M_VR + RET_V
M_QD = M_SG + RET_V
M_QS = M_QD + RET_QK
M_KG = M_QS + SWA_Q
M_VS = M_KG + SWA_GROUPS * SWA_KV
D_MIXIN = M_VS + SWA_KV

TOK_TILE = 512
CHUNKS_PER_TILE = TOK_TILE // CHUNK

VMEM_LIMIT_BYTES = 56 * 1024 * 1024

LOG_GAMMA = [float(np.log(1.0 - 2.0 ** (-5.0 - h))) for h in range(RET_HEADS)]
ALIBI_SLOPES = [float(2.0 ** (-8.0 / SWA_HEADS * (h + 1))) for h in range(SWA_HEADS)]
K_SCALE = float(RET_DK ** -0.5)
S_SCALE = float(SWA_DH ** -0.5)
LOG2E = float(np.log2(np.e))

f32 = jnp.float32
bf16 = jnp.bfloat16


def _in_proj_column_order():
    ref_splits = (RET_QK, RET_QK, RET_V, RET_V, SWA_Q, SWA_KV, SWA_KV, D_MODEL, D_MODEL)
    starts = np.concatenate([[0], np.cumsum(ref_splits)[:-1]])
    q_r, k_r, v_r, g_r, q_s, k_s, v_s, gate_r, gate_s = [
        np.arange(s, s + n) for s, n in zip(starts, ref_splits)]
    q_s = q_s.reshape(SWA_GROUPS, SWA_REP, SWA_DH).transpose(1, 0, 2).reshape(-1)
    return np.concatenate([q_r, k_r, v_r, g_r, q_s, k_s, v_s, gate_r, gate_s])


def _swa_out_row_order():
    return np.arange(SWA_Q).reshape(SWA_GROUPS, SWA_REP, SWA_DH).transpose(1, 0, 2).reshape(-1)


def _take_runs(w, order, axis):
    breaks = np.flatnonzero(np.diff(order) != 1) + 1
    pieces = [lax.slice_in_dim(w, int(run[0]), int(run[-1]) + 1, axis=axis) for run in np.split(order, breaks)]
    return jnp.concatenate(pieces, axis=axis)


def _rms_scale(v):
    return lax.rsqrt(jnp.mean(v * v, axis=-1, keepdims=True) + EPS)


def _in_proj_kernel(x_ref, g_ref, w_ref, mix_ref, kdt_ref, gates_ref, qdec_ref, kdec_ref):
    C = CHUNK

    @pl.when(pl.program_id(0) == 0)
    def _init_decay_tables():
        pos = (lax.broadcasted_iota(jnp.int32, (TOK_TILE, RET_DK), 0) % C).astype(f32)
        for h in range(RET_HEADS):
            qdec_ref[:, h * RET_DK:(h + 1) * RET_DK] = jnp.exp(LOG_GAMMA[h] * (pos + 1.0))
            kdec_ref[:, h * RET_DK:(h + 1) * RET_DK] = jnp.exp(LOG_GAMMA[h] * (C - 1.0 - pos)) * K_SCALE

    x = x_ref[...]
    row_scale = _rms_scale(x)
    xg = (x * g_ref[...]).astype(bf16)

    def proj(c0, width):
        return jnp.dot(xg, w_ref[:, c0:c0 + width], preferred_element_type=f32) * row_scale

    for c0 in range(0, 2 * D_MODEL, D_MODEL):
        gates_ref[:, c0:c0 + D_MODEL] = jax.nn.sigmoid(proj(W_GATES + c0, D_MODEL)).astype(bf16)
    g = proj(W_GR, RET_V)
    mix_ref[:, M_SG:M_SG + RET_V] = (g * jax.nn.sigmoid(g)).astype(bf16)
    q = proj(W_QR, RET_QK)
    mix_ref[:, M_QR:M_QR + RET_QK] = q.astype(bf16)
    mix_ref[:, M_QD:M_QD + RET_QK] = (q * qdec_ref[...]).astype(bf16)
    k = proj(W_KR, RET_QK)
    mix_ref[:, M_KR:M_KR + RET_QK] = k.astype(bf16)
    kd = k * kdec_ref[...]
    for ci in range(CHUNKS_PER_TILE):
        for hd in range(RET_HEADS):
            kdt_ref[ci, hd * RET_DK:(hd + 1) * RET_DK, :] = (
                kd[ci * C:(ci + 1) * C, hd * RET_DK:(hd + 1) * RET_DK].T.astype(bf16))
    mix_ref[:, M_QS:M_QS + SWA_Q] = proj(W_QS, SWA_Q).astype(bf16)
    kv_s = proj(W_KS, 2 * SWA_KV)
    ks = kv_s[:, 0:SWA_KV] * (S_SCALE * LOG2E)
    lane = lax.broadcasted_iota(jnp.int32, (TOK_TILE, SWA_KV), 1)
    for grp in range(SWA_GROUPS):
        in_group = (lane >= grp * SWA_DH) & (lane < (grp + 1) * SWA_DH)
        mix_ref[:, M_KG + grp * SWA_KV:M_KG + (grp + 1) * SWA_KV] = jnp.where(in_group, ks, 0.0).astype(bf16)
    mix_ref[:, M_VS:M_VS + SWA_KV] = kv_s[:, SWA_KV:2 * SWA_KV].astype(bf16)
    mix_ref[:, M_VR:M_VR + RET_V] = proj(W_VR, RET_V).astype(bf16)


def _in_proj(x2d, gain, w_in):
    n = x2d.shape[0]
    nt = n // TOK_TILE
    const = lambda i: (0, 0)
    return pl.pallas_call(
        _in_proj_kernel,
        out_shape=(jax.ShapeDtypeStruct((n, D_MIXIN), bf16),
                   jax.ShapeDtypeStruct((n // CHUNK, RET_QK, CHUNK), bf16),
                   jax.ShapeDtypeStruct((n, 2 * D_MODEL), bf16)),
        grid=(nt,),
        in_specs=[
            pl.BlockSpec((TOK_TILE, D_MODEL), lambda i: (i, 0)),
            pl.BlockSpec((1, D_MODEL), const),
            pl.BlockSpec((D_MODEL, D_IN), const, pipeline_mode=pl.Buffered(1)),
        ],
        out_specs=(
            pl.BlockSpec((TOK_TILE, D_MIXIN), lambda i: (i, 0)),
            pl.BlockSpec((CHUNKS_PER_TILE, RET_QK, CHUNK), lambda i: (i, 0, 0)),
            pl.BlockSpec((TOK_TILE, 2 * D_MODEL), lambda i: (i, 0)),
        ),
        scratch_shapes=[
            pltpu.VMEM((TOK_TILE, RET_QK), f32),
            pltpu.VMEM((TOK_TILE, RET_QK), f32),
        ],
        compiler_params=pltpu.CompilerParams(
            dimension_semantics=("arbitrary",), vmem_limit_bytes=VMEM_LIMIT_BYTES),
        name="in_proj",
    )(x2d, gain, w_in)


def _mixer_kernel(qr_ref, kr_ref, vr_ref, sg_ref, qd_ref, qs_ref, kg0_ref, kg1_ref, vs_ref, kdt_ref, sink_ref,
                  ret_ref, swa_ref,
                  state_ref, kprev_ref, vprev_ref, dmat_ref, bias_ref):
    b = pl.program_id(0)
    c = pl.program_id(1)
    C = CHUNK

    @pl.when((b == 0) & (c == 0))
    def _init_tables():
        i = lax.broadcasted_iota(jnp.int32, (C, C), 0).astype(f32)
        j = lax.broadcasted_iota(jnp.int32, (C, C), 1).astype(f32)
        diff = i - j
        for h in range(RET_HEADS):
            dmat_ref[h] = jnp.where(diff >= 0, jnp.exp(LOG_GAMMA[h] * jnp.maximum(diff, 0.0)), 0.0) * K_SCALE
        row = lax.broadcasted_iota(jnp.int32, (SWA_REP * C, 2 * C), 0)
        kpos = lax.broadcasted_iota(jnp.int32, (SWA_REP * C, 2 * C), 1)
        rep = row // C
        dist = (row - rep * C + C) - kpos
        valid = (dist >= 0) & (dist < C)
        distf = dist.astype(f32)
        for g in range(SWA_GROUPS):
            slope = jnp.zeros((SWA_REP * C, 2 * C), f32)
            sink = jnp.zeros((SWA_REP * C, 2 * C), f32)
            for r in range(SWA_REP):
                slope = jnp.where(rep == r, ALIBI_SLOPES[g * SWA_REP + r], slope)
                sink = jnp.where(rep == r, sink_ref[g * SWA_REP + r], sink)
            alibi = -slope * distf
            bias_ref[g] = jnp.where(kpos == 0, sink, jnp.where(valid, alibi, -jnp.inf)) * LOG2E
            bias_ref[SWA_GROUPS + g] = jnp.where(
                kpos == 0, sink, jnp.where(valid & (kpos >= C), alibi, -jnp.inf)) * LOG2E

    @pl.when(c == 0)
    def _reset_sequence_state():
        state_ref[...] = jnp.zeros_like(state_ref)
        kprev_ref[...] = jnp.zeros_like(kprev_ref)
        vprev_ref[...] = jnp.zeros_like(vprev_ref)

    def zero_first_row(block):
        rows_per_vreg = 16
        top = block[0:rows_per_vreg, :]
        is_first = lax.broadcasted_iota(jnp.int32, top.shape, 0) == 0
        return jnp.concatenate([jnp.where(is_first, jnp.zeros_like(top), top), block[rows_per_vreg:, :]], axis=0)

    kg_refs = (kg0_ref, kg1_ref)
    first_group = lax.broadcasted_iota(jnp.int32, (C, SWA_KV), 1) < SWA_DH
    ones_cols = jnp.ones((2 * C, SWA_KV), bf16)
    units = [(j, g) for j in range(SWA_REP) for g in range(SWA_GROUPS)]
    live = [dict() for _ in range(CHUNKS_PER_TILE)]

    def rows(t):
        return slice(t * C, (t + 1) * C)

    def head(h, width):
        return slice(h * width, (h + 1) * width)

    def first_matmuls(t):
        st = live[t]
        st["qk"] = [lax.dot_general(qr_ref[rows(t), head(h, RET_DK)], kr_ref[rows(t), head(h, RET_DK)],
                                    (((1,), (1,)), ((), ())), preferred_element_type=f32)
                    for h in range(RET_HEADS)]
        st["kv"] = [jnp.dot(kdt_ref[t, head(h, RET_DK), :], vr_ref[rows(t), head(h, RET_DV)],
                            preferred_element_type=f32) for h in range(RET_HEADS)]
        if t == 0:
            keys = [jnp.concatenate([kprev_ref[g], kg_refs[g][0:C, :]], axis=0) for g in range(SWA_GROUPS)]
            vals = jnp.concatenate([vprev_ref[...], vs_ref[0:C, :]], axis=0)
            table0 = jnp.where(c == 0, SWA_GROUPS, 0)
        else:
            keys = [kg_refs[g][(t - 1) * C:(t + 1) * C, :] for g in range(SWA_GROUPS)]
            vals = vs_ref[(t - 1) * C:(t + 1) * C, :]
            table0 = 0
        keys = [zero_first_row(k) for k in keys]
        st["vv_ones"] = jnp.concatenate([zero_first_row(vals), ones_cols], axis=1)
        st["s"] = {}
        for j, g in units:
            s = lax.dot_general(qs_ref[rows(t), head(j, C)], keys[g], (((1,), (1,)), ((), ())),
                                preferred_element_type=f32)
            st["s"][j, g] = s + bias_ref[table0 + g, head(j, C), :]

    def first_vector_work(t):
        st = live[t]
        st["a"], st["state_bf16"] = [], []
        for h in range(RET_HEADS):
            s_prev = state_ref[h]
            st["a"].append((st["qk"][h] * dmat_ref[h]).astype(bf16))
            st["state_bf16"].append(s_prev.astype(bf16))
            state_ref[h] = float(np.exp(LOG_GAMMA[h] * C)) * s_prev + st["kv"][h]
        st["p"] = {}
        for u in units:
            s = st["s"][u]
            st["p"][u] = jnp.exp2(s - jnp.max(s, axis=-1, keepdims=True)).astype(bf16)
        del st["qk"], st["kv"], st["s"]

    def second_matmuls(t):
        st = live[t]
        st["o"] = [jnp.dot(jnp.concatenate([st["a"][h], qd_ref[rows(t), head(h, RET_DK)]], axis=1),
                           jnp.concatenate([vr_ref[rows(t), head(h, RET_DV)], st["state_bf16"][h]], axis=0),
                           preferred_element_type=f32) for h in range(RET_HEADS)]
        st["r"] = {u: jnp.dot(st["p"][u], st["vv_ones"], preferred_element_type=f32) for u in units}
        del st["a"], st["state_bf16"], st["p"], st["vv_ones"]

    def second_vector_work(t):
        st = live[t]
        for h in range(RET_HEADS):
            o = st["o"][h]
            mu = jnp.mean(o, axis=-1, keepdims=True)
            d = o - mu
            var = jnp.mean(d * d, axis=-1, keepdims=True)
            ret_ref[rows(t), head(h, RET_DV)] = (
                sg_ref[rows(t), head(h, RET_DV)].astype(f32) * (d * lax.rsqrt(var + EPS))).astype(bf16)
        for j in range(SWA_REP):
            num = [st["r"][j, g][:, 0:SWA_KV] for g in range(SWA_GROUPS)]
            den = [st["r"][j, g][:, SWA_KV:2 * SWA_KV] for g in range(SWA_GROUPS)]
            o_pair = jnp.where(first_group, num[0], num[1]) / jnp.where(first_group, den[0], den[1])
            swa_ref[rows(t), head(j, C)] = o_pair.astype(bf16)
        st.clear()

    for t in range(CHUNKS_PER_TILE + 1):
        if t < CHUNKS_PER_TILE:
            first_matmuls(t)
        if t >= 1:
            first_vector_work(t - 1)
            second_matmuls(t - 1)
            second_vector_work(t - 1)
    last = slice(TOK_TILE - C, TOK_TILE)
    for g in range(SWA_GROUPS):
        kprev_ref[g] = kg_refs[g][last, :]
    vprev_ref[...] = vs_ref[last, :]


def _mixer(mix, kdt, sinks, batch, seq):
    n = mix.shape[0]
    nt = seq // TOK_TILE
    C = CHUNK

    def rows(col_block):
        return lambda b, c: (b * nt + c, col_block)

    return pl.pallas_call(
        _mixer_kernel,
        out_shape=(jax.ShapeDtypeStruct((n, RET_V), bf16), jax.ShapeDtypeStruct((n, SWA_Q), bf16)),
        grid=(batch, nt),
        in_specs=[
            pl.BlockSpec((TOK_TILE, RET_QK), rows(M_QR // RET_QK)),
            pl.BlockSpec((TOK_TILE, RET_QK), rows(M_KR // RET_QK)),
            pl.BlockSpec((TOK_TILE, RET_V), rows(M_VR // RET_V)),
            pl.BlockSpec((TOK_TILE, RET_V), rows(M_SG // RET_V)),
            pl.BlockSpec((TOK_TILE, RET_QK), rows(M_QD // RET_QK)),
            pl.BlockSpec((TOK_TILE, SWA_Q), rows(M_QS // SWA_Q)),
            pl.BlockSpec((TOK_TILE, SWA_KV), rows(M_KG // SWA_KV)),
            pl.BlockSpec((TOK_TILE, SWA_KV), rows(M_KG // SWA_KV + 1)),
            pl.BlockSpec((TOK_TILE, SWA_KV), rows(M_VS // SWA_KV)),
            pl.BlockSpec((CHUNKS_PER_TILE, RET_QK, C), lambda b, c: (b * nt + c, 0, 0)),
            pl.BlockSpec(memory_space=pltpu.SMEM),
        ],
        out_specs=(
            pl.BlockSpec((TOK_TILE, RET_V), rows(0)),
            pl.BlockSpec((TOK_TILE, SWA_Q), rows(0)),
        ),
        scratch_shapes=[
            pltpu.VMEM((RET_HEADS, RET_DK, RET_DV), f32),
            pltpu.VMEM((SWA_GROUPS, C, SWA_KV), bf16),
            pltpu.VMEM((C, SWA_KV), bf16),
            pltpu.VMEM((RET_HEADS, C, C), f32),
            pltpu.VMEM((2 * SWA_GROUPS, SWA_REP * C, 2 * C), f32),
        ],
        compiler_params=pltpu.CompilerParams(
            dimension_semantics=("arbitrary", "arbitrary"), vmem_limit_bytes=VMEM_LIMIT_BYTES),
        name="mixer",
    )(mix, mix, mix, mix, mix, mix, mix, mix, mix, kdt, sinks)


def _post_mix_kernel(x_ref, ret_ref, swa_ref, gr_ref, gs_ref, wro_ref, wso_ref, wo_ref, g_ref, o_ref, *,
                     sub_rows):
    tm = x_ref.shape[0]
    blocks = [slice(r0, r0 + sub_rows) for r0 in range(0, tm, sub_rows)]

    def branch_outputs(rows):
        y_r = jnp.dot(ret_ref[rows, :], wro_ref[...], preferred_element_type=f32)
        y_s = jnp.dot(swa_ref[rows, :], wso_ref[...], preferred_element_type=f32)
        return y_r, y_s

    ys_next = branch_outputs(blocks[0])
    for n, rows in enumerate(blocks):
        y_r, y_s = ys_next
        if n + 1 < len(blocks):
            ys_next = branch_outputs(blocks[n + 1])
        merged = gr_ref[rows, :].astype(f32) * y_r + gs_ref[rows, :].astype(f32) * y_s
        z = jnp.dot(merged.astype(bf16), wo_ref[...], preferred_element_type=f32)
        o_ref[rows, :] = x_ref[rows, :] + z * _rms_scale(z) * g_ref[...]


def _post_mix(x2d, ret, swa, gates, w_ro, w_so, w_o, gain, *, tm=2 * TOK_TILE, sub_rows=256):
    n = x2d.shape[0]
    const = lambda i: (0, 0)
    return pl.pallas_call(
        functools.partial(_post_mix_kernel, sub_rows=sub_rows),
        out_shape=jax.ShapeDtypeStruct((n, D_MODEL), f32),
        grid=(n // tm,),
        in_specs=[
            pl.BlockSpec((tm, D_MODEL), lambda i: (i, 0)),
            pl.BlockSpec((tm, RET_V), lambda i: (i, 0)),
            pl.BlockSpec((tm, SWA_Q), lambda i: (i, 0)),
            pl.BlockSpec((tm, D_MODEL), lambda i: (i, 0)),
            pl.BlockSpec((tm, D_MODEL), lambda i: (i, 1)),
            pl.BlockSpec((RET_V, D_MODEL), const, pipeline_mode=pl.Buffered(1)),
            pl.BlockSpec((SWA_Q, D_MODEL), const, pipeline_mode=pl.Buffered(1)),
            pl.BlockSpec((D_MODEL, D_MODEL), const, pipeline_mode=pl.Buffered(1)),
            pl.BlockSpec((1, D_MODEL), const),
        ],
        out_specs=pl.BlockSpec((tm, D_MODEL), lambda i: (i, 0)),
        compiler_params=pltpu.CompilerParams(
            dimension_semantics=("arbitrary",), vmem_limit_bytes=VMEM_LIMIT_BYTES),
        name="post_mix",
    )(x2d, ret, swa, gates, gates, w_ro, w_so, w_o, gain)


def _mlp_kernel(x_ref, gpre_ref, wup_ref, wdn_ref, gpost_ref, o_ref, *, ff_chunk, sub_rows):
    tm = x_ref.shape[0]
    starts = list(range(0, D_FF, ff_chunk))
    finish_previous = None
    for r0 in range(0, tm, sub_rows):
        rows = slice(r0, r0 + sub_rows)
        x = x_ref[rows, :]
        row_scale = _rms_scale(x)
        xg = (x * gpre_ref[...]).astype(bf16)

        def up(c0, xg=xg):
            return jnp.dot(xg, wup_ref[:, c0:c0 + ff_chunk], preferred_element_type=f32)

        acc = None
        u_next = up(starts[0])
        if finish_previous is not None:
            finish_previous()
        for n, c0 in enumerate(starts):
            u = u_next
            if n + 1 < len(starts):
                u_next = up(starts[n + 1])
            u = jnp.maximum(u * row_scale, 0.0)
            part = jnp.dot((u * u).astype(bf16), wdn_ref[c0:c0 + ff_chunk, :], preferred_element_type=f32)
            acc = part if acc is None else acc + part

        def finish_previous(rows=rows, x=x, acc=acc):
            o_ref[rows, :] = x + acc * _rms_scale(acc) * gpost_ref[...]
    finish_previous()


def _mlp(x2d, gpre, w_up, w_dn, gpost, *, tm=2 * TOK_TILE, ff_chunk=1024, sub_rows=512):
    n = x2d.shape[0]
    const = lambda i: (0, 0)
    return pl.pallas_call(
        functools.partial(_mlp_kernel, ff_chunk=ff_chunk, sub_rows=sub_rows),
        out_shape=jax.ShapeDtypeStruct((n, D_MODEL), f32),
        grid=(n // tm,),
        in_specs=[
            pl.BlockSpec((tm, D_MODEL), lambda i: (i, 0)),
            pl.BlockSpec((1, D_MODEL), const),
            pl.BlockSpec((D_MODEL, D_FF), const, pipeline_mode=pl.Buffered(1)),
            pl.BlockSpec((D_FF, D_MODEL), const, pipeline_mode=pl.Buffered(1)),
            pl.BlockSpec((1, D_MODEL), const),
        ],
        out_specs=pl.BlockSpec((tm, D_MODEL), lambda i: (i, 0)),
        compiler_params=pltpu.CompilerParams(
            dimension_semantics=("arbitrary",), vmem_limit_bytes=VMEM_LIMIT_BYTES),
        name="mlp",
    )(x2d, gpre, w_up, w_dn, gpost)


def kernel(x, pre_mix_norm, w_in, w_ret_out, w_swa_out, w_out, sinks, post_mix_norm, pre_mlp_norm, w_up,
           w_down, post_mlp_norm):
    batch, seq, d = x.shape
    depth = w_in.shape[0]
    assert d == D_MODEL and seq % TOK_TILE == 0
    col_order = _in_proj_column_order()
    row_order = _swa_out_row_order()
    x2d = x.reshape(batch * seq, d)
    for l in range(depth):
        mix, kdt, gates = _in_proj(x2d, pre_mix_norm[l][None, :], _take_runs(w_in[l], col_order, 1).astype(bf16))
        ret, swa = _mixer(mix, kdt, sinks[l].astype(f32), batch, seq)
        x2d = _post_mix(x2d, ret, swa, gates, w_ret_out[l].astype(bf16),
                        _take_runs(w_swa_out[l], row_order, 0).astype(bf16), w_out[l].astype(bf16),
                        post_mix_norm[l][None, :])
        x2d = _mlp(x2d, pre_mlp_norm[l][None, :], w_up[l].astype(bf16), w_down[l].astype(bf16),
                   post_mlp_norm[l][None, :])
    return x2d.reshape(batch, seq, d)
```

```python
import functools

import numpy as np
import jax
import jax.numpy as jnp
from jax import lax
from jax.experimental import pallas as pl
from jax.experimental.pallas import tpu as pltpu

D_MODEL = 1024
RET_HEADS = 4
RET_DK = 128
RET_DV = 256
CHUNK = 128
SWA_HEADS = 8
SWA_GROUPS = 2
SWA_REP = SWA_HEADS // SWA_GROUPS
SWA_DH = 64
D_FF = 4 * D_MODEL
EPS = 1e-6

RET_QK = RET_HEADS * RET_DK
RET_V = RET_HEADS * RET_DV
SWA_Q = SWA_HEADS * SWA_DH
SWA_KV = SWA_GROUPS * SWA_DH

W_QR = 0
W_KR = W_QR + RET_QK
W_VR = W_KR + RET_QK
W_GR = W_VR + RET_V
W_QS = W_GR + RET_V
W_KS = W_QS + SWA_Q
W_VS = W_KS + SWA_KV
D_MIXW = W_VS + SWA_KV
D_GATES = 2 * D_MODEL

M_QR = 0
M_KR = M_QR + RET_QK
M_VR = M_KR + RET_QK
M_SG = M_VR + RET_V
M_QD = M_SG + RET_V
M_QS = M_QD + RET_QK
M_KG = M_QS + SWA_Q
M_VS = M_KG + SWA_GROUPS * SWA_KV
D_MIXIN = M_VS + SWA_KV

TOK_TILE = 512
CHUNKS_PER_TILE = TOK_TILE // CHUNK

VMEM_LIMIT_BYTES = 56 * 1024 * 1024

LOG_GAMMA = [float(np.log(1.0 - 2.0 ** (-5.0 - h))) for h in range(RET_HEADS)]
ALIBI_SLOPES = [float(2.0 ** (-8.0 / SWA_HEADS * (h + 1))) for h in range(SWA_HEADS)]
K_SCALE = float(RET_DK ** -0.5)
S_SCALE = float(SWA_DH ** -0.5)
LOG2E = float(np.log2(np.e))

f32 = jnp.float32
bf16 = jnp.bfloat16


def _in_proj_column_order():
    ref_splits = (RET_QK, RET_QK, RET_V, RET_V, SWA_Q, SWA_KV, SWA_KV)
    starts = np.concatenate([[0], np.cumsum(ref_splits)[:-1]])
    q_r, k_r, v_r, g_r, q_s, k_s, v_s = [np.arange(s, s + n) for s, n in zip(starts, ref_splits)]
    q_s = q_s.reshape(SWA_GROUPS, SWA_REP, SWA_DH).transpose(1, 0, 2).reshape(-1)
    return np.concatenate([q_r, k_r, v_r, g_r, q_s, k_s, v_s])


def _swa_out_row_order():
    return np.arange(SWA_Q).reshape(SWA_GROUPS, SWA_REP, SWA_DH).transpose(1, 0, 2).reshape(-1)


def _take_runs(w, order, axis):
    breaks = np.flatnonzero(np.diff(order) != 1) + 1
    pieces = [lax.slice_in_dim(w, int(run[0]), int(run[-1]) + 1, axis=axis) for run in np.split(order, breaks)]
    return jnp.concatenate(pieces, axis=axis)


def _rms_scale(v):
    return lax.rsqrt(jnp.mean(v * v, axis=-1, keepdims=True) + EPS)


def _in_proj_kernel(x_ref, g_ref, w_ref, mix_ref, kdt_ref, qdec_ref, kdec_ref):
    C = CHUNK

    @pl.when(pl.program_id(0) == 0)
    def _init_decay_tables():
        pos = (lax.broadcasted_iota(jnp.int32, (TOK_TILE, RET_DK), 0) % C).astype(f32)
        for h in range(RET_HEADS):
            qdec_ref[:, h * RET_DK:(h + 1) * RET_DK] = jnp.exp(LOG_GAMMA[h] * (pos + 1.0))
            kdec_ref[:, h * RET_DK:(h + 1) * RET_DK] = jnp.exp(LOG_GAMMA[h] * (C - 1.0 - pos)) * K_SCALE

    x = x_ref[...]
    row_scale = _rms_scale(x)
    xg = (x * g_ref[...]).astype(bf16)

    def proj(c0, width):
        return jnp.dot(xg, w_ref[:, c0:c0 + width], preferred_element_type=f32) * row_scale

    g = proj(W_GR, RET_V)
    mix_ref[:, M_SG:M_SG + RET_V] = (g * jax.nn.sigmoid(g)).astype(bf16)
    q = proj(W_QR, RET_QK)
    mix_ref[:, M_QR:M_QR + RET_QK] = q.astype(bf16)
    mix_ref[:, M_QD:M_QD + RET_QK] = (q * qdec_ref[...]).astype(bf16)
    k = proj(W_KR, RET_QK)
    mix_ref[:, M_KR:M_KR + RET_QK] = k.astype(bf16)
    kd = k * kdec_ref[...]
    for ci in range(CHUNKS_PER_TILE):
        for hd in range(RET_HEADS):
            kdt_ref[ci, hd * RET_DK:(hd + 1) * RET_DK, :] = (
                kd[ci * C:(ci + 1) * C, hd * RET_DK:(hd + 1) * RET_DK].T.astype(bf16))
    mix_ref[:, M_QS:M_QS + SWA_Q] = proj(W_QS, SWA_Q).astype(bf16)
    kv_s = proj(W_KS, 2 * SWA_KV)
    ks = kv_s[:, 0:SWA_KV] * (S_SCALE * LOG2E)
    lane = lax.broadcasted_iota(jnp.int32, (TOK_TILE, SWA_KV), 1)
    for grp in range(SWA_GROUPS):
        in_group = (lane >= grp * SWA_DH) & (lane < (grp + 1) * SWA_DH)
        mix_ref[:, M_KG + grp * SWA_KV:M_KG + (grp + 1) * SWA_KV] = jnp.where(in_group, ks, 0.0).astype(bf16)
    mix_ref[:, M_VS:M_VS + SWA_KV] = kv_s[:, SWA_KV:2 * SWA_KV].astype(bf16)
    mix_ref[:, M_VR:M_VR + RET_V] = proj(W_VR, RET_V).astype(bf16)


def _in_proj(x2d, gain, w_in):
    n = x2d.shape[0]
    nt = n // TOK_TILE
    const = lambda i: (0, 0)
    return pl.pallas_call(
        _in_proj_kernel,
        out_shape=(jax.ShapeDtypeStruct((n, D_MIXIN), bf16),
                   jax.ShapeDtypeStruct((n // CHUNK, RET_QK, CHUNK), bf16)),
        grid=(nt,),
        in_specs=[
            pl.BlockSpec((TOK_TILE, D_MODEL), lambda i: (i, 0)),
            pl.BlockSpec((1, D_MODEL), const),
            pl.BlockSpec((D_MODEL, D_MIXW), const, pipeline_mode=pl.Buffered(1)),
        ],
        out_specs=(
            pl.BlockSpec((TOK_TILE, D_MIXIN), lambda i: (i, 0)),
            pl.BlockSpec((CHUNKS_PER_TILE, RET_QK, CHUNK), lambda i: (i, 0, 0)),
        ),
        scratch_shapes=[
            pltpu.VMEM((TOK_TILE, RET_QK), f32),
            pltpu.VMEM((TOK_TILE, RET_QK), f32),
        ],
        compiler_params=pltpu.CompilerParams(
            dimension_semantics=("arbitrary",), vmem_limit_bytes=VMEM_LIMIT_BYTES),
        name="in_proj",
    )(x2d, gain, w_in)


def _mixer_kernel(qr_ref, kr_ref, vr_ref, sg_ref, qd_ref, qs_ref, kg0_ref, kg1_ref, vs_ref, kdt_ref, sink_ref,
                  ret_ref, swa_ref,
                  state_ref, kprev_ref, vprev_ref, dmat_ref, bias_ref):
    b = pl.program_id(0)
    c = pl.program_id(1)
    C = CHUNK

    @pl.when((b == 0) & (c == 0))
    def _init_tables():
        i = lax.broadcasted_iota(jnp.int32, (C, C), 0).astype(f32)
        j = lax.broadcasted_iota(jnp.int32, (C, C), 1).astype(f32)
        diff = i - j
        for h in range(RET_HEADS):
            dmat_ref[h] = jnp.where(diff >= 0, jnp.exp(LOG_GAMMA[h] * jnp.maximum(diff, 0.0)), 0.0) * K_SCALE
        row = lax.broadcasted_iota(jnp.int32, (SWA_REP * C, 2 * C), 0)
        kpos = lax.broadcasted_iota(jnp.int32, (SWA_REP * C, 2 * C), 1)
        rep = row // C
        dist = (row - rep * C + C) - kpos
        valid = (dist >= 0) & (dist < C)
        distf = dist.astype(f32)
        for g in range(SWA_GROUPS):
            slope = jnp.zeros((SWA_REP * C, 2 * C), f32)
            sink = jnp.zeros((SWA_REP * C, 2 * C), f32)
            for r in range(SWA_REP):
                slope = jnp.where(rep == r, ALIBI_SLOPES[g * SWA_REP + r], slope)
                sink = jnp.where(rep == r, sink_ref[g * SWA_REP + r], sink)
            alibi = -slope * distf
            bias_ref[g] = jnp.where(kpos == 0, sink, jnp.where(valid, alibi, -jnp.inf)) * LOG2E
            bias_ref[SWA_GROUPS + g] = jnp.where(
                kpos == 0, sink, jnp.where(valid & (kpos >= C), alibi, -jnp.inf)) * LOG2E

    @pl.when(c == 0)
    def _reset_sequence_state():
        state_ref[...] = jnp.zeros_like(state_ref)
        kprev_ref[...] = jnp.zeros_like(kprev_ref)
        vprev_ref[...] = jnp.zeros_like(vprev_ref)

    def zero_first_row(block):
        rows_per_vreg = 16
        top = block[0:rows_per_vreg, :]
        is_first = lax.broadcasted_iota(jnp.int32, top.shape, 0) == 0
        return jnp.concatenate([jnp.where(is_first, jnp.zeros_like(top), top), block[rows_per_vreg:, :]], axis=0)

    kg_refs = (kg0_ref, kg1_ref)
    first_group = lax.broadcasted_iota(jnp.int32, (C, SWA_KV), 1) < SWA_DH
    ones_cols = jnp.ones((2 * C, SWA_KV), bf16)
    units = [(j, g) for j in range(SWA_REP) for g in range(SWA_GROUPS)]
    live = [dict() for _ in range(CHUNKS_PER_TILE)]

    def rows(t):
        return slice(t * C, (t + 1) * C)

    def head(h, width):
        return slice(h * width, (h + 1) * width)

    def first_matmuls(t):
        st = live[t]
        st["qk"] = [lax.dot_general(qr_ref[rows(t), head(h, RET_DK)], kr_ref[rows(t), head(h, RET_DK)],
                                    (((1,), (1,)), ((), ())), preferred_element_type=f32)
                    for h in range(RET_HEADS)]
        st["kv"] = [jnp.dot(kdt_ref[t, head(h, RET_DK), :], vr_ref[rows(t), head(h, RET_DV)],
                            preferred_element_type=f32) for h in range(RET_HEADS)]
        if t == 0:
            keys = [jnp.concatenate([kprev_ref[g], kg_refs[g][0:C, :]], axis=0) for g in range(SWA_GROUPS)]
            vals = jnp.concatenate([vprev_ref[...], vs_ref[0:C, :]], axis=0)
            table0 = jnp.where(c == 0, SWA_GROUPS, 0)
        else:
            keys = [kg_refs[g][(t - 1) * C:(t + 1) * C, :] for g in range(SWA_GROUPS)]
            vals = vs_ref[(t - 1) * C:(t + 1) * C, :]
            table0 = 0
        keys = [zero_first_row(k) for k in keys]
        st["vv_ones"] = jnp.concatenate([zero_first_row(vals), ones_cols], axis=1)
        st["s"] = {}
        for j, g in units:
            s = lax.dot_general(qs_ref[rows(t), head(j, C)], keys[g], (((1,), (1,)), ((), ())),
                                preferred_element_type=f32)
            st["s"][j, g] = s + bias_ref[table0 + g, head(j, C), :]

    def first_vector_work(t):
        st = live[t]
        st["a"], st["state_bf16"] = [], []
        for h in range(RET_HEADS):
            s_prev = state_ref[h]
            st["a"].append((st["qk"][h] * dmat_ref[h]).astype(bf16))
            st["state_bf16"].append(s_prev.astype(bf16))
            state_ref[h] = float(np.exp(LOG_GAMMA[h] * C)) * s_prev + st["kv"][h]
        st["p"] = {}
        for u in units:
            s = st["s"][u]
            st["p"][u] = jnp.exp2(s - jnp.max(s, axis=-1, keepdims=True)).astype(bf16)
        del st["qk"], st["kv"], st["s"]

    def second_matmuls(t):
        st = live[t]
        st["o"] = [jnp.dot(jnp.concatenate([st["a"][h], qd_ref[rows(t), head(h, RET_DK)]], axis=1),
                           jnp.concatenate([vr_ref[rows(t), head(h, RET_DV)], st["state_bf16"][h]], axis=0),
                           preferred_element_type=f32) for h in range(RET_HEADS)]
        st["r"] = {u: jnp.dot(st["p"][u], st["vv_ones"], preferred_element_type=f32) for u in units}
        del st["a"], st["state_bf16"], st["p"], st["vv_ones"]

    def second_vector_work(t):
        st = live[t]
        for h in range(RET_HEADS):
            o = st["o"][h]
            mu = jnp.mean(o, axis=-1, keepdims=True)
            d = o - mu
            var = jnp.mean(d * d, axis=-1, keepdims=True)
            ret_ref[rows(t), head(h, RET_DV)] = (
                sg_ref[rows(t), head(h, RET_DV)].astype(f32) * (d * lax.rsqrt(var + EPS))).astype(bf16)
        for j in range(SWA_REP):
            num = [st["r"][j, g][:, 0:SWA_KV] for g in range(SWA_GROUPS)]
            den = [st["r"][j, g][:, SWA_KV:2 * SWA_KV] for g in range(SWA_GROUPS)]
            o_pair = jnp.where(first_group, num[0], num[1]) / jnp.where(first_group, den[0], den[1])
            swa_ref[rows(t), head(j, C)] = o_pair.astype(bf16)
        st.clear()

    for t in range(CHUNKS_PER_TILE + 1):
        if t < CHUNKS_PER_TILE:
            first_matmuls(t)
        if t >= 1:
            first_vector_work(t - 1)
            second_matmuls(t - 1)
            second_vector_work(t - 1)
    last = slice(TOK_TILE - C, TOK_TILE)
    for g in range(SWA_GROUPS):
        kprev_ref[g] = kg_refs[g][last, :]
    vprev_ref[...] = vs_ref[last, :]


def _mixer(mix, kdt, sinks, batch, seq):
    n = mix.shape[0]
    nt = seq // TOK_TILE
    C = CHUNK

    def rows(col_block):
        return lambda b, c: (b * nt + c, col_block)

    return pl.pallas_call(
        _mixer_kernel,
        out_shape=(jax.ShapeDtypeStruct((n, RET_V), bf16), jax.ShapeDtypeStruct((n, SWA_Q), bf16)),
        grid=(batch, nt),
        in_specs=[
            pl.BlockSpec((TOK_TILE, RET_QK), rows(M_QR // RET_QK)),
            pl.BlockSpec((TOK_TILE, RET_QK), rows(M_KR // RET_QK)),
            pl.BlockSpec((TOK_TILE, RET_V), rows(M_VR // RET_V)),
            pl.BlockSpec((TOK_TILE, RET_V), rows(M_SG // RET_V)),
            pl.BlockSpec((TOK_TILE, RET_QK), rows(M_QD // RET_QK)),
            pl.BlockSpec((TOK_TILE, SWA_Q), rows(M_QS // SWA_Q)),
            pl.BlockSpec((TOK_TILE, SWA_KV), rows(M_KG // SWA_KV)),
            pl.BlockSpec((TOK_TILE, SWA_KV), rows(M_KG // SWA_KV + 1)),
            pl.BlockSpec((TOK_TILE, SWA_KV), rows(M_VS // SWA_KV)),
            pl.BlockSpec((CHUNKS_PER_TILE, RET_QK, C), lambda b, c: (b * nt + c, 0, 0)),
            pl.BlockSpec(memory_space=pltpu.SMEM),
        ],
        out_specs=(
            pl.BlockSpec((TOK_TILE, RET_V), rows(0)),
            pl.BlockSpec((TOK_TILE, SWA_Q), rows(0)),
        ),
        scratch_shapes=[
            pltpu.VMEM((RET_HEADS, RET_DK, RET_DV), f32),
            pltpu.VMEM((SWA_GROUPS, C, SWA_KV), bf16),
            pltpu.VMEM((C, SWA_KV), bf16),
            pltpu.VMEM((RET_HEADS, C, C), f32),
            pltpu.VMEM((2 * SWA_GROUPS, SWA_REP * C, 2 * C), f32),
        ],
        compiler_params=pltpu.CompilerParams(
            dimension_semantics=("arbitrary", "arbitrary"), vmem_limit_bytes=VMEM_LIMIT_BYTES),
        name="mixer",
    )(mix, mix, mix, mix, mix, mix, mix, mix, mix, kdt, sinks)


def _post_kernel(x_ref, ret_ref, swa_ref, gpre_mix_ref, wg_ref, wro_ref, wso_ref, wo_ref, gpost_mix_ref,
                 gpre_mlp_ref, wup_ref, wdn_ref, gpost_mlp_ref, o_ref, *, sub_rows, ff_chunk):
    tm = x_ref.shape[0]
    ff_starts = list(range(0, D_FF, ff_chunk))

    def row_block(rows):
        x = x_ref[rows, :]
        pre_scale = _rms_scale(x)
        xg = (x * gpre_mix_ref[...]).astype(bf16)
        gate_r = jnp.dot(xg, wg_ref[:, 0:D_MODEL], preferred_element_type=f32)
        gate_s = jnp.dot(xg, wg_ref[:, D_MODEL:D_GATES], preferred_element_type=f32)
        y_r = jnp.dot(ret_ref[rows, :], wro_ref[...], preferred_element_type=f32)
        y_s = jnp.dot(swa_ref[rows, :], wso_ref[...], preferred_element_type=f32)
        yield
        merged = jax.nn.sigmoid(gate_r * pre_scale) * y_r + jax.nn.sigmoid(gate_s * pre_scale) * y_s
        z = jnp.dot(merged.astype(bf16), wo_ref[...], preferred_element_type=f32)
        yield
        x1 = x + z * _rms_scale(z) * gpost_mix_ref[...]
        mlp_scale = _rms_scale(x1)
        x1g = (x1 * gpre_mlp_ref[...]).astype(bf16)

        def up(c0):
            return jnp.dot(x1g, wup_ref[:, c0:c0 + ff_chunk], preferred_element_type=f32)

        acc = None
        u_next = up(ff_starts[0])
        yield
        for n, c0 in enumerate(ff_starts):
            u = u_next
            if n + 1 < len(ff_starts):
                u_next = up(ff_starts[n + 1])
            u = jnp.maximum(u * mlp_scale, 0.0)
            part = jnp.dot((u * u).astype(bf16), wdn_ref[c0:c0 + ff_chunk, :], preferred_element_type=f32)
            acc = part if acc is None else acc + part
            yield
        o_ref[rows, :] = x1 + acc * _rms_scale(acc) * gpost_mlp_ref[...]

    active = [row_block(slice(r0, r0 + sub_rows)) for r0 in range(0, tm, sub_rows)]
    while active:
        for gen in list(active):
            if next(gen, "done") == "done":
                active.remove(gen)


def _post(x2d, ret, swa, gpre_mix, w_gates, w_ro, w_so, w_o, gpost_mix, gpre_mlp, w_up, w_dn, gpost_mlp, *,
          tm=TOK_TILE, sub_rows=256, ff_chunk=1024):
    n = x2d.shape[0]
    const = lambda i: (0, 0)
    tile = lambda width: pl.BlockSpec((tm, width), lambda i: (i, 0))
    weight = lambda shape: pl.BlockSpec(shape, const, pipeline_mode=pl.Buffered(1))
    gain = pl.BlockSpec((1, D_MODEL), const)
    return pl.pallas_call(
        functools.partial(_post_kernel, sub_rows=sub_rows, ff_chunk=ff_chunk),
        out_shape=jax.ShapeDtypeStruct((n, D_MODEL), f32),
        grid=(n // tm,),
        in_specs=[
            tile(D_MODEL), tile(RET_V), tile(SWA_Q),
            gain, weight((D_MODEL, D_GATES)),
            weight((RET_V, D_MODEL)), weight((SWA_Q, D_MODEL)), weight((D_MODEL, D_MODEL)), gain,
            gain, weight((D_MODEL, D_FF)), weight((D_FF, D_MODEL)), gain,
        ],
        out_specs=tile(D_MODEL),
        compiler_params=pltpu.CompilerParams(
            dimension_semantics=("arbitrary",), vmem_limit_bytes=VMEM_LIMIT_BYTES),
        name="post",
    )(x2d, ret, swa, gpre_mix, w_gates, w_ro, w_so, w_o, gpost_mix, gpre_mlp, w_up, w_dn, gpost_mlp)


def kernel(x, pre_mix_norm, w_in, w_ret_out, w_swa_out, w_out, sinks, post_mix_norm, pre_mlp_norm, w_up,
           w_down, post_mlp_norm):
    batch, seq, d = x.shape
    depth = w_in.shape[0]
    assert d == D_MODEL and seq % TOK_TILE == 0
    col_order = _in_proj_column_order()
    row_order = _swa_out_row_order()
    x2d = x.reshape(batch * seq, d)
    for l in range(depth):
        pre_mix_gain = pre_mix_norm[l][None, :]
        mix, kdt = _in_proj(x2d, pre_mix_gain, _take_runs(w_in[l], col_order, 1).astype(bf16))
        ret, swa = _mixer(mix, kdt, sinks[l].astype(f32), batch, seq)
        x2d = _post(x2d, ret, swa, pre_mix_gain, w_in[l][:, D_MIXW:D_MIXW + D_GATES].astype(bf16),
                    w_ret_out[l].astype(bf16), _take_runs(w_swa_out[l], row_order, 0).astype(bf16),
                    w_out[l].astype(bf16), post_mix_norm[l][None, :],
                    pre_mlp_norm[l][None, :], w_up[l].astype(bf16), w_down[l].astype(bf16),
                    post_mlp_norm[l][None, :])
    return x2d.reshape(batch, seq, d)
```

```python
import functools

import numpy as np
import jax
import jax.numpy as jnp
from jax import lax
from jax.experimental import pallas as pl
from jax.experimental.pallas import tpu as pltpu

D_MODEL = 1024
RET_HEADS = 4
RET_DK = 128
RET_DV = 256
CHUNK = 128
SWA_HEADS = 8
SWA_GROUPS = 2
SWA_REP = SWA_HEADS // SWA_GROUPS
SWA_DH = 64
D_FF = 4 * D_MODEL
EPS = 1e-6

RET_QK = RET_HEADS * RET_DK
RET_V = RET_HEADS * RET_DV
SWA_Q = SWA_HEADS * SWA_DH
SWA_KV = SWA_GROUPS * SWA_DH

W_QR = 0
W_KR = W_QR + RET_QK
W_VR = W_KR + RET_QK
W_GR = W_VR + RET_V
W_QS = W_GR + RET_V
W_KS = W_QS + SWA_Q
W_VS = W_KS + SWA_KV
D_MIXW = W_VS + SWA_KV
D_GATES = 2 * D_MODEL

M_QR = 0
M_KR = M_QR + RET_QK
M_VR = M_KR + RET_QK
M_SG = M_VR + RET_V
M_QD = M_SG + RET_V
M_QS = M_QD + RET_QK
M_KG = M_QS + SWA_Q
M_VS = M_KG + SWA_GROUPS * SWA_KV
D_MIXIN = M_VS + SWA_KV

TOK_TILE = 512
CHUNKS_PER_TILE = TOK_TILE // CHUNK
PROJ_PIECE_COLS = 256
POST_TILE = 512

VMEM_LIMIT_BYTES = 56 * 1024 * 1024

LOG_GAMMA = [float(np.log(1.0 - 2.0 ** (-5.0 - h))) for h in range(RET_HEADS)]
ALIBI_SLOPES = [float(2.0 ** (-8.0 / SWA_HEADS * (h + 1))) for h in range(SWA_HEADS)]
K_SCALE = float(RET_DK ** -0.5)
S_SCALE = float(SWA_DH ** -0.5)
LOG2E = float(np.log2(np.e))

f32 = jnp.float32
bf16 = jnp.bfloat16


def _in_proj_column_order():
    ref_splits = (RET_QK, RET_QK, RET_V, RET_V, SWA_Q, SWA_KV, SWA_KV)
    starts = np.concatenate([[0], np.cumsum(ref_splits)[:-1]])
    q_r, k_r, v_r, g_r, q_s, k_s, v_s = [np.arange(s, s + n) for s, n in zip(starts, ref_splits)]
    q_s = q_s.reshape(SWA_GROUPS, SWA_REP, SWA_DH).transpose(1, 0, 2).reshape(-1)
    return np.concatenate([q_r, k_r, v_r, g_r, q_s, k_s, v_s])


def _swa_out_row_order():
    return np.arange(SWA_Q).reshape(SWA_GROUPS, SWA_REP, SWA_DH).transpose(1, 0, 2).reshape(-1)


def _take_runs(w, order, axis):
    breaks = np.flatnonzero(np.diff(order) != 1) + 1
    pieces = [lax.slice_in_dim(w, int(run[0]), int(run[-1]) + 1, axis=axis) for run in np.split(order, breaks)]
    return jnp.concatenate(pieces, axis=axis)


def _rms_scale(v):
    return lax.rsqrt(jnp.mean(v * v, axis=-1, keepdims=True) + EPS)


def _cols(offset, index, width):
    return slice(offset + index * width, offset + (index + 1) * width)


def _init_tables(sink_ref, qdec_ref, kdec_ref, dmat_ref, bias_ref):
    C = CHUNK
    pos = (lax.broadcasted_iota(jnp.int32, (TOK_TILE, RET_DK), 0) % C).astype(f32)
    i = lax.broadcasted_iota(jnp.int32, (C, C), 0).astype(f32)
    j = lax.broadcasted_iota(jnp.int32, (C, C), 1).astype(f32)
    diff = i - j
    for h in range(RET_HEADS):
        qdec_ref[:, _cols(0, h, RET_DK)] = jnp.exp(LOG_GAMMA[h] * (pos + 1.0))
        kdec_ref[:, _cols(0, h, RET_DK)] = jnp.exp(LOG_GAMMA[h] * (C - 1.0 - pos)) * K_SCALE
        dmat_ref[h] = jnp.where(diff >= 0, jnp.exp(LOG_GAMMA[h] * jnp.maximum(diff, 0.0)), 0.0) * K_SCALE
    row = lax.broadcasted_iota(jnp.int32, (SWA_REP * C, 2 * C), 0)
    kpos = lax.broadcasted_iota(jnp.int32, (SWA_REP * C, 2 * C), 1)
    rep = row // C
    dist = (row - rep * C + C) - kpos
    valid = (dist >= 0) & (dist < C)
    distf = dist.astype(f32)
    for g in range(SWA_GROUPS):
        slope = jnp.zeros((SWA_REP * C, 2 * C), f32)
        sink = jnp.zeros((SWA_REP * C, 2 * C), f32)
        for r in range(SWA_REP):
            slope = jnp.where(rep == r, ALIBI_SLOPES[g * SWA_REP + r], slope)
            sink = jnp.where(rep == r, sink_ref[g * SWA_REP + r], sink)
        alibi = -slope * distf
        bias_ref[g] = jnp.where(kpos == 0, sink, jnp.where(valid, alibi, -jnp.inf)) * LOG2E
        bias_ref[SWA_GROUPS + g] = jnp.where(
            kpos == 0, sink, jnp.where(valid & (kpos >= C), alibi, -jnp.inf)) * LOG2E


def _projection_pieces(x_ref, g_ref, w_ref, qdec_ref, kdec_ref, mix_ref, kdt_ref):
    C = CHUNK
    shared = {}

    def prepare():
        x = x_ref[...]
        shared["row_scale"] = _rms_scale(x)
        shared["xg"] = (x * g_ref[...]).astype(bf16)

    def proj(c0, width):
        return (jnp.dot(shared["xg"], w_ref[:, c0:c0 + width], preferred_element_type=f32)
                * shared["row_scale"])

    width = PROJ_PIECE_COLS

    def swish_gate(part):
        def run():
            g = proj(W_GR + part * width, width)
            mix_ref[:, _cols(M_SG, part, width)] = (g * jax.nn.sigmoid(g)).astype(bf16)
        return run

    def queries(part):
        def run():
            q = proj(W_QR + part * width, width)
            mix_ref[:, _cols(M_QR, part, width)] = q.astype(bf16)
            mix_ref[:, _cols(M_QD, part, width)] = (q * qdec_ref[:, _cols(0, part, width)]).astype(bf16)
        return run

    def keys(part):
        def run():
            k = proj(W_KR + part * width, width)
            mix_ref[:, _cols(M_KR, part, width)] = k.astype(bf16)
            kd = k * kdec_ref[:, _cols(0, part, width)]
            heads_per_piece = width // RET_DK
            for ci in range(CHUNKS_PER_TILE):
                for hd in range(heads_per_piece):
                    kdt_ref[ci, _cols(0, part * heads_per_piece + hd, RET_DK), :] = (
                        kd[ci * C:(ci + 1) * C, _cols(0, hd, RET_DK)].T.astype(bf16))
        return run

    def attention_queries(part):
        def run():
            mix_ref[:, _cols(M_QS, part, width)] = proj(W_QS + part * width, width).astype(bf16)
        return run

    def attention_keys_values():
        kv_s = proj(W_KS, 2 * SWA_KV)
        ks = kv_s[:, 0:SWA_KV] * (S_SCALE * LOG2E)
        lane = lax.broadcasted_iota(jnp.int32, (TOK_TILE, SWA_KV), 1)
        for grp in range(SWA_GROUPS):
            in_group = (lane >= grp * SWA_DH) & (lane < (grp + 1) * SWA_DH)
            mix_ref[:, _cols(M_KG, grp, SWA_KV)] = jnp.where(in_group, ks, 0.0).astype(bf16)
        mix_ref[:, M_VS:M_VS + SWA_KV] = kv_s[:, SWA_KV:2 * SWA_KV].astype(bf16)

    def values(part):
        def run():
            mix_ref[:, _cols(M_VR, part, width)] = proj(W_VR + part * width, width).astype(bf16)
        return run

    pieces = [swish_gate(n) for n in range(RET_V // width)]
    pieces += [queries(n) for n in range(RET_QK // width)]
    pieces += [keys(n) for n in range(RET_QK // width)]
    pieces += [attention_queries(n) for n in range(SWA_Q // width)]
    pieces += [attention_keys_values]
    pieces += [values(n) for n in range(RET_V // width)]
    return prepare, pieces


def _mixer_phases(mix_ref, kdt_ref, first_tile_of_seq, ret_ref, swa_ref,
                  state_ref, kprev_ref, vprev_ref, dmat_ref, bias_ref):
    C = CHUNK
    first_group = lax.broadcasted_iota(jnp.int32, (C, SWA_KV), 1) < SWA_DH
    ones_cols = jnp.ones((2 * C, SWA_KV), bf16)
    units = [(j, g) for j in range(SWA_REP) for g in range(SWA_GROUPS)]
    live = [dict() for _ in range(CHUNKS_PER_TILE)]

    def rows(t):
        return slice(t * C, (t + 1) * C)

    def zero_first_row(block):
        rows_per_vreg = 16
        top = block[0:rows_per_vreg, :]
        is_first = lax.broadcasted_iota(jnp.int32, top.shape, 0) == 0
        return jnp.concatenate([jnp.where(is_first, jnp.zeros_like(top), top), block[rows_per_vreg:, :]], axis=0)

    def first_matmuls(t):
        st = live[t]
        st["qk"] = [lax.dot_general(mix_ref[rows(t), _cols(M_QR, h, RET_DK)], mix_ref[rows(t), _cols(M_KR, h, RET_DK)],
                                    (((1,), (1,)), ((), ())), preferred_element_type=f32)
                    for h in range(RET_HEADS)]
        st["kv"] = [jnp.dot(kdt_ref[t, _cols(0, h, RET_DK), :], mix_ref[rows(t), _cols(M_VR, h, RET_DV)],
                            preferred_element_type=f32) for h in range(RET_HEADS)]
        if t == 0:
            keys = [jnp.concatenate([kprev_ref[g], mix_ref[0:C, _cols(M_KG, g, SWA_KV)]], axis=0)
                    for g in range(SWA_GROUPS)]
            vals = jnp.concatenate([vprev_ref[...], mix_ref[0:C, M_VS:M_VS + SWA_KV]], axis=0)
            table0 = jnp.where(first_tile_of_seq, SWA_GROUPS, 0)
        else:
            prev_and_current = slice((t - 1) * C, (t + 1) * C)
            keys = [mix_ref[prev_and_current, _cols(M_KG, g, SWA_KV)] for g in range(SWA_GROUPS)]
            vals = mix_ref[prev_and_current, M_VS:M_VS + SWA_KV]
            table0 = 0
        keys = [zero_first_row(k) for k in keys]
        st["vv_ones"] = jnp.concatenate([zero_first_row(vals), ones_cols], axis=1)
        st["s"] = {}
        for j, g in units:
            s = lax.dot_general(mix_ref[rows(t), _cols(M_QS, j, C)], keys[g], (((1,), (1,)), ((), ())),
                                preferred_element_type=f32)
            st["s"][j, g] = s + bias_ref[table0 + g, _cols(0, j, C), :]

    def retention_vector_work(t):
        st = live[t]
        st["a"], st["state_bf16"] = [], []
        for h in range(RET_HEADS):
            s_prev = state_ref[h]
            st["a"].append((st["qk"][h] * dmat_ref[h]).astype(bf16))
            st["state_bf16"].append(s_prev.astype(bf16))
            state_ref[h] = float(np.exp(LOG_GAMMA[h] * C)) * s_prev + st["kv"][h]
        del st["qk"], st["kv"]

    def softmax_numerators(t):
        st = live[t]
        st["p"] = {}
        for u in units:
            s = st["s"][u]
            st["p"][u] = jnp.exp2(s - jnp.max(s, axis=-1, keepdims=True)).astype(bf16)
        del st["s"]

    def second_matmuls(t):
        st = live[t]
        st["o"] = [jnp.dot(jnp.concatenate([st["a"][h], mix_ref[rows(t), _cols(M_QD, h, RET_DK)]], axis=1),
                           jnp.concatenate([mix_ref[rows(t), _cols(M_VR, h, RET_DV)], st["state_bf16"][h]], axis=0),
                           preferred_element_type=f32) for h in range(RET_HEADS)]
        st["r"] = {u: jnp.dot(st["p"][u], st["vv_ones"], preferred_element_type=f32) for u in units}
        del st["a"], st["state_bf16"], st["p"], st["vv_ones"]

    def retention_output(t):
        st = live[t]
        for h in range(RET_HEADS):
            o = st["o"][h]
            mu = jnp.mean(o, axis=-1, keepdims=True)
            d = o - mu
            var = jnp.mean(d * d, axis=-1, keepdims=True)
            ret_ref[rows(t), _cols(0, h, RET_DV)] = (
                mix_ref[rows(t), _cols(M_SG, h, RET_DV)].astype(f32) * (d * lax.rsqrt(var + EPS))).astype(bf16)
        del st["o"]

    def attention_output(t):
        st = live[t]
        for j in range(SWA_REP):
            num = [st["r"][j, g][:, 0:SWA_KV] for g in range(SWA_GROUPS)]
            den = [st["r"][j, g][:, SWA_KV:2 * SWA_KV] for g in range(SWA_GROUPS)]
            o_pair = jnp.where(first_group, num[0], num[1]) / jnp.where(first_group, den[0], den[1])
            swa_ref[rows(t), _cols(0, j, C)] = o_pair.astype(bf16)
        st.clear()

    def carry_last_block():
        last = slice(TOK_TILE - C, TOK_TILE)
        for g in range(SWA_GROUPS):
            kprev_ref[g] = mix_ref[last, _cols(M_KG, g, SWA_KV)]
        vprev_ref[...] = mix_ref[last, M_VS:M_VS + SWA_KV]

    phases = []
    for t in range(CHUNKS_PER_TILE + 1):
        if t < CHUNKS_PER_TILE:
            phases.append(functools.partial(first_matmuls, t))
        if t >= 1:
            phases += [functools.partial(fn, t - 1) for fn in (
                retention_vector_work, softmax_numerators, second_matmuls, retention_output, attention_output)]
    phases.append(carry_last_block)
    return phases


def _proj_mixer_kernel(x_ref, g_ref, w_ref, sink_ref, ret_ref, swa_ref,
                       mix_even_ref, mix_odd_ref, kdt_even_ref, kdt_odd_ref, qdec_ref, kdec_ref,
                       state_ref, kprev_ref, vprev_ref, dmat_ref, bias_ref, *, tiles_per_seq):
    s = pl.program_id(0)
    first_tile_of_seq = lax.rem(s + tiles_per_seq - 1, tiles_per_seq) == 0

    @pl.when(s == 0)
    def _first_step():
        _init_tables(sink_ref, qdec_ref, kdec_ref, dmat_ref, bias_ref)
        mix_odd_ref[...] = jnp.zeros_like(mix_odd_ref)
        kdt_odd_ref[...] = jnp.zeros_like(kdt_odd_ref)

    @pl.when((s == 0) | first_tile_of_seq)
    def _reset_sequence_state():
        state_ref[...] = jnp.zeros_like(state_ref)
        kprev_ref[...] = jnp.zeros_like(kprev_ref)
        vprev_ref[...] = jnp.zeros_like(vprev_ref)

    def step(write_mix_ref, write_kdt_ref, read_mix_ref, read_kdt_ref):
        prepare, pieces = _projection_pieces(x_ref, g_ref, w_ref, qdec_ref, kdec_ref, write_mix_ref, write_kdt_ref)
        phases = _mixer_phases(read_mix_ref, read_kdt_ref, first_tile_of_seq, ret_ref, swa_ref,
                               state_ref, kprev_ref, vprev_ref, dmat_ref, bias_ref)
        phases[0]()
        prepare()
        emitted = 0
        for n, phase in enumerate(phases[1:], start=1):
            while emitted < len(pieces) and emitted * (len(phases) - 1) < n * len(pieces):
                pieces[emitted]()
                emitted += 1
            phase()

    @pl.when(lax.rem(s, 2) == 0)
    def _even_step():
        step(mix_even_ref, kdt_even_ref, mix_odd_ref, kdt_odd_ref)

    @pl.when(lax.rem(s, 2) == 1)
    def _odd_step():
        step(mix_odd_ref, kdt_odd_ref, mix_even_ref, kdt_even_ref)


def _proj_mixer(x2d, gain, w_mix, sinks, seq):
    n = x2d.shape[0]
    nt = n // TOK_TILE
    C = CHUNK
    const = lambda s: (0, 0)
    projected_tile = lambda s: (jnp.minimum(s, nt - 1), 0)
    mixed_tile = lambda s: (jnp.maximum(s - 1, 0), 0)
    return pl.pallas_call(
        functools.partial(_proj_mixer_kernel, tiles_per_seq=seq // TOK_TILE),
        out_shape=(jax.ShapeDtypeStruct((n, RET_V), bf16), jax.ShapeDtypeStruct((n, SWA_Q), bf16)),
        grid=(nt + 1,),
        in_specs=[
            pl.BlockSpec((TOK_TILE, D_MODEL), projected_tile),
            pl.BlockSpec((1, D_MODEL), const),
            pl.BlockSpec((D_MODEL, D_MIXW), const, pipeline_mode=pl.Buffered(1)),
            pl.BlockSpec(memory_space=pltpu.SMEM),
        ],
        out_specs=(
            pl.BlockSpec((TOK_TILE, RET_V), mixed_tile),
            pl.BlockSpec((TOK_TILE, SWA_Q), mixed_tile),
        ),
        scratch_shapes=[
            pltpu.VMEM((TOK_TILE, D_MIXIN), bf16),
            pltpu.VMEM((TOK_TILE, D_MIXIN), bf16),
            pltpu.VMEM((CHUNKS_PER_TILE, RET_QK, C), bf16),
            pltpu.VMEM((CHUNKS_PER_TILE, RET_QK, C), bf16),
            pltpu.VMEM((TOK_TILE, RET_QK), f32),
            pltpu.VMEM((TOK_TILE, RET_QK), f32),
            pltpu.VMEM((RET_HEADS, RET_DK, RET_DV), f32),
            pltpu.VMEM((SWA_GROUPS, C, SWA_KV), bf16),
            pltpu.VMEM((C, SWA_KV), bf16),
            pltpu.VMEM((RET_HEADS, C, C), f32),
            pltpu.VMEM((2 * SWA_GROUPS, SWA_REP * C, 2 * C), f32),
        ],
        compiler_params=pltpu.CompilerParams(
            dimension_semantics=("arbitrary",), vmem_limit_bytes=VMEM_LIMIT_BYTES),
        name="proj_mixer",
    )(x2d, gain, w_mix, sinks)


def _post_kernel(x_ref, ret_ref, swa_ref, gpre_mix_ref, wg_ref, wro_ref, wso_ref, wo_ref, gpost_mix_ref,
                 gpre_mlp_ref, wup_ref, wdn_ref, gpost_mlp_ref, o_ref, *, sub_rows, ff_chunk):
    tm = x_ref.shape[0]
    ff_starts = list(range(0, D_FF, ff_chunk))

    def row_block(rows):
        x = x_ref[rows, :]
        pre_scale = _rms_scale(x)
        xg = (x * gpre_mix_ref[...]).astype(bf16)
        gate_r = jnp.dot(xg, wg_ref[:, 0:D_MODEL], preferred_element_type=f32)
        gate_s = jnp.dot(xg, wg_ref[:, D_MODEL:D_GATES], preferred_element_type=f32)
        y_r = jnp.dot(ret_ref[rows, :], wro_ref[...], preferred_element_type=f32)
        y_s = jnp.dot(swa_ref[rows, :], wso_ref[...], preferred_element_type=f32)
        yield
        merged = jax.nn.sigmoid(gate_r * pre_scale) * y_r + jax.nn.sigmoid(gate_s * pre_scale) * y_s
        z = jnp.dot(merged.astype(bf16), wo_ref[...], preferred_element_type=f32)
        yield
        x1 = x + z * _rms_scale(z) * gpost_mix_ref[...]
        mlp_scale = _rms_scale(x1)
        x1g = (x1 * gpre_mlp_ref[...]).astype(bf16)

        def up(c0):
            return jnp.dot(x1g, wup_ref[:, c0:c0 + ff_chunk], preferred_element_type=f32)

        acc = None
        u_next = up(ff_starts[0])
        yield
        for n, c0 in enumerate(ff_starts):
            u = u_next
            if n + 1 < len(ff_starts):
                u_next = up(ff_starts[n + 1])
            u = jnp.maximum(u * mlp_scale, 0.0)
            part = jnp.dot((u * u).astype(bf16), wdn_ref[c0:c0 + ff_chunk, :], preferred_element_type=f32)
            acc = part if acc is None else acc + part
            yield
        o_ref[rows, :] = x1 + acc * _rms_scale(acc) * gpost_mlp_ref[...]

    active = [row_block(slice(r0, r0 + sub_rows)) for r0 in range(0, tm, sub_rows)]
    while active:
        for gen in list(active):
            if next(gen, "done") == "done":
                active.remove(gen)


def _post(x2d, ret, swa, gpre_mix, w_gates, w_ro, w_so, w_o, gpost_mix, gpre_mlp, w_up, w_dn, gpost_mlp, *,
          tm=POST_TILE, sub_rows=256, ff_chunk=1024):
    n = x2d.shape[0]
    const = lambda i: (0, 0)
    tile = lambda width: pl.BlockSpec((tm, width), lambda i: (i, 0))
    weight = lambda shape: pl.BlockSpec(shape, const, pipeline_mode=pl.Buffered(1))
    gain = pl.BlockSpec((1, D_MODEL), const)
    return pl.pallas_call(
        functools.partial(_post_kernel, sub_rows=sub_rows, ff_chunk=ff_chunk),
        out_shape=jax.ShapeDtypeStruct((n, D_MODEL), f32),
        grid=(n // tm,),
        in_specs=[
            tile(D_MODEL), tile(RET_V), tile(SWA_Q),
            gain, weight((D_MODEL, D_GATES)),
            weight((RET_V, D_MODEL)), weight((SWA_Q, D_MODEL)), weight((D_MODEL, D_MODEL)), gain,
            gain, weight((D_MODEL, D_FF)), weight((D_FF, D_MODEL)), gain,
        ],
        out_specs=tile(D_MODEL),
        compiler_params=pltpu.CompilerParams(
            dimension_semantics=("arbitrary",), vmem_limit_bytes=VMEM_LIMIT_BYTES),
        name="post",
    )(x2d, ret, swa, gpre_mix, w_gates, w_ro, w_so, w_o, gpost_mix, gpre_mlp, w_up, w_dn, gpost_mlp)


def kernel(x, pre_mix_norm, w_in, w_ret_out, w_swa_out, w_out, sinks, post_mix_norm, pre_mlp_norm, w_up,
           w_down, post_mlp_norm):
    batch, seq, d = x.shape
    depth = w_in.shape[0]
    assert d == D_MODEL and seq % TOK_TILE == 0 and seq // TOK_TILE > 1
    col_order = _in_proj_column_order()
    row_order = _swa_out_row_order()
    x2d = x.reshape(batch * seq, d)
    for l in range(depth):
        pre_mix_gain = pre_mix_norm[l][None, :]
        ret, swa = _proj_mixer(x2d, pre_mix_gain, _take_runs(w_in[l], col_order, 1).astype(bf16),
                               sinks[l].astype(f32), seq)
        x2d = _post(x2d, ret, swa, pre_mix_gain, w_in[l][:, D_MIXW:D_MIXW + D_GATES].astype(bf16),
                    w_ret_out[l].astype(bf16), _take_runs(w_swa_out[l], row_order, 0).astype(bf16),
                    w_out[l].astype(bf16), post_mix_norm[l][None, :],
                    pre_mlp_norm[l][None, :], w_up[l].astype(bf16), w_down[l].astype(bf16),
                    post_mlp_norm[l][None, :])
    return x2d.reshape(batch, seq, d)
```

```python
import functools

import numpy as np
import jax
import jax.numpy as jnp
from jax import lax
from jax.experimental import pallas as pl
from jax.experimental.pallas import tpu as pltpu

D_MODEL = 1024
RET_HEADS = 4
RET_DK = 128
RET_DV = 256
CHUNK = 128
SWA_HEADS = 8
SWA_GROUPS = 2
SWA_REP = SWA_HEADS // SWA_GROUPS
SWA_DH = 64
D_FF = 4 * D_MODEL
EPS = 1e-6

RET_QK = RET_HEADS * RET_DK
RET_V = RET_HEADS * RET_DV
SWA_Q = SWA_HEADS * SWA_DH
SWA_KV = SWA_GROUPS * SWA_DH

W_QR = 0
W_KR = W_QR + RET_QK
W_VR = W_KR + RET_QK
W_GR = W_VR + RET_V
W_QS = W_GR + RET_V
W_KS = W_QS + SWA_Q
W_VS = W_KS + SWA_KV
D_MIXW = W_VS + SWA_KV
D_GATES = 2 * D_MODEL

M_QR = 0
M_KR = M_QR + RET_QK
M_VR = M_KR + RET_QK
M_SG = M_VR + RET_V
M_QD = M_SG + RET_V
M_QS = M_QD + RET_QK
M_KG = M_QS + SWA_Q
M_VS = M_KG + SWA_GROUPS * SWA_KV
D_MIXIN = M_VS + SWA_KV

TOK_TILE = 512
CHUNKS_PER_TILE = TOK_TILE // CHUNK
PROJ_PIECE_COLS = 256
POST_TILE = 512

VMEM_LIMIT_BYTES = 56 * 1024 * 1024

LOG_GAMMA = [float(np.log(1.0 - 2.0 ** (-5.0 - h))) for h in range(RET_HEADS)]
ALIBI_SLOPES = [float(2.0 ** (-8.0 / SWA_HEADS * (h + 1))) for h in range(SWA_HEADS)]
K_SCALE = float(RET_DK ** -0.5)
S_SCALE = float(SWA_DH ** -0.5)
LOG2E = float(np.log2(np.e))

f32 = jnp.float32
bf16 = jnp.bfloat16


def _in_proj_column_order():
    ref_splits = (RET_QK, RET_QK, RET_V, RET_V, SWA_Q, SWA_KV, SWA_KV)
    starts = np.concatenate([[0], np.cumsum(ref_splits)[:-1]])
    q_r, k_r, v_r, g_r, q_s, k_s, v_s = [np.arange(s, s + n) for s, n in zip(starts, ref_splits)]
    q_s = q_s.reshape(SWA_GROUPS, SWA_REP, SWA_DH).transpose(1, 0, 2).reshape(-1)
    return np.concatenate([q_r, k_r, v_r, g_r, q_s, k_s, v_s])


def _swa_out_row_order():
    return np.arange(SWA_Q).reshape(SWA_GROUPS, SWA_REP, SWA_DH).transpose(1, 0, 2).reshape(-1)


def _take_runs(w, order, axis):
    breaks = np.flatnonzero(np.diff(order) != 1) + 1
    pieces = [lax.slice_in_dim(w, int(run[0]), int(run[-1]) + 1, axis=axis) for run in np.split(order, breaks)]
    return jnp.concatenate(pieces, axis=axis)


def _rms_scale(v):
    return lax.rsqrt(jnp.mean(v * v, axis=-1, keepdims=True) + EPS)


def _sigmoid(v):
    return 0.5 * jnp.tanh(0.5 * v) + 0.5


def _cols(offset, index, width):
    return slice(offset + index * width, offset + (index + 1) * width)


def _init_tables(sink_ref, qdec_ref, kdec_ref, dmat_ref, bias_ref):
    C = CHUNK
    pos = (lax.broadcasted_iota(jnp.int32, (TOK_TILE, RET_DK), 0) % C).astype(f32)
    i = lax.broadcasted_iota(jnp.int32, (C, C), 0).astype(f32)
    j = lax.broadcasted_iota(jnp.int32, (C, C), 1).astype(f32)
    diff = i - j
    for h in range(RET_HEADS):
        qdec_ref[:, _cols(0, h, RET_DK)] = jnp.exp(LOG_GAMMA[h] * (pos + 1.0))
        kdec_ref[:, _cols(0, h, RET_DK)] = jnp.exp(LOG_GAMMA[h] * (C - 1.0 - pos)) * K_SCALE
        dmat_ref[h] = jnp.where(diff >= 0, jnp.exp(LOG_GAMMA[h] * jnp.maximum(diff, 0.0)), 0.0) * K_SCALE
    row = lax.broadcasted_iota(jnp.int32, (SWA_REP * C, 2 * C), 0)
    kpos = lax.broadcasted_iota(jnp.int32, (SWA_REP * C, 2 * C), 1)
    rep = row // C
    dist = (row - rep * C + C) - kpos
    valid = (dist >= 0) & (dist < C)
    distf = dist.astype(f32)
    for g in range(SWA_GROUPS):
        slope = jnp.zeros((SWA_REP * C, 2 * C), f32)
        sink = jnp.zeros((SWA_REP * C, 2 * C), f32)
        for r in range(SWA_REP):
            slope = jnp.where(rep == r, ALIBI_SLOPES[g * SWA_REP + r], slope)
            sink = jnp.where(rep == r, sink_ref[g * SWA_REP + r], sink)
        alibi = -slope * distf
        bias_ref[g] = jnp.where(kpos == 0, sink, jnp.where(valid, alibi, -jnp.inf)) * LOG2E
        bias_ref[SWA_GROUPS + g] = jnp.where(
            kpos == 0, sink, jnp.where(valid & (kpos >= C), alibi, -jnp.inf)) * LOG2E


def _projection_pieces(x_ref, g_ref, w_ref, qdec_ref, kdec_ref, mix_ref, kdt_ref):
    C = CHUNK
    shared = {}

    def prepare():
        x = x_ref[...]
        shared["h"] = (x * _rms_scale(x) * g_ref[...]).astype(bf16)

    def proj(c0, width):
        return jnp.dot(shared["h"], w_ref[:, c0:c0 + width], preferred_element_type=f32)

    width = PROJ_PIECE_COLS

    def swish_gate(part):
        def run():
            g = proj(W_GR + part * width, width)
            mix_ref[:, _cols(M_SG, part, width)] = (g * _sigmoid(g)).astype(bf16)
        return run

    def queries(part):
        def run():
            q = proj(W_QR + part * width, width)
            mix_ref[:, _cols(M_QR, part, width)] = q.astype(bf16)
            mix_ref[:, _cols(M_QD, part, width)] = (q * qdec_ref[:, _cols(0, part, width)]).astype(bf16)
        return run

    def keys(part):
        def run():
            k = proj(W_KR + part * width, width)
            mix_ref[:, _cols(M_KR, part, width)] = k.astype(bf16)
            kd = k * kdec_ref[:, _cols(0, part, width)]
            heads_per_piece = width // RET_DK
            for ci in range(CHUNKS_PER_TILE):
                for hd in range(heads_per_piece):
                    kdt_ref[ci, _cols(0, part * heads_per_piece + hd, RET_DK), :] = (
                        kd[ci * C:(ci + 1) * C, _cols(0, hd, RET_DK)].T.astype(bf16))
        return run

    def attention_queries(part):
        def run():
            mix_ref[:, _cols(M_QS, part, width)] = proj(W_QS + part * width, width).astype(bf16)
        return run

    def attention_keys_values():
        kv_s = proj(W_KS, 2 * SWA_KV)
        ks = kv_s[:, 0:SWA_KV] * (S_SCALE * LOG2E)
        lane = lax.broadcasted_iota(jnp.int32, (TOK_TILE, SWA_KV), 1)
        for grp in range(SWA_GROUPS):
            in_group = (lane >= grp * SWA_DH) & (lane < (grp + 1) * SWA_DH)
            mix_ref[:, _cols(M_KG, grp, SWA_KV)] = jnp.where(in_group, ks, 0.0).astype(bf16)
        mix_ref[:, M_VS:M_VS + SWA_KV] = kv_s[:, SWA_KV:2 * SWA_KV].astype(bf16)

    def values(part):
        def run():
            mix_ref[:, _cols(M_VR, part, width)] = proj(W_VR + part * width, width).astype(bf16)
        return run

    light = [queries(n) for n in range(RET_QK // width)]
    light += [keys(n) for n in range(RET_QK // width)]
    light += [attention_queries(n) for n in range(SWA_Q // width)]
    light += [attention_keys_values]
    light += [values(n) for n in range(RET_V // width)]
    heavy = [swish_gate(n) for n in range(RET_V // width)]
    pieces = []
    per_heavy = -(-len(light) // len(heavy))
    for n, piece in enumerate(heavy):
        pieces.append(piece)
        pieces += light[n * per_heavy:(n + 1) * per_heavy]
    return prepare, pieces


def _mixer_phases(mix_ref, kdt_ref, first_tile_of_seq, ret_ref, swa_ref,
                  state_ref, kprev_ref, vprev_ref, dmat_ref, bias_ref):
    C = CHUNK
    first_group = lax.broadcasted_iota(jnp.int32, (C, SWA_KV), 1) < SWA_DH
    ones_cols = jnp.ones((2 * C, SWA_KV), bf16)
    units = [(j, g) for j in range(SWA_REP) for g in range(SWA_GROUPS)]
    live = [dict() for _ in range(CHUNKS_PER_TILE)]

    def rows(t):
        return slice(t * C, (t + 1) * C)

    def zero_first_row(block):
        rows_per_vreg = 16
        top = block[0:rows_per_vreg, :]
        is_first = lax.broadcasted_iota(jnp.int32, top.shape, 0) == 0
        return jnp.concatenate([jnp.where(is_first, jnp.zeros_like(top), top), block[rows_per_vreg:, :]], axis=0)

    def first_matmuls(t):
        st = live[t]
        st["qk"] = [lax.dot_general(mix_ref[rows(t), _cols(M_QR, h, RET_DK)], mix_ref[rows(t), _cols(M_KR, h, RET_DK)],
                                    (((1,), (1,)), ((), ())), preferred_element_type=f32)
                    for h in range(RET_HEADS)]
        st["kv"] = [jnp.dot(kdt_ref[t, _cols(0, h, RET_DK), :], mix_ref[rows(t), _cols(M_VR, h, RET_DV)],
                            preferred_element_type=f32) for h in range(RET_HEADS)]
        if t == 0:
            keys = [jnp.concatenate([kprev_ref[g], mix_ref[0:C, _cols(M_KG, g, SWA_KV)]], axis=0)
                    for g in range(SWA_GROUPS)]
            vals = jnp.concatenate([vprev_ref[...], mix_ref[0:C, M_VS:M_VS + SWA_KV]], axis=0)
            table0 = jnp.where(first_tile_of_seq, SWA_GROUPS, 0)
        else:
            prev_and_current = slice((t - 1) * C, (t + 1) * C)
            keys = [mix_ref[prev_and_current, _cols(M_KG, g, SWA_KV)] for g in range(SWA_GROUPS)]
            vals = mix_ref[prev_and_current, M_VS:M_VS + SWA_KV]
            table0 = 0
        keys = [zero_first_row(k) for k in keys]
        st["vv_ones"] = jnp.concatenate([zero_first_row(vals), ones_cols], axis=1)
        st["s"] = {}
        for j, g in units:
            s = lax.dot_general(mix_ref[rows(t), _cols(M_QS, j, C)], keys[g], (((1,), (1,)), ((), ())),
                                preferred_element_type=f32)
            st["s"][j, g] = s + bias_ref[table0 + g, _cols(0, j, C), :]

    def retention_vector_work(t):
        st = live[t]
        st["a"], st["state_bf16"] = [], []
        for h in range(RET_HEADS):
            s_prev = state_ref[h]
            st["a"].append((st["qk"][h] * dmat_ref[h]).astype(bf16))
            st["state_bf16"].append(s_prev.astype(bf16))
            state_ref[h] = float(np.exp(LOG_GAMMA[h] * C)) * s_prev + st["kv"][h]
        del st["qk"], st["kv"]

    def softmax_numerators(t):
        st = live[t]
        st["p"] = {}
        for u in units:
            s = st["s"][u]
            st["p"][u] = jnp.exp2(s - jnp.max(s, axis=-1, keepdims=True)).astype(bf16)
        del st["s"]

    def second_matmuls(t):
        st = live[t]
        st["o"] = [jnp.dot(jnp.concatenate([st["a"][h], mix_ref[rows(t), _cols(M_QD, h, RET_DK)]], axis=1),
                           jnp.concatenate([mix_ref[rows(t), _cols(M_VR, h, RET_DV)], st["state_bf16"][h]], axis=0),
                           preferred_element_type=f32) for h in range(RET_HEADS)]
        st["r"] = {u: jnp.dot(st["p"][u], st["vv_ones"], preferred_element_type=f32) for u in units}
        del st["a"], st["state_bf16"], st["p"], st["vv_ones"]

    def retention_output(t):
        st = live[t]
        for h in range(RET_HEADS):
            o = st["o"][h]
            mu = jnp.mean(o, axis=-1, keepdims=True)
            d = o - mu
            var = jnp.mean(d * d, axis=-1, keepdims=True)
            ret_ref[rows(t), _cols(0, h, RET_DV)] = (
                mix_ref[rows(t), _cols(M_SG, h, RET_DV)].astype(f32) * (d * lax.rsqrt(var + EPS))).astype(bf16)
        del st["o"]

    def attention_output(t):
        st = live[t]
        for j in range(SWA_REP):
            num = [st["r"][j, g][:, 0:SWA_KV] for g in range(SWA_GROUPS)]
            den = [st["r"][j, g][:, SWA_KV:2 * SWA_KV] for g in range(SWA_GROUPS)]
            o_pair = jnp.where(first_group, num[0], num[1]) / jnp.where(first_group, den[0], den[1])
            swa_ref[rows(t), _cols(0, j, C)] = o_pair.astype(bf16)
        st.clear()

    def carry_last_block():
        last = slice(TOK_TILE - C, TOK_TILE)
        for g in range(SWA_GROUPS):
            kprev_ref[g] = mix_ref[last, _cols(M_KG, g, SWA_KV)]
        vprev_ref[...] = mix_ref[last, M_VS:M_VS + SWA_KV]

    phases = []
    for t in range(CHUNKS_PER_TILE + 1):
        if t < CHUNKS_PER_TILE:
            phases.append(functools.partial(first_matmuls, t))
        if t >= 1:
            phases += [functools.partial(fn, t - 1) for fn in (
                retention_vector_work, softmax_numerators, second_matmuls, retention_output, attention_output)]
    phases.append(carry_last_block)
    return phases


def _proj_mixer_kernel(x_ref, g_ref, w_ref, sink_ref, ret_ref, swa_ref,
                       mix_even_ref, mix_odd_ref, kdt_even_ref, kdt_odd_ref, qdec_ref, kdec_ref,
                       state_ref, kprev_ref, vprev_ref, dmat_ref, bias_ref, *, tiles_per_seq):
    s = pl.program_id(0)
    first_tile_of_seq = lax.rem(s + tiles_per_seq - 1, tiles_per_seq) == 0

    @pl.when(s == 0)
    def _first_step():
        _init_tables(sink_ref, qdec_ref, kdec_ref, dmat_ref, bias_ref)
        mix_odd_ref[...] = jnp.zeros_like(mix_odd_ref)
        kdt_odd_ref[...] = jnp.zeros_like(kdt_odd_ref)

    @pl.when((s == 0) | first_tile_of_seq)
    def _reset_sequence_state():
        state_ref[...] = jnp.zeros_like(state_ref)
        kprev_ref[...] = jnp.zeros_like(kprev_ref)
        vprev_ref[...] = jnp.zeros_like(vprev_ref)

    def step(write_mix_ref, write_kdt_ref, read_mix_ref, read_kdt_ref):
        prepare, pieces = _projection_pieces(x_ref, g_ref, w_ref, qdec_ref, kdec_ref, write_mix_ref, write_kdt_ref)
        phases = _mixer_phases(read_mix_ref, read_kdt_ref, first_tile_of_seq, ret_ref, swa_ref,
                               state_ref, kprev_ref, vprev_ref, dmat_ref, bias_ref)
        prepare()
        phases[0]()
        emitted = 0
        for n, phase in enumerate(phases[1:], start=1):
            while emitted < len(pieces) and emitted * (len(phases) - 1) < n * len(pieces):
                pieces[emitted]()
                emitted += 1
            phase()

    @pl.when(lax.rem(s, 2) == 0)
    def _even_step():
        step(mix_even_ref, kdt_even_ref, mix_odd_ref, kdt_odd_ref)

    @pl.when(lax.rem(s, 2) == 1)
    def _odd_step():
        step(mix_odd_ref, kdt_odd_ref, mix_even_ref, kdt_even_ref)


def _proj_mixer(x2d, gain, w_mix, sinks, seq):
    n = x2d.shape[0]
    nt = n // TOK_TILE
    C = CHUNK
    const = lambda s: (0, 0)
    projected_tile = lambda s: (jnp.minimum(s, nt - 1), 0)
    mixed_tile = lambda s: (jnp.maximum(s - 1, 0), 0)
    return pl.pallas_call(
        functools.partial(_proj_mixer_kernel, tiles_per_seq=seq // TOK_TILE),
        out_shape=(jax.ShapeDtypeStruct((n, RET_V), bf16), jax.ShapeDtypeStruct((n, SWA_Q), bf16)),
        grid=(nt + 1,),
        in_specs=[
            pl.BlockSpec((TOK_TILE, D_MODEL), projected_tile),
            pl.BlockSpec((1, D_MODEL), const),
            pl.BlockSpec((D_MODEL, D_MIXW), const, pipeline_mode=pl.Buffered(1)),
            pl.BlockSpec(memory_space=pltpu.SMEM),
        ],
        out_specs=(
            pl.BlockSpec((TOK_TILE, RET_V), mixed_tile),
            pl.BlockSpec((TOK_TILE, SWA_Q), mixed_tile),
        ),
        scratch_shapes=[
            pltpu.VMEM((TOK_TILE, D_MIXIN), bf16),
            pltpu.VMEM((TOK_TILE, D_MIXIN), bf16),
            pltpu.VMEM((CHUNKS_PER_TILE, RET_QK, C), bf16),
            pltpu.VMEM((CHUNKS_PER_TILE, RET_QK, C), bf16),
            pltpu.VMEM((TOK_TILE, RET_QK), f32),
            pltpu.VMEM((TOK_TILE, RET_QK), f32),
            pltpu.VMEM((RET_HEADS, RET_DK, RET_DV), f32),
            pltpu.VMEM((SWA_GROUPS, C, SWA_KV), bf16),
            pltpu.VMEM((C, SWA_KV), bf16),
            pltpu.VMEM((RET_HEADS, C, C), f32),
            pltpu.VMEM((2 * SWA_GROUPS, SWA_REP * C, 2 * C), f32),
        ],
        compiler_params=pltpu.CompilerParams(
            dimension_semantics=("arbitrary",), vmem_limit_bytes=VMEM_LIMIT_BYTES),
        name="proj_mixer",
    )(x2d, gain, w_mix, sinks)


def _post_kernel(x_ref, ret_ref, swa_ref, gpre_mix_ref, wg_ref, wro_ref, wso_ref, wo_ref, gpost_mix_ref,
                 gpre_mlp_ref, wup_ref, wdn_ref, gpost_mlp_ref, o_ref, *, sub_rows, ff_chunk):
    tm = x_ref.shape[0]
    ff_starts = list(range(0, D_FF, ff_chunk))

    def row_block(rows):
        x = x_ref[rows, :]
        y_r = jnp.dot(ret_ref[rows, :], wro_ref[...], preferred_element_type=f32)
        y_s = jnp.dot(swa_ref[rows, :], wso_ref[...], preferred_element_type=f32)
        h = (x * _rms_scale(x) * gpre_mix_ref[...]).astype(bf16)
        gate_r = jnp.dot(h, wg_ref[:, 0:D_MODEL], preferred_element_type=f32)
        gate_s = jnp.dot(h, wg_ref[:, D_MODEL:D_GATES], preferred_element_type=f32)
        yield
        merged = _sigmoid(gate_r) * y_r + _sigmoid(gate_s) * y_s
        z = jnp.dot(merged.astype(bf16), wo_ref[...], preferred_element_type=f32)
        yield
        x1 = x + z * _rms_scale(z) * gpost_mix_ref[...]
        h1 = (x1 * _rms_scale(x1) * gpre_mlp_ref[...]).astype(bf16)

        def up(c0):
            return jnp.dot(h1, wup_ref[:, c0:c0 + ff_chunk], preferred_element_type=f32)

        acc = None
        u_next = up(ff_starts[0])
        yield
        for n, c0 in enumerate(ff_starts):
            u = u_next
            if n + 1 < len(ff_starts):
                u_next = up(ff_starts[n + 1])
            u = jnp.maximum(u, 0.0)
            part = jnp.dot((u * u).astype(bf16), wdn_ref[c0:c0 + ff_chunk, :], preferred_element_type=f32)
            acc = part if acc is None else acc + part
            yield
        o_ref[rows, :] = x1 + acc * _rms_scale(acc) * gpost_mlp_ref[...]

    active = [row_block(slice(r0, r0 + sub_rows)) for r0 in range(0, tm, sub_rows)]
    while active:
        for gen in list(active):
            if next(gen, "done") == "done":
                active.remove(gen)


def _post(x2d, ret, swa, gpre_mix, w_gates, w_ro, w_so, w_o, gpost_mix, gpre_mlp, w_up, w_dn, gpost_mlp, *,
          tm=POST_TILE, sub_rows=256, ff_chunk=1024):
    n = x2d.shape[0]
    const = lambda i: (0, 0)
    tile = lambda width: pl.BlockSpec((tm, width), lambda i: (i, 0))
    weight = lambda shape: pl.BlockSpec(shape, const, pipeline_mode=pl.Buffered(1))
    gain = pl.BlockSpec((1, D_MODEL), const)
    return pl.pallas_call(
        functools.partial(_post_kernel, sub_rows=sub_rows, ff_chunk=ff_chunk),
        out_shape=jax.ShapeDtypeStruct((n, D_MODEL), f32),
        grid=(n // tm,),
        in_specs=[
            tile(D_MODEL), tile(RET_V), tile(SWA_Q),
            gain, weight((D_MODEL, D_GATES)),
            weight((RET_V, D_MODEL)), weight((SWA_Q, D_MODEL)), weight((D_MODEL, D_MODEL)), gain,
            gain, weight((D_MODEL, D_FF)), weight((D_FF, D_MODEL)), gain,
        ],
        out_specs=tile(D_MODEL),
        compiler_params=pltpu.CompilerParams(
            dimension_semantics=("arbitrary",), vmem_limit_bytes=VMEM_LIMIT_BYTES),
        name="post",
    )(x2d, ret, swa, gpre_mix, w_gates, w_ro, w_so, w_o, gpost_mix, gpre_mlp, w_up, w_dn, gpost_mlp)


def kernel(x, pre_mix_norm, w_in, w_ret_out, w_swa_out, w_out, sinks, post_mix_norm, pre_mlp_norm, w_up,
           w_down, post_mlp_norm):
    batch, seq, d = x.shape
    depth = w_in.shape[0]
    assert d == D_MODEL and seq % TOK_TILE == 0 and seq // TOK_TILE > 1
    col_order = _in_proj_column_order()
    row_order = _swa_out_row_order()
    x2d = x.reshape(batch * seq, d)
    for l in range(depth):
        pre_mix_gain = pre_mix_norm[l][None, :]
        ret, swa = _proj_mixer(x2d, pre_mix_gain, _take_runs(w_in[l], col_order, 1).astype(bf16),
                               sinks[l].astype(f32), seq)
        x2d = _post(x2d, ret, swa, pre_mix_gain, w_in[l][:, D_MIXW:D_MIXW + D_GATES].astype(bf16),
                    w_ret_out[l].astype(bf16), _take_runs(w_swa_out[l], row_order, 0).astype(bf16),
                    w_out[l].astype(bf16), post_mix_norm[l][None, :],
                    pre_mlp_norm[l][None, :], w_up[l].astype(bf16), w_down[l].astype(bf16),
                    post_mlp_norm[l][None, :])
    return x2d.reshape(batch, seq, d)
```

```python
import functools

import numpy as np
import jax
import jax.numpy as jnp
from jax import lax
from jax.experimental import pallas as pl
from jax.experimental.pallas import tpu as pltpu

D_MODEL = 1024
RET_HEADS = 4
RET_DK = 128
RET_DV = 256
CHUNK = 128
SWA_HEADS = 8
SWA_GROUPS = 2
SWA_REP = SWA_HEADS // SWA_GROUPS
SWA_DH = 64
D_FF = 4 * D_MODEL
EPS = 1e-6

RET_QK = RET_HEADS * RET_DK
RET_V = RET_HEADS * RET_DV
SWA_Q = SWA_HEADS * SWA_DH
SWA_KV = SWA_GROUPS * SWA_DH

W_QR = 0
W_KR = W_QR + RET_QK
W_VR = W_KR + RET_QK
W_GR = W_VR + RET_V
W_QS = W_GR + RET_V
W_KS = W_QS + SWA_Q
W_VS = W_KS + SWA_KV
D_MIXW = W_VS + SWA_KV
D_GATES = 2 * D_MODEL

M_QR = 0
M_KR = M_QR + RET_QK
M_VR = M_KR + RET_QK
M_SG = M_VR + RET_V
M_QD = M_SG + RET_V
M_QS = M_QD + RET_QK
M_KG = M_QS + SWA_Q
M_VS = M_KG + SWA_GROUPS * SWA_KV
D_MIXIN = M_VS + SWA_KV

TOK_TILE = 512
CHUNKS_PER_TILE = TOK_TILE // CHUNK
PROJ_PIECE_COLS = 256
POST_TILE = 512
STAGE_ROWS, STAGE_COLS = 512, 1024

VMEM_LIMIT_BYTES = 56 * 1024 * 1024

LOG_GAMMA = [float(np.log(1.0 - 2.0 ** (-5.0 - h))) for h in range(RET_HEADS)]
ALIBI_SLOPES = [float(2.0 ** (-8.0 / SWA_HEADS * (h + 1))) for h in range(SWA_HEADS)]
K_SCALE = float(RET_DK ** -0.5)
S_SCALE = float(SWA_DH ** -0.5)
LOG2E = float(np.log2(np.e))

f32 = jnp.float32
bf16 = jnp.bfloat16


def _in_proj_column_order():
    ref_splits = (RET_QK, RET_QK, RET_V, RET_V, SWA_Q, SWA_KV, SWA_KV)
    starts = np.concatenate([[0], np.cumsum(ref_splits)[:-1]])
    q_r, k_r, v_r, g_r, q_s, k_s, v_s = [np.arange(s, s + n) for s, n in zip(starts, ref_splits)]
    q_s = q_s.reshape(SWA_GROUPS, SWA_REP, SWA_DH).transpose(1, 0, 2).reshape(-1)
    return np.concatenate([q_r, k_r, v_r, g_r, q_s, k_s, v_s])


def _swa_out_row_order():
    return np.arange(SWA_Q).reshape(SWA_GROUPS, SWA_REP, SWA_DH).transpose(1, 0, 2).reshape(-1)


def _take_runs(w, order, axis):
    breaks = np.flatnonzero(np.diff(order) != 1) + 1
    pieces = [lax.slice_in_dim(w, int(run[0]), int(run[-1]) + 1, axis=axis) for run in np.split(order, breaks)]
    return jnp.concatenate(pieces, axis=axis)


def _rms_scale(v):
    return lax.rsqrt(jnp.mean(v * v, axis=-1, keepdims=True) + EPS)


def _sigmoid(v):
    return 0.5 * jnp.tanh(0.5 * v) + 0.5


def _cols(offset, index, width):
    return slice(offset + index * width, offset + (index + 1) * width)


def _init_tables(sink_ref, qdec_ref, kdec_ref, dmat_ref, bias_ref):
    C = CHUNK
    pos = (lax.broadcasted_iota(jnp.int32, (TOK_TILE, RET_DK), 0) % C).astype(f32)
    i = lax.broadcasted_iota(jnp.int32, (C, C), 0).astype(f32)
    j = lax.broadcasted_iota(jnp.int32, (C, C), 1).astype(f32)
    diff = i - j
    for h in range(RET_HEADS):
        qdec_ref[:, _cols(0, h, RET_DK)] = jnp.exp(LOG_GAMMA[h] * (pos + 1.0))
        kdec_ref[:, _cols(0, h, RET_DK)] = jnp.exp(LOG_GAMMA[h] * (C - 1.0 - pos)) * K_SCALE
        dmat_ref[h] = jnp.where(diff >= 0, jnp.exp(LOG_GAMMA[h] * jnp.maximum(diff, 0.0)), 0.0) * K_SCALE
    row = lax.broadcasted_iota(jnp.int32, (SWA_REP * C, 2 * C), 0)
    kpos = lax.broadcasted_iota(jnp.int32, (SWA_REP * C, 2 * C), 1)
    rep = row // C
    dist = (row - rep * C + C) - kpos
    valid = (dist >= 0) & (dist < C)
    distf = dist.astype(f32)
    for g in range(SWA_GROUPS):
        slope = jnp.zeros((SWA_REP * C, 2 * C), f32)
        sink = jnp.zeros((SWA_REP * C, 2 * C), f32)
        for r in range(SWA_REP):
            slope = jnp.where(rep == r, ALIBI_SLOPES[g * SWA_REP + r], slope)
            sink = jnp.where(rep == r, sink_ref[g * SWA_REP + r], sink)
        alibi = -slope * distf
        bias_ref[g] = jnp.where(kpos == 0, sink, jnp.where(valid, alibi, -jnp.inf)) * LOG2E
        bias_ref[SWA_GROUPS + g] = jnp.where(
            kpos == 0, sink, jnp.where(valid & (kpos >= C), alibi, -jnp.inf)) * LOG2E


def _projection_pieces(x_ref, g_ref, w_ref, qdec_ref, kdec_ref, mix_ref, kdt_ref):
    C = CHUNK
    shared = {}

    def prepare():
        x = x_ref[...]
        shared["h"] = (x * _rms_scale(x) * g_ref[...]).astype(bf16)

    def proj(c0, width):
        return jnp.dot(shared["h"], w_ref[:, c0:c0 + width], preferred_element_type=f32)

    width = PROJ_PIECE_COLS

    def swish_gate(part):
        def run():
            g = proj(W_GR + part * width, width)
            mix_ref[:, _cols(M_SG, part, width)] = (g * _sigmoid(g)).astype(bf16)
        return run

    def queries(part):
        def run():
            q = proj(W_QR + part * width, width)
            mix_ref[:, _cols(M_QR, part, width)] = q.astype(bf16)
            mix_ref[:, _cols(M_QD, part, width)] = (q * qdec_ref[:, _cols(0, part, width)]).astype(bf16)
        return run

    def keys(part):
        def run():
            k = proj(W_KR + part * width, width)
            mix_ref[:, _cols(M_KR, part, width)] = k.astype(bf16)
            kd = k * kdec_ref[:, _cols(0, part, width)]
            heads_per_piece = width // RET_DK
            for ci in range(CHUNKS_PER_TILE):
                for hd in range(heads_per_piece):
                    kdt_ref[ci, _cols(0, part * heads_per_piece + hd, RET_DK), :] = (
                        kd[ci * C:(ci + 1) * C, _cols(0, hd, RET_DK)].T.astype(bf16))
        return run

    def attention_queries(part):
        def run():
            mix_ref[:, _cols(M_QS, part, width)] = proj(W_QS + part * width, width).astype(bf16)
        return run

    def attention_keys_values():
        kv_s = proj(W_KS, 2 * SWA_KV)
        ks = kv_s[:, 0:SWA_KV] * (S_SCALE * LOG2E)
        lane = lax.broadcasted_iota(jnp.int32, (TOK_TILE, SWA_KV), 1)
        for grp in range(SWA_GROUPS):
            in_group = (lane >= grp * SWA_DH) & (lane < (grp + 1) * SWA_DH)
            mix_ref[:, _cols(M_KG, grp, SWA_KV)] = jnp.where(in_group, ks, 0.0).astype(bf16)
        mix_ref[:, M_VS:M_VS + SWA_KV] = kv_s[:, SWA_KV:2 * SWA_KV].astype(bf16)

    def values(part):
        def run():
            mix_ref[:, _cols(M_VR, part, width)] = proj(W_VR + part * width, width).astype(bf16)
        return run

    light = [queries(n) for n in range(RET_QK // width)]
    light += [keys(n) for n in range(RET_QK // width)]
    light += [attention_queries(n) for n in range(SWA_Q // width)]
    light += [attention_keys_values]
    light += [values(n) for n in range(RET_V // width)]
    heavy = [swish_gate(n) for n in range(RET_V // width)]
    pieces = []
    per_heavy = -(-len(light) // len(heavy))
    for n, piece in enumerate(heavy):
        pieces.append(piece)
        pieces += light[n * per_heavy:(n + 1) * per_heavy]
    return prepare, pieces


def _mixer_phases(mix_ref, kdt_ref, first_tile_of_seq, ret_ref, swa_ref,
                  state_ref, kprev_ref, vprev_ref, dmat_ref, bias_ref):
    C = CHUNK
    first_group = lax.broadcasted_iota(jnp.int32, (C, SWA_KV), 1) < SWA_DH
    ones_cols = jnp.ones((2 * C, SWA_KV), bf16)
    units = [(j, g) for j in range(SWA_REP) for g in range(SWA_GROUPS)]
    live = [dict() for _ in range(CHUNKS_PER_TILE)]

    def rows(t):
        return slice(t * C, (t + 1) * C)

    def zero_first_row(block):
        rows_per_vreg = 16
        top = block[0:rows_per_vreg, :]
        is_first = lax.broadcasted_iota(jnp.int32, top.shape, 0) == 0
        return jnp.concatenate([jnp.where(is_first, jnp.zeros_like(top), top), block[rows_per_vreg:, :]], axis=0)

    def first_matmuls(t):
        st = live[t]
        st["qk"] = [lax.dot_general(mix_ref[rows(t), _cols(M_QR, h, RET_DK)], mix_ref[rows(t), _cols(M_KR, h, RET_DK)],
                                    (((1,), (1,)), ((), ())), preferred_element_type=f32)
                    for h in range(RET_HEADS)]
        st["kv"] = [jnp.dot(kdt_ref[t, _cols(0, h, RET_DK), :], mix_ref[rows(t), _cols(M_VR, h, RET_DV)],
                            preferred_element_type=f32) for h in range(RET_HEADS)]
        if t == 0:
            keys = [jnp.concatenate([kprev_ref[g], mix_ref[0:C, _cols(M_KG, g, SWA_KV)]], axis=0)
                    for g in range(SWA_GROUPS)]
            vals = jnp.concatenate([vprev_ref[...], mix_ref[0:C, M_VS:M_VS + SWA_KV]], axis=0)
            table0 = jnp.where(first_tile_of_seq, SWA_GROUPS, 0)
        else:
            prev_and_current = slice((t - 1) * C, (t + 1) * C)
            keys = [mix_ref[prev_and_current, _cols(M_KG, g, SWA_KV)] for g in range(SWA_GROUPS)]
            vals = mix_ref[prev_and_current, M_VS:M_VS + SWA_KV]
            table0 = 0
        keys = [zero_first_row(k) for k in keys]
        st["vv_ones"] = jnp.concatenate([zero_first_row(vals), ones_cols], axis=1)
        st["s"] = {}
        for j, g in units:
            s = lax.dot_general(mix_ref[rows(t), _cols(M_QS, j, C)], keys[g], (((1,), (1,)), ((), ())),
                                preferred_element_type=f32)
            st["s"][j, g] = s + bias_ref[table0 + g, _cols(0, j, C), :]

    def retention_vector_work(t):
        st = live[t]
        st["a"], st["state_bf16"] = [], []
        for h in range(RET_HEADS):
            s_prev = state_ref[h]
            st["a"].append((st["qk"][h] * dmat_ref[h]).astype(bf16))
            st["state_bf16"].append(s_prev.astype(bf16))
            state_ref[h] = float(np.exp(LOG_GAMMA[h] * C)) * s_prev + st["kv"][h]
        del st["qk"], st["kv"]

    def softmax_numerators(t):
        st = live[t]
        st["p"] = {}
        for u in units:
            s = st["s"][u]
            st["p"][u] = jnp.exp2(s - jnp.max(s, axis=-1, keepdims=True)).astype(bf16)
        del st["s"]

    def second_matmuls(t):
        st = live[t]
        st["o"] = [jnp.dot(jnp.concatenate([st["a"][h], mix_ref[rows(t), _cols(M_QD, h, RET_DK)]], axis=1),
                           jnp.concatenate([mix_ref[rows(t), _cols(M_VR, h, RET_DV)], st["state_bf16"][h]], axis=0),
                           preferred_element_type=f32) for h in range(RET_HEADS)]
        st["r"] = {u: jnp.dot(st["p"][u], st["vv_ones"], preferred_element_type=f32) for u in units}
        del st["a"], st["state_bf16"], st["p"], st["vv_ones"]

    def retention_output(t):
        st = live[t]
        for h in range(RET_HEADS):
            o = st["o"][h]
            mu = jnp.mean(o, axis=-1, keepdims=True)
            d = o - mu
            var = jnp.mean(d * d, axis=-1, keepdims=True)
            ret_ref[rows(t), _cols(0, h, RET_DV)] = (
                mix_ref[rows(t), _cols(M_SG, h, RET_DV)].astype(f32) * (d * lax.rsqrt(var + EPS))).astype(bf16)
        del st["o"]

    def attention_output(t):
        st = live[t]
        for j in range(SWA_REP):
            num = [st["r"][j, g][:, 0:SWA_KV] for g in range(SWA_GROUPS)]
            den = [st["r"][j, g][:, SWA_KV:2 * SWA_KV] for g in range(SWA_GROUPS)]
            o_pair = jnp.where(first_group, num[0], num[1]) / jnp.where(first_group, den[0], den[1])
            swa_ref[rows(t), _cols(0, j, C)] = o_pair.astype(bf16)
        st.clear()

    def carry_last_block():
        last = slice(TOK_TILE - C, TOK_TILE)
        for g in range(SWA_GROUPS):
            kprev_ref[g] = mix_ref[last, _cols(M_KG, g, SWA_KV)]
        vprev_ref[...] = mix_ref[last, M_VS:M_VS + SWA_KV]

    phases = []
    for t in range(CHUNKS_PER_TILE + 1):
        if t < CHUNKS_PER_TILE:
            phases.append(functools.partial(first_matmuls, t))
        if t >= 1:
            phases += [functools.partial(fn, t - 1) for fn in (
                retention_vector_work, softmax_numerators, second_matmuls, retention_output, attention_output)]
    phases.append(carry_last_block)
    return phases


def _proj_mixer_kernel(x_ref, g_ref, w_ref, sink_ref, ret_ref, swa_ref,
                       mix_even_ref, mix_odd_ref, kdt_even_ref, kdt_odd_ref, qdec_ref, kdec_ref,
                       state_ref, kprev_ref, vprev_ref, dmat_ref, bias_ref, *, tiles_per_seq):
    s = pl.program_id(0)
    first_tile_of_seq = lax.rem(s + tiles_per_seq - 1, tiles_per_seq) == 0

    @pl.when(s == 0)
    def _first_step():
        _init_tables(sink_ref, qdec_ref, kdec_ref, dmat_ref, bias_ref)
        mix_odd_ref[...] = jnp.zeros_like(mix_odd_ref)
        kdt_odd_ref[...] = jnp.zeros_like(kdt_odd_ref)

    @pl.when((s == 0) | first_tile_of_seq)
    def _reset_sequence_state():
        state_ref[...] = jnp.zeros_like(state_ref)
        kprev_ref[...] = jnp.zeros_like(kprev_ref)
        vprev_ref[...] = jnp.zeros_like(vprev_ref)

    def step(write_mix_ref, write_kdt_ref, read_mix_ref, read_kdt_ref):
        prepare, pieces = _projection_pieces(x_ref, g_ref, w_ref, qdec_ref, kdec_ref, write_mix_ref, write_kdt_ref)
        phases = _mixer_phases(read_mix_ref, read_kdt_ref, first_tile_of_seq, ret_ref, swa_ref,
                               state_ref, kprev_ref, vprev_ref, dmat_ref, bias_ref)
        prepare()
        phases[0]()
        emitted = 0
        for n, phase in enumerate(phases[1:], start=1):
            while emitted < len(pieces) and emitted * (len(phases) - 1) < n * len(pieces):
                pieces[emitted]()
                emitted += 1
            phase()

    @pl.when(lax.rem(s, 2) == 0)
    def _even_step():
        step(mix_even_ref, kdt_even_ref, mix_odd_ref, kdt_odd_ref)

    @pl.when(lax.rem(s, 2) == 1)
    def _odd_step():
        step(mix_odd_ref, kdt_odd_ref, mix_even_ref, kdt_even_ref)


def _proj_mixer(x2d, gain, w_mix, sinks, seq):
    n = x2d.shape[0]
    nt = n // TOK_TILE
    C = CHUNK
    const = lambda s: (0, 0)
    projected_tile = lambda s: (jnp.minimum(s, nt - 1), 0)
    mixed_tile = lambda s: (jnp.maximum(s - 1, 0), 0)
    return pl.pallas_call(
        functools.partial(_proj_mixer_kernel, tiles_per_seq=seq // TOK_TILE),
        out_shape=(jax.ShapeDtypeStruct((n, RET_V), bf16), jax.ShapeDtypeStruct((n, SWA_Q), bf16)),
        grid=(nt + 1,),
        in_specs=[
            pl.BlockSpec((TOK_TILE, D_MODEL), projected_tile),
            pl.BlockSpec((1, D_MODEL), const),
            pl.BlockSpec((D_MODEL, D_MIXW), const, pipeline_mode=pl.Buffered(1)),
            pl.BlockSpec(memory_space=pltpu.SMEM),
        ],
        out_specs=(
            pl.BlockSpec((TOK_TILE, RET_V), mixed_tile),
            pl.BlockSpec((TOK_TILE, SWA_Q), mixed_tile),
        ),
        scratch_shapes=[
            pltpu.VMEM((TOK_TILE, D_MIXIN), bf16),
            pltpu.VMEM((TOK_TILE, D_MIXIN), bf16),
            pltpu.VMEM((CHUNKS_PER_TILE, RET_QK, C), bf16),
            pltpu.VMEM((CHUNKS_PER_TILE, RET_QK, C), bf16),
            pltpu.VMEM((TOK_TILE, RET_QK), f32),
            pltpu.VMEM((TOK_TILE, RET_QK), f32),
            pltpu.VMEM((RET_HEADS, RET_DK, RET_DV), f32),
            pltpu.VMEM((SWA_GROUPS, C, SWA_KV), bf16),
            pltpu.VMEM((C, SWA_KV), bf16),
            pltpu.VMEM((RET_HEADS, C, C), f32),
            pltpu.VMEM((2 * SWA_GROUPS, SWA_REP * C, 2 * C), f32),
        ],
        compiler_params=pltpu.CompilerParams(
            dimension_semantics=("arbitrary",), vmem_limit_bytes=VMEM_LIMIT_BYTES),
        name="proj_mixer",
    )(x2d, gain, w_mix, sinks)


def _load_post_weights(w_in_hbm, w_ro_hbm, w_so_hbm, w_o_hbm, w_up_hbm, w_dn_hbm,
                       wg_ref, wro_ref, wso_ref, wo_ref, wup_ref, wdn_ref, stage_ref, sem_ref):
    R, Cc = STAGE_ROWS, STAGE_COLS
    whole = slice(0, R)

    def grid_jobs(src_hbm, dst_ref, col0=0):
        n_rows, n_cols = dst_ref.shape
        return [([(src_hbm.at[r0:r0 + R, col0 + c0:col0 + c0 + Cc], whole)], dst_ref.at[r0:r0 + R, c0:c0 + Cc])
                for r0 in range(0, n_rows, R) for c0 in range(0, n_cols, Cc)]

    jobs = grid_jobs(w_in_hbm, wg_ref, col0=D_MIXW)
    jobs += grid_jobs(w_ro_hbm, wro_ref)
    runs = _swa_out_row_order().reshape(-1, SWA_DH)[:, 0]
    jobs += [([(w_so_hbm.at[int(r0):int(r0) + SWA_DH, :], slice(n * SWA_DH, (n + 1) * SWA_DH))
               for n, r0 in enumerate(runs)], wso_ref)]
    jobs += grid_jobs(w_o_hbm, wo_ref)
    jobs += grid_jobs(w_up_hbm, wup_ref)
    jobs += grid_jobs(w_dn_hbm, wdn_ref)

    def copies(job, slot):
        return [pltpu.make_async_copy(src, stage_ref.at[slot, rows, :], sem_ref.at[slot]) for src, rows in job[0]]

    for copy in copies(jobs[0], 0):
        copy.start()
    for k, job in enumerate(jobs):
        slot = k % 2
        if k + 1 < len(jobs):
            for copy in copies(jobs[k + 1], 1 - slot):
                copy.start()
        for copy in copies(job, slot):
            copy.wait()
        job[1][...] = stage_ref[slot].astype(bf16)


def _post_kernel(x_ref, ret_ref, swa_ref, gpre_mix_ref, w_in_hbm, w_ro_hbm, w_so_hbm, w_o_hbm, gpost_mix_ref,
                 gpre_mlp_ref, w_up_hbm, w_dn_hbm, gpost_mlp_ref, o_ref,
                 wg_ref, wro_ref, wso_ref, wo_ref, wup_ref, wdn_ref, stage_ref, sem_ref, *, sub_rows, ff_chunk):
    tm = x_ref.shape[0]
    ff_starts = list(range(0, D_FF, ff_chunk))

    @pl.when(pl.program_id(0) == 0)
    def _first_step():
        _load_post_weights(w_in_hbm, w_ro_hbm, w_so_hbm, w_o_hbm, w_up_hbm, w_dn_hbm,
                           wg_ref, wro_ref, wso_ref, wo_ref, wup_ref, wdn_ref, stage_ref, sem_ref)

    def row_block(rows):
        x = x_ref[rows, :]
        y_r = jnp.dot(ret_ref[rows, :], wro_ref[...], preferred_element_type=f32)
        y_s = jnp.dot(swa_ref[rows, :], wso_ref[...], preferred_element_type=f32)
        h = (x * _rms_scale(x) * gpre_mix_ref[...]).astype(bf16)
        gate_r = jnp.dot(h, wg_ref[:, 0:D_MODEL], preferred_element_type=f32)
        gate_s = jnp.dot(h, wg_ref[:, D_MODEL:D_GATES], preferred_element_type=f32)
        yield
        merged = _sigmoid(gate_r) * y_r + _sigmoid(gate_s) * y_s
        z = jnp.dot(merged.astype(bf16), wo_ref[...], preferred_element_type=f32)
        yield
        x1 = x + z * _rms_scale(z) * gpost_mix_ref[...]
        h1 = (x1 * _rms_scale(x1) * gpre_mlp_ref[...]).astype(bf16)

        def up(c0):
            return jnp.dot(h1, wup_ref[:, c0:c0 + ff_chunk], preferred_element_type=f32)

        acc = None
        u_next = up(ff_starts[0])
        yield
        for n, c0 in enumerate(ff_starts):
            u = u_next
            if n + 1 < len(ff_starts):
                u_next = up(ff_starts[n + 1])
            u = jnp.maximum(u, 0.0)
            part = jnp.dot((u * u).astype(bf16), wdn_ref[c0:c0 + ff_chunk, :], preferred_element_type=f32)
            acc = part if acc is None else acc + part
            yield
        o_ref[rows, :] = x1 + acc * _rms_scale(acc) * gpost_mlp_ref[...]

    active = [row_block(slice(r0, r0 + sub_rows)) for r0 in range(0, tm, sub_rows)]
    while active:
        for gen in list(active):
            if next(gen, "done") == "done":
                active.remove(gen)


def _post(x2d, ret, swa, gpre_mix, w_in, w_ro, w_so, w_o, gpost_mix, gpre_mlp, w_up, w_dn, gpost_mlp, *,
          tm=POST_TILE, sub_rows=256, ff_chunk=1024):
    n = x2d.shape[0]
    const = lambda i: (0, 0)
    tile = lambda width: pl.BlockSpec((tm, width), lambda i: (i, 0))
    in_hbm = pl.BlockSpec(memory_space=pl.ANY)
    gain = pl.BlockSpec((1, D_MODEL), const)
    return pl.pallas_call(
        functools.partial(_post_kernel, sub_rows=sub_rows, ff_chunk=ff_chunk),
        out_shape=jax.ShapeDtypeStruct((n, D_MODEL), f32),
        grid=(n // tm,),
        in_specs=[
            tile(D_MODEL), tile(RET_V), tile(SWA_Q),
            gain, in_hbm,
            in_hbm, in_hbm, in_hbm, gain,
            gain, in_hbm, in_hbm, gain,
        ],
        out_specs=tile(D_MODEL),
        scratch_shapes=[
            pltpu.VMEM((D_MODEL, D_GATES), bf16),
            pltpu.VMEM((RET_V, D_MODEL), bf16),
            pltpu.VMEM((SWA_Q, D_MODEL), bf16),
            pltpu.VMEM((D_MODEL, D_MODEL), bf16),
            pltpu.VMEM((D_MODEL, D_FF), bf16),
            pltpu.VMEM((D_FF, D_MODEL), bf16),
            pltpu.VMEM((2, STAGE_ROWS, STAGE_COLS), f32),
            pltpu.SemaphoreType.DMA((2,)),
        ],
        compiler_params=pltpu.CompilerParams(
            dimension_semantics=("arbitrary",), vmem_limit_bytes=VMEM_LIMIT_BYTES),
        name="post",
    )(x2d, ret, swa, gpre_mix, w_in, w_ro, w_so, w_o, gpost_mix, gpre_mlp, w_up, w_dn, gpost_mlp)


def kernel(x, pre_mix_norm, w_in, w_ret_out, w_swa_out, w_out, sinks, post_mix_norm, pre_mlp_norm, w_up,
           w_down, post_mlp_norm):
    batch, seq, d = x.shape
    depth = w_in.shape[0]
    assert d == D_MODEL and seq % TOK_TILE == 0 and seq // TOK_TILE > 1
    col_order = _in_proj_column_order()
    x2d = x.reshape(batch * seq, d)
    for l in range(depth):
        pre_mix_gain = pre_mix_norm[l][None, :]
        ret, swa = _proj_mixer(x2d, pre_mix_gain, _take_runs(w_in[l], col_order, 1).astype(bf16),
                               sinks[l].astype(f32), seq)
        x2d = _post(x2d, ret, swa, pre_mix_gain, w_in[l], w_ret_out[l], w_swa_out[l], w_out[l],
                    post_mix_norm[l][None, :], pre_mlp_norm[l][None, :], w_up[l], w_down[l],
                    post_mlp_norm[l][None, :])
    return x2d.reshape(batch, seq, d)
```

```python
import functools

import numpy as np
import jax
import jax.numpy as jnp
from jax import lax
from jax.experimental import pallas as pl
from jax.experimental.pallas import tpu as pltpu

D_MODEL = 1024
RET_HEADS = 4
RET_DK = 128
RET_DV = 256
CHUNK = 128
SWA_HEADS = 8
SWA_GROUPS = 2
SWA_REP = SWA_HEADS // SWA_GROUPS
SWA_DH = 64
D_FF = 4 * D_MODEL
EPS = 1e-6

RET_QK = RET_HEADS * RET_DK
RET_V = RET_HEADS * RET_DV
SWA_Q = SWA_HEADS * SWA_DH
SWA_KV = SWA_GROUPS * SWA_DH

W_QR = 0
W_KR = W_QR + RET_QK
W_VR = W_KR + RET_QK
W_GR = W_VR + RET_V
W_QS = W_GR + RET_V
W_KS = W_QS + SWA_Q
W_VS = W_KS + SWA_KV
D_MIXW = W_VS + SWA_KV
D_GATES = 2 * D_MODEL

M_QR = 0
M_KR = M_QR + RET_QK
M_VR = M_KR + RET_QK
M_SG = M_VR + RET_V
M_QD = M_SG + RET_V
M_QS = M_QD + RET_QK
M_KG = M_QS + SWA_Q
M_VS = M_KG + SWA_GROUPS * SWA_KV
D_MIXIN = M_VS + SWA_KV

TOK_TILE = 512
CHUNKS_PER_TILE = TOK_TILE // CHUNK
PROJ_PIECE_COLS = 256
POST_TILE = 512
STAGE_ROWS, STAGE_COLS = 512, 1024

VMEM_LIMIT_BYTES = 56 * 1024 * 1024

LOG_GAMMA = [float(np.log(1.0 - 2.0 ** (-5.0 - h))) for h in range(RET_HEADS)]
ALIBI_SLOPES = [float(2.0 ** (-8.0 / SWA_HEADS * (h + 1))) for h in range(SWA_HEADS)]
K_SCALE = float(RET_DK ** -0.5)
S_SCALE = float(SWA_DH ** -0.5)
LOG2E = float(np.log2(np.e))

f32 = jnp.float32
bf16 = jnp.bfloat16


def _swa_out_row_order():
    return np.arange(SWA_Q).reshape(SWA_GROUPS, SWA_REP, SWA_DH).transpose(1, 0, 2).reshape(-1)


def _rms_scale(v):
    return lax.rsqrt(jnp.mean(v * v, axis=-1, keepdims=True) + EPS)


def _sigmoid(v):
    return 0.5 * jnp.tanh(0.5 * v) + 0.5


def _cols(offset, index, width):
    return slice(offset + index * width, offset + (index + 1) * width)


def _store_bf16(dst_ref, chunk):
    dst_ref[...] = chunk[0:dst_ref.shape[0], 0:dst_ref.shape[1]].astype(bf16)


def _plain_weight_jobs(src_hbm, dst_ref, col0=0):
    n_rows, n_cols = dst_ref.shape
    jobs = []
    for r0 in range(0, n_rows, STAGE_ROWS):
        for c0 in range(0, n_cols, STAGE_COLS):
            rows, cols = min(STAGE_ROWS, n_rows - r0), min(STAGE_COLS, n_cols - c0)
            piece = (src_hbm.at[r0:r0 + rows, col0 + c0:col0 + c0 + cols], slice(0, rows), slice(0, cols))
            jobs.append(([piece], functools.partial(_store_bf16, dst_ref.at[r0:r0 + rows, c0:c0 + cols])))
    return jobs


def _stream_weight_chunks(jobs, stage_ref, sem_ref):
    def copies(job, slot):
        return [pltpu.make_async_copy(src, stage_ref.at[slot, rows, cols], sem_ref.at[slot])
                for src, rows, cols in job[0]]

    for copy in copies(jobs[0], 0):
        copy.start()
    for k, job in enumerate(jobs):
        slot = k % 2
        if k + 1 < len(jobs):
            for copy in copies(jobs[k + 1], 1 - slot):
                copy.start()
        for copy in copies(job, slot):
            copy.wait()
        job[1](stage_ref[slot])


def _init_tables(sink_ref, qdec_ref, kdec_ref, dmat_ref, bias_ref):
    C = CHUNK
    pos = (lax.broadcasted_iota(jnp.int32, (TOK_TILE, RET_DK), 0) % C).astype(f32)
    i = lax.broadcasted_iota(jnp.int32, (C, C), 0).astype(f32)
    j = lax.broadcasted_iota(jnp.int32, (C, C), 1).astype(f32)
    diff = i - j
    for h in range(RET_HEADS):
        qdec_ref[:, _cols(0, h, RET_DK)] = jnp.exp(LOG_GAMMA[h] * (pos + 1.0))
        kdec_ref[:, _cols(0, h, RET_DK)] = jnp.exp(LOG_GAMMA[h] * (C - 1.0 - pos)) * K_SCALE
        dmat_ref[h] = jnp.where(diff >= 0, jnp.exp(LOG_GAMMA[h] * jnp.maximum(diff, 0.0)), 0.0) * K_SCALE
    row = lax.broadcasted_iota(jnp.int32, (SWA_REP * C, 2 * C), 0)
    kpos = lax.broadcasted_iota(jnp.int32, (SWA_REP * C, 2 * C), 1)
    rep = row // C
    dist = (row - rep * C + C) - kpos
    valid = (dist >= 0) & (dist < C)
    distf = dist.astype(f32)
    for g in range(SWA_GROUPS):
        slope = jnp.zeros((SWA_REP * C, 2 * C), f32)
        sink = jnp.zeros((SWA_REP * C, 2 * C), f32)
        for r in range(SWA_REP):
            slope = jnp.where(rep == r, ALIBI_SLOPES[g * SWA_REP + r], slope)
            sink = jnp.where(rep == r, sink_ref[g * SWA_REP + r], sink)
        alibi = -slope * distf
        bias_ref[g] = jnp.where(kpos == 0, sink, jnp.where(valid, alibi, -jnp.inf)) * LOG2E
        bias_ref[SWA_GROUPS + g] = jnp.where(
            kpos == 0, sink, jnp.where(valid & (kpos >= C), alibi, -jnp.inf)) * LOG2E


def _projection_pieces(x_ref, g_ref, w_ref, qdec_ref, kdec_ref, mix_ref, kdt_ref):
    C = CHUNK
    shared = {}

    def prepare():
        x = x_ref[...]
        shared["h"] = (x * _rms_scale(x) * g_ref[...]).astype(bf16)

    def proj(c0, width):
        return jnp.dot(shared["h"], w_ref[:, c0:c0 + width], preferred_element_type=f32)

    width = PROJ_PIECE_COLS

    def swish_gate(part):
        def run():
            g = proj(W_GR + part * width, width)
            mix_ref[:, _cols(M_SG, part, width)] = (g * _sigmoid(g)).astype(bf16)
        return run

    def queries(part):
        def run():
            q = proj(W_QR + part * width, width)
            mix_ref[:, _cols(M_QR, part, width)] = q.astype(bf16)
            mix_ref[:, _cols(M_QD, part, width)] = (q * qdec_ref[:, _cols(0, part, width)]).astype(bf16)
        return run

    def keys(part):
        def run():
            k = proj(W_KR + part * width, width)
            mix_ref[:, _cols(M_KR, part, width)] = k.astype(bf16)
            kd = k * kdec_ref[:, _cols(0, part, width)]
            heads_per_piece = width // RET_DK
            for ci in range(CHUNKS_PER_TILE):
                for hd in range(heads_per_piece):
                    kdt_ref[ci, _cols(0, part * heads_per_piece + hd, RET_DK), :] = (
                        kd[ci * C:(ci + 1) * C, _cols(0, hd, RET_DK)].T.astype(bf16))
        return run

    def attention_queries(part):
        def run():
            mix_ref[:, _cols(M_QS, part, width)] = proj(W_QS + part * width, width).astype(bf16)
        return run

    def attention_keys_values():
        kv_s = proj(W_KS, 2 * SWA_KV)
        ks = kv_s[:, 0:SWA_KV] * (S_SCALE * LOG2E)
        lane = lax.broadcasted_iota(jnp.int32, (TOK_TILE, SWA_KV), 1)
        for grp in range(SWA_GROUPS):
            in_group = (lane >= grp * SWA_DH) & (lane < (grp + 1) * SWA_DH)
            mix_ref[:, _cols(M_KG, grp, SWA_KV)] = jnp.where(in_group, ks, 0.0).astype(bf16)
        mix_ref[:, M_VS:M_VS + SWA_KV] = kv_s[:, SWA_KV:2 * SWA_KV].astype(bf16)

    def values(part):
        def run():
            mix_ref[:, _cols(M_VR, part, width)] = proj(W_VR + part * width, width).astype(bf16)
        return run

    light = [queries(n) for n in range(RET_QK // width)]
    light += [keys(n) for n in range(RET_QK // width)]
    light += [attention_queries(n) for n in range(SWA_Q // width)]
    light += [attention_keys_values]
    light += [values(n) for n in range(RET_V // width)]
    heavy = [swish_gate(n) for n in range(RET_V // width)]
    pieces = []
    per_heavy = -(-len(light) // len(heavy))
    for n, piece in enumerate(heavy):
        pieces.append(piece)
        pieces += light[n * per_heavy:(n + 1) * per_heavy]
    return prepare, pieces


def _mixer_phases(mix_ref, kdt_ref, first_tile_of_seq, ret_ref, swa_ref,
                  state_ref, kprev_ref, vprev_ref, dmat_ref, bias_ref):
    C = CHUNK
    first_group = lax.broadcasted_iota(jnp.int32, (C, SWA_KV), 1) < SWA_DH
    ones_cols = jnp.ones((2 * C, SWA_KV), bf16)
    units = [(j, g) for j in range(SWA_REP) for g in range(SWA_GROUPS)]
    live = [dict() for _ in range(CHUNKS_PER_TILE)]

    def rows(t):
        return slice(t * C, (t + 1) * C)

    def zero_first_row(block):
        rows_per_vreg = 16
        top = block[0:rows_per_vreg, :]
        is_first = lax.broadcasted_iota(jnp.int32, top.shape, 0) == 0
        return jnp.concatenate([jnp.where(is_first, jnp.zeros_like(top), top), block[rows_per_vreg:, :]], axis=0)

    def first_matmuls(t):
        st = live[t]
        st["qk"] = [lax.dot_general(mix_ref[rows(t), _cols(M_QR, h, RET_DK)], mix_ref[rows(t), _cols(M_KR, h, RET_DK)],
                                    (((1,), (1,)), ((), ())), preferred_element_type=f32)
                    for h in range(RET_HEADS)]
        st["kv"] = [jnp.dot(kdt_ref[t, _cols(0, h, RET_DK), :], mix_ref[rows(t), _cols(M_VR, h, RET_DV)],
                            preferred_element_type=f32) for h in range(RET_HEADS)]
        if t == 0:
            keys = [jnp.concatenate([kprev_ref[g], mix_ref[0:C, _cols(M_KG, g, SWA_KV)]], axis=0)
                    for g in range(SWA_GROUPS)]
            vals = jnp.concatenate([vprev_ref[...], mix_ref[0:C, M_VS:M_VS + SWA_KV]], axis=0)
            table0 = jnp.where(first_tile_of_seq, SWA_GROUPS, 0)
        else:
            prev_and_current = slice((t - 1) * C, (t + 1) * C)
            keys = [mix_ref[prev_and_current, _cols(M_KG, g, SWA_KV)] for g in range(SWA_GROUPS)]
            vals = mix_ref[prev_and_current, M_VS:M_VS + SWA_KV]
            table0 = 0
        keys = [zero_first_row(k) for k in keys]
        st["vv_ones"] = jnp.concatenate([zero_first_row(vals), ones_cols], axis=1)
        st["s"] = {}
        for j, g in units:
            s = lax.dot_general(mix_ref[rows(t), _cols(M_QS, j, C)], keys[g], (((1,), (1,)), ((), ())),
                                preferred_element_type=f32)
            st["s"][j, g] = s + bias_ref[table0 + g, _cols(0, j, C), :]

    def retention_vector_work(t):
        st = live[t]
        st["a"], st["state_bf16"] = [], []
        for h in range(RET_HEADS):
            s_prev = state_ref[h]
            st["a"].append((st["qk"][h] * dmat_ref[h]).astype(bf16))
            st["state_bf16"].append(s_prev.astype(bf16))
            state_ref[h] = float(np.exp(LOG_GAMMA[h] * C)) * s_prev + st["kv"][h]
        del st["qk"], st["kv"]

    def softmax_numerators(t):
        st = live[t]
        st["p"] = {}
        for u in units:
            s = st["s"][u]
            st["p"][u] = jnp.exp2(s - jnp.max(s, axis=-1, keepdims=True)).astype(bf16)
        del st["s"]

    def second_matmuls(t):
        st = live[t]
        st["o"] = [jnp.dot(jnp.concatenate([st["a"][h], mix_ref[rows(t), _cols(M_QD, h, RET_DK)]], axis=1),
                           jnp.concatenate([mix_ref[rows(t), _cols(M_VR, h, RET_DV)], st["state_bf16"][h]], axis=0),
                           preferred_element_type=f32) for h in range(RET_HEADS)]
        st["r"] = {u: jnp.dot(st["p"][u], st["vv_ones"], preferred_element_type=f32) for u in units}
        del st["a"], st["state_bf16"], st["p"], st["vv_ones"]

    def retention_output(t):
        st = live[t]
        for h in range(RET_HEADS):
            o = st["o"][h]
            mu = jnp.mean(o, axis=-1, keepdims=True)
            d = o - mu
            var = jnp.mean(d * d, axis=-1, keepdims=True)
            ret_ref[rows(t), _cols(0, h, RET_DV)] = (
                mix_ref[rows(t), _cols(M_SG, h, RET_DV)].astype(f32) * (d * lax.rsqrt(var + EPS))).astype(bf16)
        del st["o"]

    def attention_output(t):
        st = live[t]
        for j in range(SWA_REP):
            num = [st["r"][j, g][:, 0:SWA_KV] for g in range(SWA_GROUPS)]
            den = [st["r"][j, g][:, SWA_KV:2 * SWA_KV] for g in range(SWA_GROUPS)]
            o_pair = jnp.where(first_group, num[0], num[1]) / jnp.where(first_group, den[0], den[1])
            swa_ref[rows(t), _cols(0, j, C)] = o_pair.astype(bf16)
        st.clear()

    def carry_last_block():
        last = slice(TOK_TILE - C, TOK_TILE)
        for g in range(SWA_GROUPS):
            kprev_ref[g] = mix_ref[last, _cols(M_KG, g, SWA_KV)]
        vprev_ref[...] = mix_ref[last, M_VS:M_VS + SWA_KV]

    phases = []
    for t in range(CHUNKS_PER_TILE + 1):
        if t < CHUNKS_PER_TILE:
            phases.append(functools.partial(first_matmuls, t))
        if t >= 1:
            phases += [functools.partial(fn, t - 1) for fn in (
                retention_vector_work, softmax_numerators, second_matmuls, retention_output, attention_output)]
    phases.append(carry_last_block)
    return phases


def _store_attention_columns(dst_ref, chunk):
    lanes = 2 * SWA_DH
    first_half = lax.broadcasted_iota(jnp.int32, (chunk.shape[0], lanes), 1) < SWA_DH
    for j in range(SWA_REP):
        a = chunk[:, _cols(0, j // 2, lanes)]
        b = chunk[:, _cols(0, (SWA_REP + j) // 2, lanes)]
        if j % 2 == 0:
            pair = jnp.where(first_half, a, pltpu.roll(b, SWA_DH, 1))
        else:
            pair = jnp.where(first_half, pltpu.roll(a, SWA_DH, 1), b)
        dst_ref[:, _cols(0, j, lanes)] = pair.astype(bf16)
    rest = slice(SWA_Q, SWA_Q + 2 * SWA_KV)
    dst_ref[:, rest] = chunk[:, rest].astype(bf16)


def _load_mixer_weights(w_in_hbm, w_ref, stage_ref, sem_ref):
    jobs = _plain_weight_jobs(w_in_hbm, w_ref.at[:, 0:W_QS])
    for r0 in range(0, D_MODEL, STAGE_ROWS):
        rows = slice(r0, r0 + STAGE_ROWS)
        piece = (w_in_hbm.at[rows, W_QS:D_MIXW], slice(0, STAGE_ROWS), slice(0, D_MIXW - W_QS))
        jobs.append(([piece], functools.partial(_store_attention_columns, w_ref.at[rows, W_QS:D_MIXW])))
    _stream_weight_chunks(jobs, stage_ref, sem_ref)


def _proj_mixer_kernel(x_ref, g_ref, w_in_hbm, sink_ref, ret_ref, swa_ref,
                       w_ref, stage_ref, sem_ref,
                       mix_even_ref, mix_odd_ref, kdt_even_ref, kdt_odd_ref, qdec_ref, kdec_ref,
                       state_ref, kprev_ref, vprev_ref, dmat_ref, bias_ref, *, tiles_per_seq):
    s = pl.program_id(0)
    first_tile_of_seq = lax.rem(s + tiles_per_seq - 1, tiles_per_seq) == 0

    @pl.when(s == 0)
    def _first_step():
        _load_mixer_weights(w_in_hbm, w_ref, stage_ref, sem_ref)
        _init_tables(sink_ref, qdec_ref, kdec_ref, dmat_ref, bias_ref)
        mix_odd_ref[...] = jnp.zeros_like(mix_odd_ref)
        kdt_odd_ref[...] = jnp.zeros_like(kdt_odd_ref)

    @pl.when((s == 0) | first_tile_of_seq)
    def _reset_sequence_state():
        state_ref[...] = jnp.zeros_like(state_ref)
        kprev_ref[...] = jnp.zeros_like(kprev_ref)
        vprev_ref[...] = jnp.zeros_like(vprev_ref)

    def step(write_mix_ref, write_kdt_ref, read_mix_ref, read_kdt_ref):
        prepare, pieces = _projection_pieces(x_ref, g_ref, w_ref, qdec_ref, kdec_ref, write_mix_ref, write_kdt_ref)
        phases = _mixer_phases(read_mix_ref, read_kdt_ref, first_tile_of_seq, ret_ref, swa_ref,
                               state_ref, kprev_ref, vprev_ref, dmat_ref, bias_ref)
        prepare()
        phases[0]()
        emitted = 0
        for n, phase in enumerate(phases[1:], start=1):
            while emitted < len(pieces) and emitted * (len(phases) - 1) < n * len(pieces):
                pieces[emitted]()
                emitted += 1
            phase()

    @pl.when(lax.rem(s, 2) == 0)
    def _even_step():
        step(mix_even_ref, kdt_even_ref, mix_odd_ref, kdt_odd_ref)

    @pl.when(lax.rem(s, 2) == 1)
    def _odd_step():
        step(mix_odd_ref, kdt_odd_ref, mix_even_ref, kdt_even_ref)


def _proj_mixer(x2d, gain, w_in, sinks, seq):
    n = x2d.shape[0]
    nt = n // TOK_TILE
    C = CHUNK
    const = lambda s: (0, 0)
    projected_tile = lambda s: (jnp.minimum(s, nt - 1), 0)
    mixed_tile = lambda s: (jnp.maximum(s - 1, 0), 0)
    return pl.pallas_call(
        functools.partial(_proj_mixer_kernel, tiles_per_seq=seq // TOK_TILE),
        out_shape=(jax.ShapeDtypeStruct((n, RET_V), bf16), jax.ShapeDtypeStruct((n, SWA_Q), bf16)),
        grid=(nt + 1,),
        in_specs=[
            pl.BlockSpec((TOK_TILE, D_MODEL), projected_tile),
            pl.BlockSpec((1, D_MODEL), const),
            pl.BlockSpec(memory_space=pl.ANY),
            pl.BlockSpec(memory_space=pltpu.SMEM),
        ],
        out_specs=(
            pl.BlockSpec((TOK_TILE, RET_V), mixed_tile),
            pl.BlockSpec((TOK_TILE, SWA_Q), mixed_tile),
        ),
        scratch_shapes=[
            pltpu.VMEM((D_MODEL, D_MIXW), bf16),
            pltpu.VMEM((2, STAGE_ROWS, STAGE_COLS), f32),
            pltpu.SemaphoreType.DMA((2,)),
            pltpu.VMEM((TOK_TILE, D_MIXIN), bf16),
            pltpu.VMEM((TOK_TILE, D_MIXIN), bf16),
            pltpu.VMEM((CHUNKS_PER_TILE, RET_QK, C), bf16),
            pltpu.VMEM((CHUNKS_PER_TILE, RET_QK, C), bf16),
            pltpu.VMEM((TOK_TILE, RET_QK), f32),
            pltpu.VMEM((TOK_TILE, RET_QK), f32),
            pltpu.VMEM((RET_HEADS, RET_DK, RET_DV), f32),
            pltpu.VMEM((SWA_GROUPS, C, SWA_KV), bf16),
            pltpu.VMEM((C, SWA_KV), bf16),
            pltpu.VMEM((RET_HEADS, C, C), f32),
            pltpu.VMEM((2 * SWA_GROUPS, SWA_REP * C, 2 * C), f32),
        ],
        compiler_params=pltpu.CompilerParams(
            dimension_semantics=("arbitrary",), vmem_limit_bytes=VMEM_LIMIT_BYTES),
        name="proj_mixer",
    )(x2d, gain, w_in, sinks)


def _load_post_weights(w_in_hbm, w_ro_hbm, w_so_hbm, w_o_hbm, w_up_hbm, w_dn_hbm,
                       wg_ref, wro_ref, wso_ref, wo_ref, wup_ref, wdn_ref, stage_ref, sem_ref):
    jobs = _plain_weight_jobs(w_in_hbm, wg_ref, col0=D_MIXW)
    jobs += _plain_weight_jobs(w_ro_hbm, wro_ref)
    runs = _swa_out_row_order().reshape(-1, SWA_DH)[:, 0]
    pieces = [(w_so_hbm.at[int(r0):int(r0) + SWA_DH, :], slice(n * SWA_DH, (n + 1) * SWA_DH), slice(0, STAGE_COLS))
              for n, r0 in enumerate(runs)]
    jobs += [(pieces, functools.partial(_store_bf16, wso_ref))]
    jobs += _plain_weight_jobs(w_o_hbm, wo_ref)
    jobs += _plain_weight_jobs(w_up_hbm, wup_ref)
    jobs += _plain_weight_jobs(w_dn_hbm, wdn_ref)
    _stream_weight_chunks(jobs, stage_ref, sem_ref)


def _post_kernel(x_ref, ret_ref, swa_ref, gpre_mix_ref, w_in_hbm, w_ro_hbm, w_so_hbm, w_o_hbm, gpost_mix_ref,
                 gpre_mlp_ref, w_up_hbm, w_dn_hbm, gpost_mlp_ref, o_ref,
                 wg_ref, wro_ref, wso_ref, wo_ref, wup_ref, wdn_ref, stage_ref, sem_ref, *, sub_rows, ff_chunk):
    tm = x_ref.shape[0]
    ff_starts = list(range(0, D_FF, ff_chunk))

    @pl.when(pl.program_id(0) == 0)
    def _first_step():
        _load_post_weights(w_in_hbm, w_ro_hbm, w_so_hbm, w_o_hbm, w_up_hbm, w_dn_hbm,
                           wg_ref, wro_ref, wso_ref, wo_ref, wup_ref, wdn_ref, stage_ref, sem_ref)

    def row_block(rows):
        x = x_ref[rows, :]
        y_r = jnp.dot(ret_ref[rows, :], wro_ref[...], preferred_element_type=f32)
        y_s = jnp.dot(swa_ref[rows, :], wso_ref[...], preferred_element_type=f32)
        h = (x * _rms_scale(x) * gpre_mix_ref[...]).astype(bf16)
        gate_r = jnp.dot(h, wg_ref[:, 0:D_MODEL], preferred_element_type=f32)
        gate_s = jnp.dot(h, wg_ref[:, D_MODEL:D_GATES], preferred_element_type=f32)
        yield
        merged = _sigmoid(gate_r) * y_r + _sigmoid(gate_s) * y_s
        z = jnp.dot(merged.astype(bf16), wo_ref[...], preferred_element_type=f32)
        yield
        x1 = x + z * _rms_scale(z) * gpost_mix_ref[...]
        h1 = (x1 * _rms_scale(x1) * gpre_mlp_ref[...]).astype(bf16)

        def up(c0):
            return jnp.dot(h1, wup_ref[:, c0:c0 + ff_chunk], preferred_element_type=f32)

        acc = None
        u_next = up(ff_starts[0])
        yield
        for n, c0 in enumerate(ff_starts):
            u = u_next
            if n + 1 < len(ff_starts):
                u_next = up(ff_starts[n + 1])
            u = jnp.maximum(u, 0.0)
            part = jnp.dot((u * u).astype(bf16), wdn_ref[c0:c0 + ff_chunk, :], preferred_element_type=f32)
            acc = part if acc is None else acc + part
            yield
        o_ref[rows, :] = x1 + acc * _rms_scale(acc) * gpost_mlp_ref[...]

    active = [row_block(slice(r0, r0 + sub_rows)) for r0 in range(0, tm, sub_rows)]
    while active:
        for gen in list(active):
            if next(gen, "done") == "done":
                active.remove(gen)


def _post(x2d, ret, swa, gpre_mix, w_in, w_ro, w_so, w_o, gpost_mix, gpre_mlp, w_up, w_dn, gpost_mlp, *,
          tm=POST_TILE, sub_rows=256, ff_chunk=1024):
    n = x2d.shape[0]
    const = lambda i: (0, 0)
    tile = lambda width: pl.BlockSpec((tm, width), lambda i: (i, 0))
    in_hbm = pl.BlockSpec(memory_space=pl.ANY)
    gain = pl.BlockSpec((1, D_MODEL), const)
    return pl.pallas_call(
        functools.partial(_post_kernel, sub_rows=sub_rows, ff_chunk=ff_chunk),
        out_shape=jax.ShapeDtypeStruct((n, D_MODEL), f32),
        grid=(n // tm,),
        in_specs=[
            tile(D_MODEL), tile(RET_V), tile(SWA_Q),
            gain, in_hbm,
            in_hbm, in_hbm, in_hbm, gain,
            gain, in_hbm, in_hbm, gain,
        ],
        out_specs=tile(D_MODEL),
        scratch_shapes=[
            pltpu.VMEM((D_MODEL, D_GATES), bf16),
            pltpu.VMEM((RET_V, D_MODEL), bf16),
            pltpu.VMEM((SWA_Q, D_MODEL), bf16),
            pltpu.VMEM((D_MODEL, D_MODEL), bf16),
            pltpu.VMEM((D_MODEL, D_FF), bf16),
            pltpu.VMEM((D_FF, D_MODEL), bf16),
            pltpu.VMEM((2, STAGE_ROWS, STAGE_COLS), f32),
            pltpu.SemaphoreType.DMA((2,)),
        ],
        compiler_params=pltpu.CompilerParams(
            dimension_semantics=("arbitrary",), vmem_limit_bytes=VMEM_LIMIT_BYTES),
        name="post",
    )(x2d, ret, swa, gpre_mix, w_in, w_ro, w_so, w_o, gpost_mix, gpre_mlp, w_up, w_dn, gpost_mlp)


def kernel(x, pre_mix_norm, w_in, w_ret_out, w_swa_out, w_out, sinks, post_mix_norm, pre_mlp_norm, w_up,
           w_down, post_mlp_norm):
    batch, seq, d = x.shape
    depth = w_in.shape[0]
    assert d == D_MODEL and seq % TOK_TILE == 0 and seq // TOK_TILE > 1
    x2d = x.reshape(batch * seq, d)
    for l in range(depth):
        pre_mix_gain = pre_mix_norm[l][None, :]
        ret, swa = _proj_mixer(x2d, pre_mix_gain, w_in[l], sinks[l].astype(f32), seq)
        x2d = _post(x2d, ret, swa, pre_mix_gain, w_in[l], w_ret_out[l], w_swa_out[l], w_out[l],
                    post_mix_norm[l][None, :], pre_mlp_norm[l][None, :], w_up[l], w_down[l],
                    post_mlp_norm[l][None, :])
    return x2d.reshape(batch, seq, d)
```

```python
import functools

import numpy as np
import jax
import jax.numpy as jnp
from jax import lax
from jax.experimental import pallas as pl
from jax.experimental.pallas import tpu as pltpu

D_MODEL = 1024
RET_HEADS = 4
RET_DK = 128
RET_DV = 256
CHUNK = 128
SWA_HEADS = 8
SWA_GROUPS = 2
SWA_REP = SWA_HEADS // SWA_GROUPS
SWA_DH = 64
D_FF = 4 * D_MODEL
EPS = 1e-6

RET_QK = RET_HEADS * RET_DK
RET_V = RET_HEADS * RET_DV
SWA_Q = SWA_HEADS * SWA_DH
SWA_KV = SWA_GROUPS * SWA_DH

W_QR = 0
W_KR = W_QR + RET_QK
W_VR = W_KR + RET_QK
W_GR = W_VR + RET_V
W_QS = W_GR + RET_V
W_KS = W_QS + SWA_Q
W_VS = W_KS + SWA_KV
D_MIXW = W_VS + SWA_KV
D_GATES = 2 * D_MODEL

M_QR = 0
M_KR = M_QR + RET_QK
M_VR = M_KR + RET_QK
M_SG = M_VR + RET_V
M_QD = M_SG + RET_V
M_QS = M_QD + RET_QK
M_KG = M_QS + SWA_Q
M_VS = M_KG + SWA_GROUPS * SWA_KV
D_MIXIN = M_VS + SWA_KV

X_RET = 0
X_SWA = X_RET + RET_V
D_MIXED = X_SWA + SWA_Q

TOK_TILE = 512
CHUNKS_PER_TILE = TOK_TILE // CHUNK
PROJ_PIECE_COLS = 256
POST_TILE = 512
STAGE_ROWS, STAGE_COLS = 512, 1024
STAGE_SLOTS = 3
BF16_ROWS = 16

VMEM_LIMIT_BYTES = 56 * 1024 * 1024

LOG_GAMMA = [float(np.log(1.0 - 2.0 ** (-5.0 - h))) for h in range(RET_HEADS)]
ALIBI_SLOPES = [float(2.0 ** (-8.0 / SWA_HEADS * (h + 1))) for h in range(SWA_HEADS)]
K_SCALE = float(RET_DK ** -0.5)
S_SCALE = float(SWA_DH ** -0.5)
LOG2E = float(np.log2(np.e))

f32 = jnp.float32
bf16 = jnp.bfloat16


def _rms_scale(v):
    return lax.rsqrt(jnp.mean(v * v, axis=-1, keepdims=True) + EPS)


def _sigmoid(v):
    return 0.5 * jnp.tanh(0.5 * v) + 0.5


def _cols(offset, index, width):
    return slice(offset + index * width, offset + (index + 1) * width)


def _store_bf16(dst_ref, chunk):
    dst_ref[...] = chunk[0:dst_ref.shape[0], 0:dst_ref.shape[1]].astype(bf16)


def _plain_weight_jobs(src_hbm, dst_ref, col0=0):
    n_rows, n_cols = dst_ref.shape
    jobs = []
    for r0 in range(0, n_rows, STAGE_ROWS):
        for c0 in range(0, n_cols, STAGE_COLS):
            rows, cols = min(STAGE_ROWS, n_rows - r0), min(STAGE_COLS, n_cols - c0)
            piece = (src_hbm.at[r0:r0 + rows, col0 + c0:col0 + c0 + cols], slice(0, rows), slice(0, cols))
            jobs.append(([piece], functools.partial(_store_bf16, dst_ref.at[r0:r0 + rows, c0:c0 + cols])))
    return jobs


def _stream_weight_chunks(jobs, stage_ref, sem_ref):
    slots = stage_ref.shape[0]

    def copies(k):
        slot = k % slots
        return [pltpu.make_async_copy(src, stage_ref.at[slot, rows, cols], sem_ref.at[slot])
                for src, rows, cols in jobs[k][0]]

    for k in range(min(slots - 1, len(jobs))):
        for copy in copies(k):
            copy.start()
    for k, job in enumerate(jobs):
        ahead = k + slots - 1
        if ahead < len(jobs):
            for copy in copies(ahead):
                copy.start()
        for copy in copies(k):
            copy.wait()
        job[1](stage_ref[k % slots])


def _init_tables(sink_ref, qdec_ref, kdec_ref, dmat_ref, bias_ref):
    C = CHUNK
    pos = (lax.broadcasted_iota(jnp.int32, (TOK_TILE, RET_DK), 0) % C).astype(f32)
    i = lax.broadcasted_iota(jnp.int32, (C, C), 0).astype(f32)
    j = lax.broadcasted_iota(jnp.int32, (C, C), 1).astype(f32)
    diff = i - j
    for h in range(RET_HEADS):
        qdec_ref[:, _cols(0, h, RET_DK)] = jnp.exp(LOG_GAMMA[h] * (pos + 1.0))
        kdec_ref[:, _cols(0, h, RET_DK)] = jnp.exp(LOG_GAMMA[h] * (C - 1.0 - pos)) * K_SCALE
        dmat_ref[h] = jnp.where(diff >= 0, jnp.exp(LOG_GAMMA[h] * jnp.maximum(diff, 0.0)), 0.0) * K_SCALE
    row = lax.broadcasted_iota(jnp.int32, (SWA_REP * C, 2 * C), 0)
    kpos = lax.broadcasted_iota(jnp.int32, (SWA_REP * C, 2 * C), 1)
    rep = row // C
    dist = (row - rep * C + C) - kpos
    valid = (dist >= 0) & (dist < C)
    distf = dist.astype(f32)
    for g in range(SWA_GROUPS):
        slope = jnp.zeros((SWA_REP * C, 2 * C), f32)
        sink = jnp.zeros((SWA_REP * C, 2 * C), f32)
        for r in range(SWA_REP):
            slope = jnp.where(rep == r, ALIBI_SLOPES[g * SWA_REP + r], slope)
            sink = jnp.where(rep == r, sink_ref[g * SWA_REP + r], sink)
        alibi = -slope * distf
        bias_ref[g] = jnp.where(kpos == 0, sink, jnp.where(valid, alibi, -jnp.inf)) * LOG2E
        bias_ref[SWA_GROUPS + g] = jnp.where(
            kpos == 0, sink, jnp.where(valid & (kpos >= C), alibi, -jnp.inf)) * LOG2E


def _projection_pieces(x_ref, g_ref, w_ref, qdec_ref, kdec_ref, mix_ref, kdt_ref):
    C = CHUNK
    shared = {}

    def prepare():
        x = x_ref[...]
        shared["h"] = (x * _rms_scale(x) * g_ref[...]).astype(bf16)

    def proj(c0, width):
        return jnp.dot(shared["h"], w_ref[:, c0:c0 + width], preferred_element_type=f32)

    width = PROJ_PIECE_COLS

    def swish_gate(part):
        def run():
            g = proj(W_GR + part * width, width)
            mix_ref[:, _cols(M_SG, part, width)] = (g * _sigmoid(g)).astype(bf16)
        return run

    def queries(part):
        def run():
            q = proj(W_QR + part * width, width)
            mix_ref[:, _cols(M_QR, part, width)] = q.astype(bf16)
            mix_ref[:, _cols(M_QD, part, width)] = (q * qdec_ref[:, _cols(0, part, width)]).astype(bf16)
        return run

    def keys(part):
        def run():
            k = proj(W_KR + part * width, width)
            mix_ref[:, _cols(M_KR, part, width)] = k.astype(bf16)
            kd = k * kdec_ref[:, _cols(0, part, width)]
            heads_per_piece = width // RET_DK
            for ci in range(CHUNKS_PER_TILE):
                for hd in range(heads_per_piece):
                    kdt_ref[ci, _cols(0, part * heads_per_piece + hd, RET_DK), :] = (
                        kd[ci * C:(ci + 1) * C, _cols(0, hd, RET_DK)].T.astype(bf16))
        return run

    def attention_queries(part):
        def run():
            mix_ref[:, _cols(M_QS, part, width)] = proj(W_QS + part * width, width).astype(bf16)
        return run

    def attention_keys_values():
        kv_s = proj(W_KS, 2 * SWA_KV)
        ks = kv_s[:, 0:SWA_KV] * (S_SCALE * LOG2E)
        lane = lax.broadcasted_iota(jnp.int32, (TOK_TILE, SWA_KV), 1)
        for grp in range(SWA_GROUPS):
            in_group = (lane >= grp * SWA_DH) & (lane < (grp + 1) * SWA_DH)
            mix_ref[:, _cols(M_KG, grp, SWA_KV)] = jnp.where(in_group, ks, 0.0).astype(bf16)
        mix_ref[:, M_VS:M_VS + SWA_KV] = kv_s[:, SWA_KV:2 * SWA_KV].astype(bf16)

    def values(part):
        def run():
            mix_ref[:, _cols(M_VR, part, width)] = proj(W_VR + part * width, width).astype(bf16)
        return run

    light = [queries(n) for n in range(RET_QK // width)]
    light += [keys(n) for n in range(RET_QK // width)]
    light += [attention_queries(n) for n in range(SWA_Q // width)]
    light += [attention_keys_values]
    light += [values(n) for n in range(RET_V // width)]
    heavy = [swish_gate(n) for n in range(RET_V // width)]
    pieces = []
    per_heavy = -(-len(light) // len(heavy))
    for n, piece in enumerate(heavy):
        pieces.append(piece)
        pieces += light[n * per_heavy:(n + 1) * per_heavy]
    return prepare, pieces


def _mixer_phases(mix_ref, kdt_ref, first_tile_of_seq, mixed_ref,
                  state_ref, kprev_ref, vprev_ref, dmat_ref, bias_ref):
    C = CHUNK
    first_group = lax.broadcasted_iota(jnp.int32, (C, SWA_KV), 1) < SWA_DH
    ones_cols = jnp.ones((2 * C, SWA_KV), bf16)
    units = [(j, g) for j in range(SWA_REP) for g in range(SWA_GROUPS)]
    live = [dict() for _ in range(CHUNKS_PER_TILE)]

    def rows(t):
        return slice(t * C, (t + 1) * C)

    def zero_first_row(block):
        rows_per_vreg = 16
        top = block[0:rows_per_vreg, :]
        is_first = lax.broadcasted_iota(jnp.int32, top.shape, 0) == 0
        return jnp.concatenate([jnp.where(is_first, jnp.zeros_like(top), top), block[rows_per_vreg:, :]], axis=0)

    def first_matmuls(t):
        st = live[t]
        st["qk"] = [lax.dot_general(mix_ref[rows(t), _cols(M_QR, h, RET_DK)], mix_ref[rows(t), _cols(M_KR, h, RET_DK)],
                                    (((1,), (1,)), ((), ())), preferred_element_type=f32)
                    for h in range(RET_HEADS)]
        st["kv"] = [jnp.dot(kdt_ref[t, _cols(0, h, RET_DK), :], mix_ref[rows(t), _cols(M_VR, h, RET_DV)],
                            preferred_element_type=f32) for h in range(RET_HEADS)]
        if t == 0:
            keys = [jnp.concatenate([kprev_ref[g], mix_ref[0:C, _cols(M_KG, g, SWA_KV)]], axis=0)
                    for g in range(SWA_GROUPS)]
            vals = jnp.concatenate([vprev_ref[...], mix_ref[0:C, M_VS:M_VS + SWA_KV]], axis=0)
            table0 = jnp.where(first_tile_of_seq, SWA_GROUPS, 0)
        else:
            prev_and_current = slice((t - 1) * C, (t + 1) * C)
            keys = [mix_ref[prev_and_current, _cols(M_KG, g, SWA_KV)] for g in range(SWA_GROUPS)]
            vals = mix_ref[prev_and_current, M_VS:M_VS + SWA_KV]
            table0 = 0
        keys = [zero_first_row(k) for k in keys]
        st["vv_ones"] = jnp.concatenate([zero_first_row(vals), ones_cols], axis=1)
        st["s"] = {}
        for j, g in units:
            s = lax.dot_general(mix_ref[rows(t), _cols(M_QS, j, C)], keys[g], (((1,), (1,)), ((), ())),
                                preferred_element_type=f32)
            st["s"][j, g] = s + bias_ref[table0 + g, _cols(0, j, C), :]

    def retention_vector_work(t):
        st = live[t]
        st["a"], st["state_bf16"] = [], []
        for h in range(RET_HEADS):
            s_prev = state_ref[h]
            st["a"].append((st["qk"][h] * dmat_ref[h]).astype(bf16))
            st["state_bf16"].append(s_prev.astype(bf16))
            state_ref[h] = float(np.exp(LOG_GAMMA[h] * C)) * s_prev + st["kv"][h]
        del st["qk"], st["kv"]

    def softmax_numerators(t):
        st = live[t]
        st["p"] = {}
        for u in units:
            s = st["s"][u]
            st["p"][u] = jnp.exp2(s - jnp.max(s, axis=-1, keepdims=True)).astype(bf16)
        del st["s"]

    def second_matmuls(t):
        st = live[t]
        st["o"] = [jnp.dot(jnp.concatenate([st["a"][h], mix_ref[rows(t), _cols(M_QD, h, RET_DK)]], axis=1),
                           jnp.concatenate([mix_ref[rows(t), _cols(M_VR, h, RET_DV)], st["state_bf16"][h]], axis=0),
                           preferred_element_type=f32) for h in range(RET_HEADS)]
        st["r"] = {u: jnp.dot(st["p"][u], st["vv_ones"], preferred_element_type=f32) for u in units}
        del st["a"], st["state_bf16"], st["p"], st["vv_ones"]

    def retention_output(t):
        st = live[t]
        for h in range(RET_HEADS):
            o = st["o"][h]
            mu = jnp.mean(o, axis=-1, keepdims=True)
            d = o - mu
            var = jnp.mean(d * d, axis=-1, keepdims=True)
            mixed_ref[rows(t), _cols(X_RET, h, RET_DV)] = (
                mix_ref[rows(t), _cols(M_SG, h, RET_DV)].astype(f32) * (d * lax.rsqrt(var + EPS))).astype(bf16)
        del st["o"]

    def attention_output(t):
        st = live[t]
        for j in range(SWA_REP):
            num = [st["r"][j, g][:, 0:SWA_KV] for g in range(SWA_GROUPS)]
            den = [st["r"][j, g][:, SWA_KV:2 * SWA_KV] for g in range(SWA_GROUPS)]
            o_pair = jnp.where(first_group, num[0], num[1]) / jnp.where(first_group, den[0], den[1])
            mixed_ref[rows(t), _cols(X_SWA, j, C)] = o_pair.astype(bf16)
        st.clear()

    def carry_last_block():
        last = slice(TOK_TILE - C, TOK_TILE)
        for g in range(SWA_GROUPS):
            kprev_ref[g] = mix_ref[last, _cols(M_KG, g, SWA_KV)]
        vprev_ref[...] = mix_ref[last, M_VS:M_VS + SWA_KV]

    phases = []
    for t in range(CHUNKS_PER_TILE + 1):
        if t < CHUNKS_PER_TILE:
            phases.append(functools.partial(first_matmuls, t))
        if t >= 1:
            phases += [functools.partial(fn, t - 1) for fn in (
                retention_vector_work, softmax_numerators, second_matmuls, retention_output, attention_output)]
    phases.append(carry_last_block)
    return phases


def _store_attention_columns(dst_ref, chunk):
    lanes = 2 * SWA_DH
    first_half = lax.broadcasted_iota(jnp.int32, (chunk.shape[0], lanes), 1) < SWA_DH
    for j in range(SWA_REP):
        a = chunk[:, _cols(0, j // 2, lanes)]
        b = chunk[:, _cols(0, (SWA_REP + j) // 2, lanes)]
        if j % 2 == 0:
            pair = jnp.where(first_half, a, pltpu.roll(b, SWA_DH, 1))
        else:
            pair = jnp.where(first_half, pltpu.roll(a, SWA_DH, 1), b)
        dst_ref[:, _cols(0, j, lanes)] = pair.astype(bf16)
    rest = slice(SWA_Q, SWA_Q + 2 * SWA_KV)
    dst_ref[:, rest] = chunk[:, rest].astype(bf16)


def _load_mixer_weights(w_in_hbm, w_ref, stage_ref, sem_ref):
    jobs = _plain_weight_jobs(w_in_hbm, w_ref.at[:, 0:W_QS])
    for r0 in range(0, D_MODEL, STAGE_ROWS):
        rows = slice(r0, r0 + STAGE_ROWS)
        piece = (w_in_hbm.at[rows, W_QS:D_MIXW], slice(0, STAGE_ROWS), slice(0, D_MIXW - W_QS))
        jobs.append(([piece], functools.partial(_store_attention_columns, w_ref.at[rows, W_QS:D_MIXW])))
    _stream_weight_chunks(jobs, stage_ref, sem_ref)


def _proj_mixer_kernel(x_ref, g_ref, w_in_hbm, sink_ref, win_rows, wro_rows, wso_rows, wo_rows, wup_rows, wdn_rows,
                       mixed_ref, wg_out, wro_out, wso_out, wo_out, wup_out, wdn_out,
                       w_ref, stage_ref, sem_ref,
                       mix_even_ref, mix_odd_ref, kdt_even_ref, kdt_odd_ref, qdec_ref, kdec_ref,
                       state_ref, kprev_ref, vprev_ref, dmat_ref, bias_ref, *, tiles_per_seq, num_tiles):
    s = pl.program_id(0)
    first_tile_of_seq = lax.rem(s + tiles_per_seq - 1, tiles_per_seq) == 0

    @pl.when(s == 0)
    def _first_step():
        _load_mixer_weights(w_in_hbm, w_ref, stage_ref, sem_ref)
        _init_tables(sink_ref, qdec_ref, kdec_ref, dmat_ref, bias_ref)

    @pl.when(first_tile_of_seq)
    def _reset_sequence_state():
        state_ref[...] = jnp.zeros_like(state_ref)
        kprev_ref[...] = jnp.zeros_like(kprev_ref)
        vprev_ref[...] = jnp.zeros_like(vprev_ref)

    even = (mix_even_ref, kdt_even_ref)
    odd = (mix_odd_ref, kdt_odd_ref)

    def convert_post_weight_rows():
        wg_out[...] = win_rows[:, D_MIXW:D_MIXW + D_GATES].astype(bf16)
        for src, dst in ((wro_rows, wro_out), (wso_rows, wso_out), (wo_rows, wo_out), (wup_rows, wup_out),
                         (wdn_rows, wdn_out)):
            dst[...] = src[...].astype(bf16)

    def step(write, read):
        convert_post_weight_rows()
        prepare, pieces = (lambda: None), []
        if write is not None:
            prepare, pieces = _projection_pieces(x_ref, g_ref, w_ref, qdec_ref, kdec_ref, *write)
        phases = []
        if read is not None:
            phases = _mixer_phases(*read, first_tile_of_seq, mixed_ref,
                                   state_ref, kprev_ref, vprev_ref, dmat_ref, bias_ref)
        prepare()
        for phase in phases[:1]:
            phase()
        emitted = 0
        for n, phase in enumerate(phases[1:], start=1):
            while emitted < len(pieces) and emitted * (len(phases) - 1) < n * len(pieces):
                pieces[emitted]()
                emitted += 1
            phase()
        for piece in pieces[emitted:]:
            piece()

    last = num_tiles
    last_read = odd if (num_tiles - 1) % 2 else even

    @pl.when(s == 0)
    def _first_tile():
        step(even, None)

    @pl.when((s > 0) & (s < last) & (lax.rem(s, 2) == 0))
    def _even_step():
        step(even, odd)

    @pl.when((s > 0) & (s < last) & (lax.rem(s, 2) == 1))
    def _odd_step():
        step(odd, even)

    @pl.when(s == last)
    def _last_tile():
        step(None, last_read)


def _proj_mixer(x2d, gain, w_in, sinks, w_ro, w_so, w_o, w_up, w_dn, seq):
    n = x2d.shape[0]
    nt = n // TOK_TILE
    C = CHUNK
    const = lambda s: (0, 0)
    projected_tile = lambda s: (jnp.minimum(s, nt - 1), 0)
    mixed_tile = lambda s: (jnp.maximum(s - 1, 0), 0)

    def rows_per_step(total_rows, steps=nt):
        assert total_rows % steps == 0 and (total_rows // steps) % BF16_ROWS == 0
        return total_rows // steps

    step_block = lambda s: (jnp.minimum(s, nt - 1), 0)
    so_blocks = SWA_Q // BF16_ROWS
    blocks_per_head = SWA_DH // BF16_ROWS
    so_out_block = lambda s: (jnp.minimum(s, so_blocks - 1), 0)

    def so_src_block(s):
        j = jnp.minimum(s, so_blocks - 1)
        run, within = j // blocks_per_head, j % blocks_per_head
        head = (run % SWA_GROUPS) * SWA_REP + run // SWA_GROUPS
        return head * blocks_per_head + within, 0

    def row_specs(rows, cols, in_map=step_block, out_map=step_block):
        return pl.BlockSpec((rows, cols), in_map), pl.BlockSpec((rows, cols), out_map)

    in_g, _ = row_specs(rows_per_step(D_MODEL), w_in.shape[1])
    _, out_g = row_specs(rows_per_step(D_MODEL), D_GATES)
    in_ro, out_ro = row_specs(rows_per_step(RET_V), D_MODEL)
    in_so, out_so = row_specs(BF16_ROWS, D_MODEL, so_src_block, so_out_block)
    in_o, out_o = row_specs(rows_per_step(D_MODEL), D_MODEL)
    in_up, out_up = row_specs(rows_per_step(D_MODEL), D_FF)
    in_dn, out_dn = row_specs(rows_per_step(D_FF), D_MODEL)
    assert so_blocks <= nt

    weight_shape = lambda rows, cols: jax.ShapeDtypeStruct((rows, cols), bf16)
    return pl.pallas_call(
        functools.partial(_proj_mixer_kernel, tiles_per_seq=seq // TOK_TILE, num_tiles=nt),
        out_shape=(jax.ShapeDtypeStruct((n, D_MIXED), bf16),
                   weight_shape(D_MODEL, D_GATES), weight_shape(RET_V, D_MODEL), weight_shape(SWA_Q, D_MODEL),
                   weight_shape(D_MODEL, D_MODEL), weight_shape(D_MODEL, D_FF), weight_shape(D_FF, D_MODEL)),
        grid=(nt + 1,),
        in_specs=[
            pl.BlockSpec((TOK_TILE, D_MODEL), projected_tile),
            pl.BlockSpec((1, D_MODEL), const),
            pl.BlockSpec(memory_space=pl.ANY),
            pl.BlockSpec(memory_space=pltpu.SMEM),
            in_g, in_ro, in_so, in_o, in_up, in_dn,
        ],
        out_specs=(pl.BlockSpec((TOK_TILE, D_MIXED), mixed_tile), out_g, out_ro, out_so, out_o, out_up, out_dn),
        scratch_shapes=[
            pltpu.VMEM((D_MODEL, D_MIXW), bf16),
            pltpu.VMEM((STAGE_SLOTS, STAGE_ROWS, STAGE_COLS), f32),
            pltpu.SemaphoreType.DMA((STAGE_SLOTS,)),
            pltpu.VMEM((TOK_TILE, D_MIXIN), bf16),
            pltpu.VMEM((TOK_TILE, D_MIXIN), bf16),
            pltpu.VMEM((CHUNKS_PER_TILE, RET_QK, C), bf16),
            pltpu.VMEM((CHUNKS_PER_TILE, RET_QK, C), bf16),
            pltpu.VMEM((TOK_TILE, RET_QK), f32),
            pltpu.VMEM((TOK_TILE, RET_QK), f32),
            pltpu.VMEM((RET_HEADS, RET_DK, RET_DV), f32),
            pltpu.VMEM((SWA_GROUPS, C, SWA_KV), bf16),
            pltpu.VMEM((C, SWA_KV), bf16),
            pltpu.VMEM((RET_HEADS, C, C), f32),
            pltpu.VMEM((2 * SWA_GROUPS, SWA_REP * C, 2 * C), f32),
        ],
        compiler_params=pltpu.CompilerParams(
            dimension_semantics=("arbitrary",), vmem_limit_bytes=VMEM_LIMIT_BYTES),
        name="proj_mixer",
    )(x2d, gain, w_in, sinks, w_in, w_ro, w_so, w_o, w_up, w_dn)


def _post_kernel(x_ref, mixed_ref, gpre_mix_ref, wg_ref, wro_ref, wso_ref, wo_ref, gpost_mix_ref,
                 gpre_mlp_ref, wup_ref, wdn_ref, gpost_mlp_ref, o_ref, *, sub_rows, ff_chunk):
    tm = x_ref.shape[0]
    ff_starts = list(range(0, D_FF, ff_chunk))

    def row_block(rows):
        x = x_ref[rows, :]
        y_r = jnp.dot(mixed_ref[rows, X_RET:X_RET + RET_V], wro_ref[...], preferred_element_type=f32)
        y_s = jnp.dot(mixed_ref[rows, X_SWA:X_SWA + SWA_Q], wso_ref[...], preferred_element_type=f32)
        h = (x * _rms_scale(x) * gpre_mix_ref[...]).astype(bf16)
        gate_r = jnp.dot(h, wg_ref[:, 0:D_MODEL], preferred_element_type=f32)
        gate_s = jnp.dot(h, wg_ref[:, D_MODEL:D_GATES], preferred_element_type=f32)
        yield
        merged = _sigmoid(gate_r) * y_r + _sigmoid(gate_s) * y_s
        z = jnp.dot(merged.astype(bf16), wo_ref[...], preferred_element_type=f32)
        yield
        x1 = x + z * _rms_scale(z) * gpost_mix_ref[...]
        h1 = (x1 * _rms_scale(x1) * gpre_mlp_ref[...]).astype(bf16)

        def up(c0):
            return jnp.dot(h1, wup_ref[:, c0:c0 + ff_chunk], preferred_element_type=f32)

        acc = None
        u_next = up(ff_starts[0])
        yield
        for n, c0 in enumerate(ff_starts):
            u = u_next
            if n + 1 < len(ff_starts):
                u_next = up(ff_starts[n + 1])
            u = jnp.maximum(u, 0.0)
            part = jnp.dot((u * u).astype(bf16), wdn_ref[c0:c0 + ff_chunk, :], preferred_element_type=f32)
            acc = part if acc is None else acc + part
            yield
        o_ref[rows, :] = x1 + acc * _rms_scale(acc) * gpost_mlp_ref[...]

    active = [row_block(slice(r0, r0 + sub_rows)) for r0 in range(0, tm, sub_rows)]
    while active:
        for gen in list(active):
            if next(gen, "done") == "done":
                active.remove(gen)


def _post(x2d, mixed, gpre_mix, w_gates, w_ro, w_so, w_o, gpost_mix, gpre_mlp, w_up, w_dn, gpost_mlp, *,
          tm=POST_TILE, sub_rows=256, ff_chunk=1024):
    n = x2d.shape[0]
    const = lambda i: (0, 0)
    tile = lambda width: pl.BlockSpec((tm, width), lambda i: (i, 0))
    weight = lambda shape: pl.BlockSpec(shape, const, pipeline_mode=pl.Buffered(1))
    gain = pl.BlockSpec((1, D_MODEL), const)
    return pl.pallas_call(
        functools.partial(_post_kernel, sub_rows=sub_rows, ff_chunk=ff_chunk),
        out_shape=jax.ShapeDtypeStruct((n, D_MODEL), f32),
        grid=(n // tm,),
        in_specs=[
            tile(D_MODEL), tile(D_MIXED),
            gain, weight((D_MODEL, D_GATES)),
            weight((RET_V, D_MODEL)), weight((SWA_Q, D_MODEL)), weight((D_MODEL, D_MODEL)), gain,
            gain, weight((D_MODEL, D_FF)), weight((D_FF, D_MODEL)), gain,
        ],
        out_specs=tile(D_MODEL),
        compiler_params=pltpu.CompilerParams(
            dimension_semantics=("arbitrary",), vmem_limit_bytes=VMEM_LIMIT_BYTES),
        name="post",
    )(x2d, mixed, gpre_mix, w_gates, w_ro, w_so, w_o, gpost_mix, gpre_mlp, w_up, w_dn, gpost_mlp)


def kernel(x, pre_mix_norm, w_in, w_ret_out, w_swa_out, w_out, sinks, post_mix_norm, pre_mlp_norm, w_up,
           w_down, post_mlp_norm):
    batch, seq, d = x.shape
    depth = w_in.shape[0]
    assert d == D_MODEL and seq % TOK_TILE == 0 and seq // TOK_TILE > 1
    x2d = x.reshape(batch * seq, d)
    for l in range(depth):
        pre_mix_gain = pre_mix_norm[l][None, :]
        mixed, w_gates, w_ro, w_so, w_o, w_up_bf16, w_dn_bf16 = _proj_mixer(
            x2d, pre_mix_gain, w_in[l], sinks[l].astype(f32), w_ret_out[l], w_swa_out[l], w_out[l], w_up[l],
            w_down[l], seq)
        x2d = _post(x2d, mixed, pre_mix_gain, w_gates, w_ro, w_so, w_o, post_mix_norm[l][None, :],
                    pre_mlp_norm[l][None, :], w_up_bf16, w_dn_bf16, post_mlp_norm[l][None, :])
    return x2d.reshape(batch, seq, d)
```

```python
import functools

import numpy as np
import jax
import jax.numpy as jnp
from jax import lax
from jax.experimental import pallas as pl
from jax.experimental.pallas import tpu as pltpu

D_MODEL = 1024
RET_HEADS = 4
RET_DK = 128
RET_DV = 256
CHUNK = 128
SWA_HEADS = 8
SWA_GROUPS = 2
SWA_REP = SWA_HEADS // SWA_GROUPS
SWA_DH = 64
D_FF = 4 * D_MODEL
EPS = 1e-6

RET_QK = RET_HEADS * RET_DK
RET_V = RET_HEADS * RET_DV
SWA_Q = SWA_HEADS * SWA_DH
SWA_KV = SWA_GROUPS * SWA_DH

W_QR = 0
W_KR = W_QR + RET_QK
W_VR = W_KR + RET_QK
W_GR = W_VR + RET_V
W_QS = W_GR + RET_V
W_KS = W_QS + SWA_Q
W_VS = W_KS + SWA_KV
D_MIXW = W_VS + SWA_KV
D_GATES = 2 * D_MODEL

M_QR = 0
M_KR = M_QR + RET_QK
M_VR = M_KR + RET_QK
M_SG = M_VR + RET_V
M_QD = M_SG + RET_V
M_QS = M_QD + RET_QK
M_KG = M_QS + SWA_Q
M_VS = M_KG + SWA_GROUPS * SWA_KV
D_MIXIN = M_VS + SWA_KV

X_RET = 0
X_SWA = X_RET + RET_V
D_MIXED = X_SWA + SWA_Q

TOK_TILE = 512
CHUNKS_PER_TILE = TOK_TILE // CHUNK
PROJ_PIECE_COLS = 256
POST_TILE = 512
POST_BLOCK_ROWS = 256
FF_CHUNK = 1024
STAGE_ROWS, STAGE_COLS = 512, 1024
STAGE_SLOTS = 3
BF16_TILE_ROWS = 16

VMEM_LIMIT_BYTES = 56 * 1024 * 1024

LOG_GAMMA = [float(np.log(1.0 - 2.0 ** (-5.0 - h))) for h in range(RET_HEADS)]
ALIBI_SLOPES = [float(2.0 ** (-8.0 / SWA_HEADS * (h + 1))) for h in range(SWA_HEADS)]
K_SCALE = float(RET_DK ** -0.5)
S_SCALE = float(SWA_DH ** -0.5)
LOG2E = float(np.log2(np.e))

f32 = jnp.float32
bf16 = jnp.bfloat16


def _swa_out_row_order():
    return np.arange(SWA_Q).reshape(SWA_GROUPS, SWA_REP, SWA_DH).transpose(1, 0, 2).reshape(-1)


def _rms_scale(v):
    return lax.rsqrt(jnp.mean(v * v, axis=-1, keepdims=True) + EPS)


def _sigmoid(v):
    return 0.5 * jnp.tanh(0.5 * v) + 0.5


def _cols(offset, index, width):
    return slice(offset + index * width, offset + (index + 1) * width)


def _store_bf16(dst_ref, chunk):
    dst_ref[...] = chunk[0:dst_ref.shape[0], 0:dst_ref.shape[1]].astype(bf16)


def _plain_weight_jobs(src_hbm, dst_ref, col0=0):
    n_rows, n_cols = dst_ref.shape
    jobs = []
    for r0 in range(0, n_rows, STAGE_ROWS):
        for c0 in range(0, n_cols, STAGE_COLS):
            rows, cols = min(STAGE_ROWS, n_rows - r0), min(STAGE_COLS, n_cols - c0)
            piece = (src_hbm.at[r0:r0 + rows, col0 + c0:col0 + c0 + cols], slice(0, rows), slice(0, cols))
            jobs.append(([piece], functools.partial(_store_bf16, dst_ref.at[r0:r0 + rows, c0:c0 + cols])))
    return jobs


def _stream_weight_chunks(jobs, stage_ref, sem_ref):
    slots = stage_ref.shape[0]

    def copies(k):
        slot = k % slots
        return [pltpu.make_async_copy(src, stage_ref.at[slot, rows, cols], sem_ref.at[slot])
                for src, rows, cols in jobs[k][0]]

    for k in range(min(slots - 1, len(jobs))):
        for copy in copies(k):
            copy.start()
    for k, job in enumerate(jobs):
        ahead = k + slots - 1
        if ahead < len(jobs):
            for copy in copies(ahead):
                copy.start()
        for copy in copies(k):
            copy.wait()
        job[1](stage_ref[k % slots])


def _init_tables(sink_ref, qdec_ref, kdec_ref, dmat_ref, bias_ref):
    C = CHUNK
    pos = (lax.broadcasted_iota(jnp.int32, (TOK_TILE, RET_DK), 0) % C).astype(f32)
    i = lax.broadcasted_iota(jnp.int32, (C, C), 0).astype(f32)
    j = lax.broadcasted_iota(jnp.int32, (C, C), 1).astype(f32)
    diff = i - j
    for h in range(RET_HEADS):
        qdec_ref[:, _cols(0, h, RET_DK)] = jnp.exp(LOG_GAMMA[h] * (pos + 1.0))
        kdec_ref[:, _cols(0, h, RET_DK)] = jnp.exp(LOG_GAMMA[h] * (C - 1.0 - pos)) * K_SCALE
        dmat_ref[h] = jnp.where(diff >= 0, jnp.exp(LOG_GAMMA[h] * jnp.maximum(diff, 0.0)), 0.0) * K_SCALE
    row = lax.broadcasted_iota(jnp.int32, (SWA_REP * C, 2 * C), 0)
    kpos = lax.broadcasted_iota(jnp.int32, (SWA_REP * C, 2 * C), 1)
    rep = row // C
    dist = (row - rep * C + C) - kpos
    valid = (dist >= 0) & (dist < C)
    distf = dist.astype(f32)
    for g in range(SWA_GROUPS):
        slope = jnp.zeros((SWA_REP * C, 2 * C), f32)
        sink = jnp.zeros((SWA_REP * C, 2 * C), f32)
        for r in range(SWA_REP):
            slope = jnp.where(rep == r, ALIBI_SLOPES[g * SWA_REP + r], slope)
            sink = jnp.where(rep == r, sink_ref[g * SWA_REP + r], sink)
        alibi = -slope * distf
        bias_ref[g] = jnp.where(kpos == 0, sink, jnp.where(valid, alibi, -jnp.inf)) * LOG2E
        bias_ref[SWA_GROUPS + g] = jnp.where(
            kpos == 0, sink, jnp.where(valid & (kpos >= C), alibi, -jnp.inf)) * LOG2E


def _projection_pieces(x_ref, g_ref, w_ref, qdec_ref, kdec_ref, mix_ref, kdt_ref):
    C = CHUNK
    shared = {}

    def prepare():
        x = x_ref[...]
        shared["h"] = (x * _rms_scale(x) * g_ref[...]).astype(bf16)

    def proj(c0, width):
        return jnp.dot(shared["h"], w_ref[:, c0:c0 + width], preferred_element_type=f32)

    width = PROJ_PIECE_COLS

    def swish_gate(part):
        def run():
            g = proj(W_GR + part * width, width)
            mix_ref[:, _cols(M_SG, part, width)] = (g * _sigmoid(g)).astype(bf16)
        return run

    def queries(part):
        def run():
            q = proj(W_QR + part * width, width)
            mix_ref[:, _cols(M_QR, part, width)] = q.astype(bf16)
            mix_ref[:, _cols(M_QD, part, width)] = (q * qdec_ref[:, _cols(0, part, width)]).astype(bf16)
        return run

    def keys(part):
        def run():
            k = proj(W_KR + part * width, width)
            mix_ref[:, _cols(M_KR, part, width)] = k.astype(bf16)
            kd = k * kdec_ref[:, _cols(0, part, width)]
            heads_per_piece = width // RET_DK
            for ci in range(CHUNKS_PER_TILE):
                for hd in range(heads_per_piece):
                    kdt_ref[ci, _cols(0, part * heads_per_piece + hd, RET_DK), :] = (
                        kd[ci * C:(ci + 1) * C, _cols(0, hd, RET_DK)].T.astype(bf16))
        return run

    def attention_queries(part):
        def run():
            mix_ref[:, _cols(M_QS, part, width)] = proj(W_QS + part * width, width).astype(bf16)
        return run

    def attention_keys_values():
        kv_s = proj(W_KS, 2 * SWA_KV)
        ks = kv_s[:, 0:SWA_KV] * (S_SCALE * LOG2E)
        lane = lax.broadcasted_iota(jnp.int32, (TOK_TILE, SWA_KV), 1)
        for grp in range(SWA_GROUPS):
            in_group = (lane >= grp * SWA_DH) & (lane < (grp + 1) * SWA_DH)
            mix_ref[:, _cols(M_KG, grp, SWA_KV)] = jnp.where(in_group, ks, 0.0).astype(bf16)
        mix_ref[:, M_VS:M_VS + SWA_KV] = kv_s[:, SWA_KV:2 * SWA_KV].astype(bf16)

    def values(part):
        def run():
            mix_ref[:, _cols(M_VR, part, width)] = proj(W_VR + part * width, width).astype(bf16)
        return run

    light = [queries(n) for n in range(RET_QK // width)]
    light += [keys(n) for n in range(RET_QK // width)]
    light += [attention_queries(n) for n in range(SWA_Q // width)]
    light += [attention_keys_values]
    light += [values(n) for n in range(RET_V // width)]
    heavy = [swish_gate(n) for n in range(RET_V // width)]
    pieces = []
    per_heavy = -(-len(light) // len(heavy))
    for n, piece in enumerate(heavy):
        pieces.append(piece)
        pieces += light[n * per_heavy:(n + 1) * per_heavy]
    return prepare, pieces


def _mixer_phases(mix_ref, kdt_ref, first_tile_of_seq, mixed_ref,
                  state_ref, kprev_ref, vprev_ref, dmat_ref, bias_ref):
    C = CHUNK
    first_group = lax.broadcasted_iota(jnp.int32, (C, SWA_KV), 1) < SWA_DH
    ones_cols = jnp.ones((2 * C, SWA_KV), bf16)
    units = [(j, g) for j in range(SWA_REP) for g in range(SWA_GROUPS)]
    live = [dict() for _ in range(CHUNKS_PER_TILE)]

    def rows(t):
        return slice(t * C, (t + 1) * C)

    def zero_first_row(block):
        top = block[0:BF16_TILE_ROWS, :]
        is_first = lax.broadcasted_iota(jnp.int32, top.shape, 0) == 0
        return jnp.concatenate([jnp.where(is_first, jnp.zeros_like(top), top), block[BF16_TILE_ROWS:, :]], axis=0)

    def first_matmuls(t):
        st = live[t]
        st["qk"] = [lax.dot_general(mix_ref[rows(t), _cols(M_QR, h, RET_DK)], mix_ref[rows(t), _cols(M_KR, h, RET_DK)],
                                    (((1,), (1,)), ((), ())), preferred_element_type=f32)
                    for h in range(RET_HEADS)]
        st["kv"] = [jnp.dot(kdt_ref[t, _cols(0, h, RET_DK), :], mix_ref[rows(t), _cols(M_VR, h, RET_DV)],
                            preferred_element_type=f32) for h in range(RET_HEADS)]
        if t == 0:
            keys = [jnp.concatenate([kprev_ref[g], mix_ref[0:C, _cols(M_KG, g, SWA_KV)]], axis=0)
                    for g in range(SWA_GROUPS)]
            vals = jnp.concatenate([vprev_ref[...], mix_ref[0:C, M_VS:M_VS + SWA_KV]], axis=0)
            table0 = jnp.where(first_tile_of_seq, SWA_GROUPS, 0)
        else:
            prev_and_current = slice((t - 1) * C, (t + 1) * C)
            keys = [mix_ref[prev_and_current, _cols(M_KG, g, SWA_KV)] for g in range(SWA_GROUPS)]
            vals = mix_ref[prev_and_current, M_VS:M_VS + SWA_KV]
            table0 = 0
        keys = [zero_first_row(k) for k in keys]
        st["vv_ones"] = jnp.concatenate([zero_first_row(vals), ones_cols], axis=1)
        st["s"] = {}
        for j, g in units:
            s = lax.dot_general(mix_ref[rows(t), _cols(M_QS, j, C)], keys[g], (((1,), (1,)), ((), ())),
                                preferred_element_type=f32)
            st["s"][j, g] = s + bias_ref[table0 + g, _cols(0, j, C), :]

    def retention_vector_work(t):
        st = live[t]
        st["a"], st["state_bf16"] = [], []
        for h in range(RET_HEADS):
            s_prev = state_ref[h]
            st["a"].append((st["qk"][h] * dmat_ref[h]).astype(bf16))
            st["state_bf16"].append(s_prev.astype(bf16))
            state_ref[h] = float(np.exp(LOG_GAMMA[h] * C)) * s_prev + st["kv"][h]
        del st["qk"], st["kv"]

    def softmax_numerators(t):
        st = live[t]
        st["p"] = {}
        for u in units:
            s = st["s"][u]
            st["p"][u] = jnp.exp2(s - jnp.max(s, axis=-1, keepdims=True)).astype(bf16)
        del st["s"]

    def second_matmuls(t):
        st = live[t]
        st["o"] = [jnp.dot(jnp.concatenate([st["a"][h], mix_ref[rows(t), _cols(M_QD, h, RET_DK)]], axis=1),
                           jnp.concatenate([mix_ref[rows(t), _cols(M_VR, h, RET_DV)], st["state_bf16"][h]], axis=0),
                           preferred_element_type=f32) for h in range(RET_HEADS)]
        st["r"] = {u: jnp.dot(st["p"][u], st["vv_ones"], preferred_element_type=f32) for u in units}
        del st["a"], st["state_bf16"], st["p"], st["vv_ones"]

    def retention_output(t):
        st = live[t]
        for h in range(RET_HEADS):
            o = st["o"][h]
            mu = jnp.mean(o, axis=-1, keepdims=True)
            d = o - mu
            var = jnp.mean(d * d, axis=-1, keepdims=True)
            mixed_ref[rows(t), _cols(X_RET, h, RET_DV)] = (
                mix_ref[rows(t), _cols(M_SG, h, RET_DV)].astype(f32) * (d * lax.rsqrt(var + EPS))).astype(bf16)
        del st["o"]

    def attention_output(t):
        st = live[t]
        for j in range(SWA_REP):
            num = [st["r"][j, g][:, 0:SWA_KV] for g in range(SWA_GROUPS)]
            den = [st["r"][j, g][:, SWA_KV:2 * SWA_KV] for g in range(SWA_GROUPS)]
            o_pair = jnp.where(first_group, num[0], num[1]) / jnp.where(first_group, den[0], den[1])
            mixed_ref[rows(t), _cols(X_SWA, j, C)] = o_pair.astype(bf16)
        st.clear()

    def carry_last_block():
        last = slice(TOK_TILE - C, TOK_TILE)
        for g in range(SWA_GROUPS):
            kprev_ref[g] = mix_ref[last, _cols(M_KG, g, SWA_KV)]
        vprev_ref[...] = mix_ref[last, M_VS:M_VS + SWA_KV]

    phases = []
    for t in range(CHUNKS_PER_TILE + 1):
        if t < CHUNKS_PER_TILE:
            phases.append(functools.partial(first_matmuls, t))
        if t >= 1:
            phases += [functools.partial(fn, t - 1) for fn in (
                retention_vector_work, softmax_numerators, second_matmuls, retention_output, attention_output)]
    phases.append(carry_last_block)
    return phases


def _store_attention_columns(dst_ref, chunk):
    lanes = 2 * SWA_DH
    first_half = lax.broadcasted_iota(jnp.int32, (chunk.shape[0], lanes), 1) < SWA_DH
    for j in range(SWA_REP):
        a = chunk[:, _cols(0, j // 2, lanes)]
        b = chunk[:, _cols(0, (SWA_REP + j) // 2, lanes)]
        if j % 2 == 0:
            pair = jnp.where(first_half, a, pltpu.roll(b, SWA_DH, 1))
        else:
            pair = jnp.where(first_half, pltpu.roll(a, SWA_DH, 1), b)
        dst_ref[:, _cols(0, j, lanes)] = pair.astype(bf16)
    rest = slice(SWA_Q, SWA_Q + 2 * SWA_KV)
    dst_ref[:, rest] = chunk[:, rest].astype(bf16)


def _load_mixer_weights(w_in_hbm, w_ref, stage_ref, sem_ref):
    jobs = _plain_weight_jobs(w_in_hbm, w_ref.at[:, 0:W_QS])
    for r0 in range(0, D_MODEL, STAGE_ROWS):
        rows = slice(r0, r0 + STAGE_ROWS)
        piece = (w_in_hbm.at[rows, W_QS:D_MIXW], slice(0, STAGE_ROWS), slice(0, D_MIXW - W_QS))
        jobs.append(([piece], functools.partial(_store_attention_columns, w_ref.at[rows, W_QS:D_MIXW])))
    _stream_weight_chunks(jobs, stage_ref, sem_ref)


def _proj_mixer_kernel(x_ref, g_ref, w_in_hbm, sink_ref, mixed_ref,
                       w_ref, stage_ref, sem_ref,
                       mix_even_ref, mix_odd_ref, kdt_even_ref, kdt_odd_ref, qdec_ref, kdec_ref,
                       state_ref, kprev_ref, vprev_ref, dmat_ref, bias_ref, *, tiles_per_seq, num_tiles):
    s = pl.program_id(0)
    first_tile_of_seq = lax.rem(s + tiles_per_seq - 1, tiles_per_seq) == 0

    @pl.when(s == 0)
    def _first_step():
        _load_mixer_weights(w_in_hbm, w_ref, stage_ref, sem_ref)
        _init_tables(sink_ref, qdec_ref, kdec_ref, dmat_ref, bias_ref)

    @pl.when(first_tile_of_seq)
    def _reset_sequence_state():
        state_ref[...] = jnp.zeros_like(state_ref)
        kprev_ref[...] = jnp.zeros_like(kprev_ref)
        vprev_ref[...] = jnp.zeros_like(vprev_ref)

    even = (mix_even_ref, kdt_even_ref)
    odd = (mix_odd_ref, kdt_odd_ref)

    def step(write, read):
        prepare, pieces = (lambda: None), []
        if write is not None:
            prepare, pieces = _projection_pieces(x_ref, g_ref, w_ref, qdec_ref, kdec_ref, *write)
        phases = []
        if read is not None:
            phases = _mixer_phases(*read, first_tile_of_seq, mixed_ref,
                                   state_ref, kprev_ref, vprev_ref, dmat_ref, bias_ref)
        prepare()
        for phase in phases[:1]:
            phase()
        emitted = 0
        for n, phase in enumerate(phases[1:], start=1):
            while emitted < len(pieces) and emitted * (len(phases) - 1) < n * len(pieces):
                pieces[emitted]()
                emitted += 1
            phase()
        for piece in pieces[emitted:]:
            piece()

    last = num_tiles
    last_read = odd if (num_tiles - 1) % 2 else even

    @pl.when(s == 0)
    def _first_tile():
        step(even, None)

    @pl.when((s > 0) & (s < last) & (lax.rem(s, 2) == 0))
    def _even_step():
        step(even, odd)

    @pl.when((s > 0) & (s < last) & (lax.rem(s, 2) == 1))
    def _odd_step():
        step(odd, even)

    @pl.when(s == last)
    def _last_tile():
        step(None, last_read)


def _proj_mixer(x2d, gain, w_in, sinks, seq):
    n = x2d.shape[0]
    nt = n // TOK_TILE
    C = CHUNK
    const = lambda s: (0, 0)
    projected_tile = lambda s: (jnp.minimum(s, nt - 1), 0)
    mixed_tile = lambda s: (jnp.maximum(s - 1, 0), 0)
    return pl.pallas_call(
        functools.partial(_proj_mixer_kernel, tiles_per_seq=seq // TOK_TILE, num_tiles=nt),
        out_shape=jax.ShapeDtypeStruct((n, D_MIXED), bf16),
        grid=(nt + 1,),
        in_specs=[
            pl.BlockSpec((TOK_TILE, D_MODEL), projected_tile),
            pl.BlockSpec((1, D_MODEL), const),
            pl.BlockSpec(memory_space=pl.ANY),
            pl.BlockSpec(memory_space=pltpu.SMEM),
        ],
        out_specs=pl.BlockSpec((TOK_TILE, D_MIXED), mixed_tile),
        scratch_shapes=[
            pltpu.VMEM((D_MODEL, D_MIXW), bf16),
            pltpu.VMEM((STAGE_SLOTS, STAGE_ROWS, STAGE_COLS), f32),
            pltpu.SemaphoreType.DMA((STAGE_SLOTS,)),
            pltpu.VMEM((TOK_TILE, D_MIXIN), bf16),
            pltpu.VMEM((TOK_TILE, D_MIXIN), bf16),
            pltpu.VMEM((CHUNKS_PER_TILE, RET_QK, C), bf16),
            pltpu.VMEM((CHUNKS_PER_TILE, RET_QK, C), bf16),
            pltpu.VMEM((TOK_TILE, RET_QK), f32),
            pltpu.VMEM((TOK_TILE, RET_QK), f32),
            pltpu.VMEM((RET_HEADS, RET_DK, RET_DV), f32),
            pltpu.VMEM((SWA_GROUPS, C, SWA_KV), bf16),
            pltpu.VMEM((C, SWA_KV), bf16),
            pltpu.VMEM((RET_HEADS, C, C), f32),
            pltpu.VMEM((2 * SWA_GROUPS, SWA_REP * C, 2 * C), f32),
        ],
        compiler_params=pltpu.CompilerParams(
            dimension_semantics=("arbitrary",), vmem_limit_bytes=VMEM_LIMIT_BYTES),
        name="proj_mixer",
    )(x2d, gain, w_in, sinks)


def _load_post_weights(w_in_hbm, w_ro_hbm, w_so_hbm, w_o_hbm, w_up_hbm, w_dn_hbm,
                       wg_ref, wro_ref, wso_ref, wo_ref, wup_ref, wdn_ref, stage_ref, sem_ref):
    jobs = _plain_weight_jobs(w_in_hbm, wg_ref, col0=D_MIXW)
    jobs += _plain_weight_jobs(w_ro_hbm, wro_ref)
    runs = _swa_out_row_order().reshape(-1, SWA_DH)[:, 0]
    pieces = [(w_so_hbm.at[int(r0):int(r0) + SWA_DH, :], slice(n * SWA_DH, (n + 1) * SWA_DH), slice(0, STAGE_COLS))
              for n, r0 in enumerate(runs)]
    jobs += [(pieces, functools.partial(_store_bf16, wso_ref))]
    jobs += _plain_weight_jobs(w_o_hbm, wo_ref)
    jobs += _plain_weight_jobs(w_up_hbm, wup_ref)
    jobs += _plain_weight_jobs(w_dn_hbm, wdn_ref)
    _stream_weight_chunks(jobs, stage_ref, sem_ref)


def _post_kernel(x_ref, mixed_ref, gpre_mix_ref, w_in_hbm, w_ro_hbm, w_so_hbm, w_o_hbm, gpost_mix_ref,
                 gpre_mlp_ref, w_up_hbm, w_dn_hbm, gpost_mlp_ref, o_ref,
                 wg_ref, wro_ref, wso_ref, wo_ref, wup_ref, wdn_ref, stage_ref, sem_ref):
    tm = x_ref.shape[0]
    ff_starts = list(range(0, D_FF, FF_CHUNK))

    @pl.when(pl.program_id(0) == 0)
    def _first_step():
        _load_post_weights(w_in_hbm, w_ro_hbm, w_so_hbm, w_o_hbm, w_up_hbm, w_dn_hbm,
                           wg_ref, wro_ref, wso_ref, wo_ref, wup_ref, wdn_ref, stage_ref, sem_ref)

    def row_block(rows):
        x = x_ref[rows, :]
        y_r = jnp.dot(mixed_ref[rows, X_RET:X_RET + RET_V], wro_ref[...], preferred_element_type=f32)
        y_s = jnp.dot(mixed_ref[rows, X_SWA:X_SWA + SWA_Q], wso_ref[...], preferred_element_type=f32)
        h = (x * _rms_scale(x) * gpre_mix_ref[...]).astype(bf16)
        gate_r = jnp.dot(h, wg_ref[:, 0:D_MODEL], preferred_element_type=f32)
        gate_s = jnp.dot(h, wg_ref[:, D_MODEL:D_GATES], preferred_element_type=f32)
        yield
        merged = _sigmoid(gate_r) * y_r + _sigmoid(gate_s) * y_s
        z = jnp.dot(merged.astype(bf16), wo_ref[...], preferred_element_type=f32)
        yield
        x1 = x + z * _rms_scale(z) * gpost_mix_ref[...]
        h1 = (x1 * _rms_scale(x1) * gpre_mlp_ref[...]).astype(bf16)

        def up(c0):
            return jnp.dot(h1, wup_ref[:, c0:c0 + FF_CHUNK], preferred_element_type=f32)

        acc = None
        u_next = up(ff_starts[0])
        yield
        for n, c0 in enumerate(ff_starts):
            u = u_next
            if n + 1 < len(ff_starts):
                u_next = up(ff_starts[n + 1])
            u = jnp.maximum(u, 0.0)
            part = jnp.dot((u * u).astype(bf16), wdn_ref[c0:c0 + FF_CHUNK, :], preferred_element_type=f32)
            acc = part if acc is None else acc + part
            yield
        o_ref[rows, :] = x1 + acc * _rms_scale(acc) * gpost_mlp_ref[...]

    active = [row_block(slice(r0, r0 + POST_BLOCK_ROWS)) for r0 in range(0, tm, POST_BLOCK_ROWS)]
    while active:
        for gen in list(active):
            if next(gen, "done") == "done":
                active.remove(gen)


def _post(x2d, mixed, gpre_mix, w_in, w_ro, w_so, w_o, gpost_mix, gpre_mlp, w_up, w_dn, gpost_mlp):
    n = x2d.shape[0]
    const = lambda i: (0, 0)
    tile = lambda width: pl.BlockSpec((POST_TILE, width), lambda i: (i, 0))
    in_hbm = pl.BlockSpec(memory_space=pl.ANY)
    gain = pl.BlockSpec((1, D_MODEL), const)
    return pl.pallas_call(
        _post_kernel,
        out_shape=jax.ShapeDtypeStruct((n, D_MODEL), f32),
        grid=(n // POST_TILE,),
        in_specs=[
            tile(D_MODEL), tile(D_MIXED),
            gain, in_hbm,
            in_hbm, in_hbm, in_hbm, gain,
            gain, in_hbm, in_hbm, gain,
        ],
        out_specs=tile(D_MODEL),
        scratch_shapes=[
            pltpu.VMEM((D_MODEL, D_GATES), bf16),
            pltpu.VMEM((RET_V, D_MODEL), bf16),
            pltpu.VMEM((SWA_Q, D_MODEL), bf16),
            pltpu.VMEM((D_MODEL, D_MODEL), bf16),
            pltpu.VMEM((D_MODEL, D_FF), bf16),
            pltpu.VMEM((D_FF, D_MODEL), bf16),
            pltpu.VMEM((STAGE_SLOTS, STAGE_ROWS, STAGE_COLS), f32),
            pltpu.SemaphoreType.DMA((STAGE_SLOTS,)),
        ],
        compiler_params=pltpu.CompilerParams(
            dimension_semantics=("arbitrary",), vmem_limit_bytes=VMEM_LIMIT_BYTES),
        name="post",
    )(x2d, mixed, gpre_mix, w_in, w_ro, w_so, w_o, gpost_mix, gpre_mlp, w_up, w_dn, gpost_mlp)


def kernel(x, pre_mix_norm, w_in, w_ret_out, w_swa_out, w_out, sinks, post_mix_norm, pre_mlp_norm, w_up,
           w_down, post_mlp_norm):
    batch, seq, d = x.shape
    depth = w_in.shape[0]
    assert d == D_MODEL and seq % TOK_TILE == 0 and seq // TOK_TILE > 1 and (batch * seq) % POST_TILE == 0
    x2d = x.reshape(batch * seq, d)
    for l in range(depth):
        pre_mix_gain = pre_mix_norm[l][None, :]
        mixed = _proj_mixer(x2d, pre_mix_gain, w_in[l], sinks[l].astype(f32), seq)
        x2d = _post(x2d, mixed, pre_mix_gain, w_in[l], w_ret_out[l], w_swa_out[l], w_out[l],
                    post_mix_norm[l][None, :], pre_mlp_norm[l][None, :], w_up[l], w_down[l],
                    post_mlp_norm[l][None, :])
    return x2d.reshape(batch, seq, d)
```

```python
import functools

import numpy as np
import jax
import jax.numpy as jnp
from jax import lax
from jax.experimental import pallas as pl
from jax.experimental.pallas import tpu as pltpu

D_MODEL = 1024
RET_HEADS = 4
RET_DK = 128
RET_DV = 256
CHUNK = 128
SWA_HEADS = 8
SWA_GROUPS = 2
SWA_REP = SWA_HEADS // SWA_GROUPS
SWA_DH = 64
D_FF = 4 * D_MODEL
EPS = 1e-6

RET_QK = RET_HEADS * RET_DK
RET_V = RET_HEADS * RET_DV
SWA_Q = SWA_HEADS * SWA_DH
SWA_KV = SWA_GROUPS * SWA_DH

W_QR = 0
W_KR = W_QR + RET_QK
W_VR = W_KR + RET_QK
W_GR = W_VR + RET_V
W_QS = W_GR + RET_V
W_KS = W_QS + SWA_Q
W_VS = W_KS + SWA_KV
D_MIXW = W_VS + SWA_KV
D_GATES = 2 * D_MODEL

M_QR = 0
M_KR = M_QR + RET_QK
M_VR = M_KR + RET_QK
M_SG = M_VR + RET_V
M_QD = M_SG + RET_V
M_QS = M_QD + RET_QK
M_KG = M_QS + SWA_Q
M_VS = M_KG + SWA_GROUPS * SWA_KV
D_MIXIN = M_VS + SWA_KV

X_RET = 0
X_SWA = X_RET + RET_V
D_MIXED = X_SWA + SWA_Q

TOK_TILE = 512
CHUNKS_PER_TILE = TOK_TILE // CHUNK
PROJ_PIECE_COLS = 256
POST_TILE = 512
POST_BLOCK_ROWS = 256
FF_CHUNK = 1024
STAGE_ROWS, STAGE_COLS = 512, 1024
STAGE_SLOTS = 3
POST_STAGE_SLOTS = 2
BF16_TILE_ROWS = 16

VMEM_LIMIT_BYTES = 56 * 1024 * 1024

LOG_GAMMA = [float(np.log(1.0 - 2.0 ** (-5.0 - h))) for h in range(RET_HEADS)]
ALIBI_SLOPES = [float(2.0 ** (-8.0 / SWA_HEADS * (h + 1))) for h in range(SWA_HEADS)]
K_SCALE = float(RET_DK ** -0.5)
S_SCALE = float(SWA_DH ** -0.5)
LOG2E = float(np.log2(np.e))

f32 = jnp.float32
bf16 = jnp.bfloat16


def _swa_out_row_order():
    return np.arange(SWA_Q).reshape(SWA_GROUPS, SWA_REP, SWA_DH).transpose(1, 0, 2).reshape(-1)


def _rms_scale(v):
    return lax.rsqrt(jnp.mean(v * v, axis=-1, keepdims=True) + EPS)


def _sigmoid(v):
    return 0.5 * jnp.tanh(0.5 * v) + 0.5


def _cols(offset, index, width):
    return slice(offset + index * width, offset + (index + 1) * width)


def _store_bf16(dst_ref, chunk):
    dst_ref[...] = chunk[0:dst_ref.shape[0], 0:dst_ref.shape[1]].astype(bf16)


def _plain_weight_jobs(src_hbm, dst_ref, col0=0):
    n_rows, n_cols = dst_ref.shape
    jobs = []
    for r0 in range(0, n_rows, STAGE_ROWS):
        for c0 in range(0, n_cols, STAGE_COLS):
            rows, cols = min(STAGE_ROWS, n_rows - r0), min(STAGE_COLS, n_cols - c0)
            piece = (src_hbm.at[r0:r0 + rows, col0 + c0:col0 + c0 + cols], slice(0, rows), slice(0, cols))
            jobs.append(([piece], functools.partial(_store_bf16, dst_ref.at[r0:r0 + rows, c0:c0 + cols])))
    return jobs


def _stream_weight_chunks(jobs, stage_ref, sem_ref):
    slots = stage_ref.shape[0]

    def copies(k):
        slot = k % slots
        return [pltpu.make_async_copy(src, stage_ref.at[slot, rows, cols], sem_ref.at[slot])
                for src, rows, cols in jobs[k][0]]

    for k in range(min(slots - 1, len(jobs))):
        for copy in copies(k):
            copy.start()
    for k, job in enumerate(jobs):
        ahead = k + slots - 1
        if ahead < len(jobs):
            for copy in copies(ahead):
                copy.start()
        for copy in copies(k):
            copy.wait()
        job[1](stage_ref[k % slots])


def _init_tables(sink_ref, qdec_ref, kdec_ref, dmat_ref, bias_ref):
    C = CHUNK
    pos = (lax.broadcasted_iota(jnp.int32, (TOK_TILE, RET_DK), 0) % C).astype(f32)
    i = lax.broadcasted_iota(jnp.int32, (C, C), 0).astype(f32)
    j = lax.broadcasted_iota(jnp.int32, (C, C), 1).astype(f32)
    diff = i - j
    for h in range(RET_HEADS):
        qdec_ref[:, _cols(0, h, RET_DK)] = jnp.exp(LOG_GAMMA[h] * (pos + 1.0))
        kdec_ref[:, _cols(0, h, RET_DK)] = jnp.exp(LOG_GAMMA[h] * (C - 1.0 - pos)) * K_SCALE
        dmat_ref[h] = jnp.where(diff >= 0, jnp.exp(LOG_GAMMA[h] * jnp.maximum(diff, 0.0)), 0.0) * K_SCALE
    row = lax.broadcasted_iota(jnp.int32, (SWA_REP * C, 2 * C), 0)
    kpos = lax.broadcasted_iota(jnp.int32, (SWA_REP * C, 2 * C), 1)
    rep = row // C
    dist = (row - rep * C + C) - kpos
    valid = (dist >= 0) & (dist < C)
    distf = dist.astype(f32)
    for g in range(SWA_GROUPS):
        slope = jnp.zeros((SWA_REP * C, 2 * C), f32)
        sink = jnp.zeros((SWA_REP * C, 2 * C), f32)
        for r in range(SWA_REP):
            slope = jnp.where(rep == r, ALIBI_SLOPES[g * SWA_REP + r], slope)
            sink = jnp.where(rep == r, sink_ref[g * SWA_REP + r], sink)
        alibi = -slope * distf
        bias_ref[g] = jnp.where(kpos == 0, sink, jnp.where(valid, alibi, -jnp.inf)) * LOG2E
        bias_ref[SWA_GROUPS + g] = jnp.where(
            kpos == 0, sink, jnp.where(valid & (kpos >= C), alibi, -jnp.inf)) * LOG2E


def _projection_pieces(x_ref, g_ref, w_ref, qdec_ref, kdec_ref, mix_ref, kdt_ref):
    C = CHUNK
    shared = {}

    def prepare():
        x = x_ref[...]
        shared["h"] = (x * _rms_scale(x) * g_ref[...]).astype(bf16)

    def proj(c0, width):
        return jnp.dot(shared["h"], w_ref[:, c0:c0 + width], preferred_element_type=f32)

    width = PROJ_PIECE_COLS

    def swish_gate(part):
        def run():
            g = proj(W_GR + part * width, width)
            mix_ref[:, _cols(M_SG, part, width)] = (g * _sigmoid(g)).astype(bf16)
        return run

    def queries(part):
        def run():
            q = proj(W_QR + part * width, width)
            mix_ref[:, _cols(M_QR, part, width)] = q.astype(bf16)
            mix_ref[:, _cols(M_QD, part, width)] = (q * qdec_ref[:, _cols(0, part, width)]).astype(bf16)
        return run

    def keys(part):
        def run():
            k = proj(W_KR + part * width, width)
            mix_ref[:, _cols(M_KR, part, width)] = k.astype(bf16)
            kd = k * kdec_ref[:, _cols(0, part, width)]
            heads_per_piece = width // RET_DK
            for ci in range(CHUNKS_PER_TILE):
                for hd in range(heads_per_piece):
                    kdt_ref[ci, _cols(0, part * heads_per_piece + hd, RET_DK), :] = (
                        kd[ci * C:(ci + 1) * C, _cols(0, hd, RET_DK)].T.astype(bf16))
        return run

    def attention_queries(part):
        def run():
            mix_ref[:, _cols(M_QS, part, width)] = proj(W_QS + part * width, width).astype(bf16)
        return run

    def attention_keys_values():
        kv_s = proj(W_KS, 2 * SWA_KV)
        ks = kv_s[:, 0:SWA_KV] * (S_SCALE * LOG2E)
        lane = lax.broadcasted_iota(jnp.int32, (TOK_TILE, SWA_KV), 1)
        for grp in range(SWA_GROUPS):
            in_group = (lane >= grp * SWA_DH) & (lane < (grp + 1) * SWA_DH)
            mix_ref[:, _cols(M_KG, grp, SWA_KV)] = jnp.where(in_group, ks, 0.0).astype(bf16)
        mix_ref[:, M_VS:M_VS + SWA_KV] = kv_s[:, SWA_KV:2 * SWA_KV].astype(bf16)

    def values(part):
        def run():
            mix_ref[:, _cols(M_VR, part, width)] = proj(W_VR + part * width, width).astype(bf16)
        return run

    light = [queries(n) for n in range(RET_QK // width)]
    light += [keys(n) for n in range(RET_QK // width)]
    light += [attention_queries(n) for n in range(SWA_Q // width)]
    light += [attention_keys_values]
    light += [values(n) for n in range(RET_V // width)]
    heavy = [swish_gate(n) for n in range(RET_V // width)]
    pieces = []
    per_heavy = -(-len(light) // len(heavy))
    for n, piece in enumerate(heavy):
        pieces.append(piece)
        pieces += light[n * per_heavy:(n + 1) * per_heavy]
    return prepare, pieces


def _mixer_phases(mix_ref, kdt_ref, first_tile_of_seq, mixed_ref,
                  state_ref, kprev_ref, vprev_ref, dmat_ref, bias_ref):
    C = CHUNK
    first_group = lax.broadcasted_iota(jnp.int32, (C, SWA_KV), 1) < SWA_DH
    ones_cols = jnp.ones((2 * C, SWA_KV), bf16)
    units = [(j, g) for j in range(SWA_REP) for g in range(SWA_GROUPS)]
    live = [dict() for _ in range(CHUNKS_PER_TILE)]

    def rows(t):
        return slice(t * C, (t + 1) * C)

    def zero_first_row(block):
        top = block[0:BF16_TILE_ROWS, :]
        is_first = lax.broadcasted_iota(jnp.int32, top.shape, 0) == 0
        return jnp.concatenate([jnp.where(is_first, jnp.zeros_like(top), top), block[BF16_TILE_ROWS:, :]], axis=0)

    def first_matmuls(t):
        st = live[t]
        st["qk"] = [lax.dot_general(mix_ref[rows(t), _cols(M_QR, h, RET_DK)], mix_ref[rows(t), _cols(M_KR, h, RET_DK)],
                                    (((1,), (1,)), ((), ())), preferred_element_type=f32)
                    for h in range(RET_HEADS)]
        st["kv"] = [jnp.dot(kdt_ref[t, _cols(0, h, RET_DK), :], mix_ref[rows(t), _cols(M_VR, h, RET_DV)],
                            preferred_element_type=f32) for h in range(RET_HEADS)]
        if t == 0:
            keys = [jnp.concatenate([kprev_ref[g], mix_ref[0:C, _cols(M_KG, g, SWA_KV)]], axis=0)
                    for g in range(SWA_GROUPS)]
            vals = jnp.concatenate([vprev_ref[...], mix_ref[0:C, M_VS:M_VS + SWA_KV]], axis=0)
            table0 = jnp.where(first_tile_of_seq, SWA_GROUPS, 0)
        else:
            prev_and_current = slice((t - 1) * C, (t + 1) * C)
            keys = [mix_ref[prev_and_current, _cols(M_KG, g, SWA_KV)] for g in range(SWA_GROUPS)]
            vals = mix_ref[prev_and_current, M_VS:M_VS + SWA_KV]
            table0 = 0
        keys = [zero_first_row(k) for k in keys]
        st["vv_ones"] = jnp.concatenate([zero_first_row(vals), ones_cols], axis=1)
        st["s"] = {}
        for j, g in units:
            s = lax.dot_general(mix_ref[rows(t), _cols(M_QS, j, C)], keys[g], (((1,), (1,)), ((), ())),
                                preferred_element_type=f32)
            st["s"][j, g] = s + bias_ref[table0 + g, _cols(0, j, C), :]

    def retention_vector_work(t):
        st = live[t]
        st["a"], st["state_bf16"] = [], []
        for h in range(RET_HEADS):
            s_prev = state_ref[h]
            st["a"].append((st["qk"][h] * dmat_ref[h]).astype(bf16))
            st["state_bf16"].append(s_prev.astype(bf16))
            state_ref[h] = float(np.exp(LOG_GAMMA[h] * C)) * s_prev + st["kv"][h]
        del st["qk"], st["kv"]

    def softmax_numerators(t):
        st = live[t]
        st["p"] = {}
        for u in units:
            s = st["s"][u]
            st["p"][u] = jnp.exp2(s - jnp.max(s, axis=-1, keepdims=True)).astype(bf16)
        del st["s"]

    def second_matmuls(t):
        st = live[t]
        st["o"] = [jnp.dot(jnp.concatenate([st["a"][h], mix_ref[rows(t), _cols(M_QD, h, RET_DK)]], axis=1),
                           jnp.concatenate([mix_ref[rows(t), _cols(M_VR, h, RET_DV)], st["state_bf16"][h]], axis=0),
                           preferred_element_type=f32) for h in range(RET_HEADS)]
        st["r"] = {u: jnp.dot(st["p"][u], st["vv_ones"], preferred_element_type=f32) for u in units}
        del st["a"], st["state_bf16"], st["p"], st["vv_ones"]

    def retention_output(t):
        st = live[t]
        for h in range(RET_HEADS):
            o = st["o"][h]
            mu = jnp.mean(o, axis=-1, keepdims=True)
            d = o - mu
            var = jnp.mean(d * d, axis=-1, keepdims=True)
            mixed_ref[rows(t), _cols(X_RET, h, RET_DV)] = (
                mix_ref[rows(t), _cols(M_SG, h, RET_DV)].astype(f32) * (d * lax.rsqrt(var + EPS))).astype(bf16)
        del st["o"]

    def attention_output(t):
        st = live[t]
        for j in range(SWA_REP):
            num = [st["r"][j, g][:, 0:SWA_KV] for g in range(SWA_GROUPS)]
            den = [st["r"][j, g][:, SWA_KV:2 * SWA_KV] for g in range(SWA_GROUPS)]
            o_pair = jnp.where(first_group, num[0], num[1]) / jnp.where(first_group, den[0], den[1])
            mixed_ref[rows(t), _cols(X_SWA, j, C)] = o_pair.astype(bf16)
        st.clear()

    def carry_last_block():
        last = slice(TOK_TILE - C, TOK_TILE)
        for g in range(SWA_GROUPS):
            kprev_ref[g] = mix_ref[last, _cols(M_KG, g, SWA_KV)]
        vprev_ref[...] = mix_ref[last, M_VS:M_VS + SWA_KV]

    phases = []
    for t in range(CHUNKS_PER_TILE + 1):
        if t < CHUNKS_PER_TILE:
            phases.append(functools.partial(first_matmuls, t))
        if t >= 1:
            phases += [functools.partial(fn, t - 1) for fn in (
                retention_vector_work, softmax_numerators, second_matmuls, retention_output, attention_output)]
    phases.append(carry_last_block)
    return phases


def _store_attention_columns(dst_ref, chunk):
    lanes = 2 * SWA_DH
    first_half = lax.broadcasted_iota(jnp.int32, (chunk.shape[0], lanes), 1) < SWA_DH
    for j in range(SWA_REP):
        a = chunk[:, _cols(0, j // 2, lanes)]
        b = chunk[:, _cols(0, (SWA_REP + j) // 2, lanes)]
        if j % 2 == 0:
            pair = jnp.where(first_half, a, pltpu.roll(b, SWA_DH, 1))
        else:
            pair = jnp.where(first_half, pltpu.roll(a, SWA_DH, 1), b)
        dst_ref[:, _cols(0, j, lanes)] = pair.astype(bf16)
    rest = slice(SWA_Q, SWA_Q + 2 * SWA_KV)
    dst_ref[:, rest] = chunk[:, rest].astype(bf16)


def _load_mixer_weights(w_in_hbm, w_ref, stage_ref, sem_ref):
    jobs = _plain_weight_jobs(w_in_hbm, w_ref.at[:, 0:W_QS])
    for r0 in range(0, D_MODEL, STAGE_ROWS):
        rows = slice(r0, r0 + STAGE_ROWS)
        piece = (w_in_hbm.at[rows, W_QS:D_MIXW], slice(0, STAGE_ROWS), slice(0, D_MIXW - W_QS))
        jobs.append(([piece], functools.partial(_store_attention_columns, w_ref.at[rows, W_QS:D_MIXW])))
    _stream_weight_chunks(jobs, stage_ref, sem_ref)


def _proj_mixer_kernel(x_ref, g_ref, w_in_hbm, sink_ref, mixed_ref,
                       w_ref, stage_ref, sem_ref,
                       mix_even_ref, mix_odd_ref, kdt_even_ref, kdt_odd_ref, qdec_ref, kdec_ref,
                       state_ref, kprev_ref, vprev_ref, dmat_ref, bias_ref, *, tiles_per_seq, num_tiles):
    s = pl.program_id(0)
    first_tile_of_seq = lax.rem(s + tiles_per_seq - 1, tiles_per_seq) == 0

    @pl.when(s == 0)
    def _first_step():
        _load_mixer_weights(w_in_hbm, w_ref, stage_ref, sem_ref)
        _init_tables(sink_ref, qdec_ref, kdec_ref, dmat_ref, bias_ref)

    @pl.when(first_tile_of_seq)
    def _reset_sequence_state():
        state_ref[...] = jnp.zeros_like(state_ref)
        kprev_ref[...] = jnp.zeros_like(kprev_ref)
        vprev_ref[...] = jnp.zeros_like(vprev_ref)

    even = (mix_even_ref, kdt_even_ref)
    odd = (mix_odd_ref, kdt_odd_ref)

    def step(write, read):
        prepare, pieces = (lambda: None), []
        if write is not None:
            prepare, pieces = _projection_pieces(x_ref, g_ref, w_ref, qdec_ref, kdec_ref, *write)
        phases = []
        if read is not None:
            phases = _mixer_phases(*read, first_tile_of_seq, mixed_ref,
                                   state_ref, kprev_ref, vprev_ref, dmat_ref, bias_ref)
        prepare()
        for phase in phases[:1]:
            phase()
        emitted = 0
        for n, phase in enumerate(phases[1:], start=1):
            while emitted < len(pieces) and emitted * (len(phases) - 1) < n * len(pieces):
                pieces[emitted]()
                emitted += 1
            phase()
        for piece in pieces[emitted:]:
            piece()

    last = num_tiles
    last_read = odd if (num_tiles - 1) % 2 else even

    @pl.when(s == 0)
    def _first_tile():
        step(even, None)

    @pl.when((s > 0) & (s < last) & (lax.rem(s, 2) == 0))
    def _even_step():
        step(even, odd)

    @pl.when((s > 0) & (s < last) & (lax.rem(s, 2) == 1))
    def _odd_step():
        step(odd, even)

    @pl.when(s == last)
    def _last_tile():
        step(None, last_read)


def _proj_mixer(x2d, gain, w_in, sinks, seq):
    n = x2d.shape[0]
    nt = n // TOK_TILE
    C = CHUNK
    const = lambda s: (0, 0)
    projected_tile = lambda s: (jnp.minimum(s, nt - 1), 0)
    mixed_tile = lambda s: (jnp.maximum(s - 1, 0), 0)
    return pl.pallas_call(
        functools.partial(_proj_mixer_kernel, tiles_per_seq=seq // TOK_TILE, num_tiles=nt),
        out_shape=jax.ShapeDtypeStruct((n, D_MIXED), bf16),
        grid=(nt + 1,),
        in_specs=[
            pl.BlockSpec((TOK_TILE, D_MODEL), projected_tile),
            pl.BlockSpec((1, D_MODEL), const),
            pl.BlockSpec(memory_space=pl.ANY),
            pl.BlockSpec(memory_space=pltpu.SMEM),
        ],
        out_specs=pl.BlockSpec((TOK_TILE, D_MIXED), mixed_tile),
        scratch_shapes=[
            pltpu.VMEM((D_MODEL, D_MIXW), bf16),
            pltpu.VMEM((STAGE_SLOTS, STAGE_ROWS, STAGE_COLS), f32),
            pltpu.SemaphoreType.DMA((STAGE_SLOTS,)),
            pltpu.VMEM((TOK_TILE, D_MIXIN), bf16),
            pltpu.VMEM((TOK_TILE, D_MIXIN), bf16),
            pltpu.VMEM((CHUNKS_PER_TILE, RET_QK, C), bf16),
            pltpu.VMEM((CHUNKS_PER_TILE, RET_QK, C), bf16),
            pltpu.VMEM((TOK_TILE, RET_QK), f32),
            pltpu.VMEM((TOK_TILE, RET_QK), f32),
            pltpu.VMEM((RET_HEADS, RET_DK, RET_DV), f32),
            pltpu.VMEM((SWA_GROUPS, C, SWA_KV), bf16),
            pltpu.VMEM((C, SWA_KV), bf16),
            pltpu.VMEM((RET_HEADS, C, C), f32),
            pltpu.VMEM((2 * SWA_GROUPS, SWA_REP * C, 2 * C), f32),
        ],
        compiler_params=pltpu.CompilerParams(
            dimension_semantics=("arbitrary",), vmem_limit_bytes=VMEM_LIMIT_BYTES),
        name="proj_mixer",
    )(x2d, gain, w_in, sinks)


def _load_post_weights(w_in_hbm, w_ro_hbm, w_so_hbm, w_o_hbm, w_up_hbm, w_dn_hbm,
                       wg_ref, wro_ref, wso_ref, wo_ref, wup_ref, wdn_ref, stage_ref, sem_ref):
    jobs = _plain_weight_jobs(w_in_hbm, wg_ref, col0=D_MIXW)
    jobs += _plain_weight_jobs(w_ro_hbm, wro_ref)
    runs = _swa_out_row_order().reshape(-1, SWA_DH)[:, 0]
    pieces = [(w_so_hbm.at[int(r0):int(r0) + SWA_DH, :], slice(n * SWA_DH, (n + 1) * SWA_DH), slice(0, STAGE_COLS))
              for n, r0 in enumerate(runs)]
    jobs += [(pieces, functools.partial(_store_bf16, wso_ref))]
    jobs += _plain_weight_jobs(w_o_hbm, wo_ref)
    jobs += _plain_weight_jobs(w_up_hbm, wup_ref)
    jobs += _plain_weight_jobs(w_dn_hbm, wdn_ref)
    _stream_weight_chunks(jobs, stage_ref, sem_ref)


def _post_kernel(x_ref, mixed_ref, gpre_mix_ref, w_in_hbm, w_ro_hbm, w_so_hbm, w_o_hbm, gpost_mix_ref,
                 gpre_mlp_ref, w_up_hbm, w_dn_hbm, gpost_mlp_ref, o_ref,
                 wg_ref, wro_ref, wso_ref, wo_ref, wup_ref, wdn_ref, stage_ref, sem_ref,
                 x1_even_ref, x1_odd_ref, *, num_tiles):
    i = pl.program_id(0)
    tm = x_ref.shape[0]
    ff_starts = list(range(0, D_FF, FF_CHUNK))
    blocks = [slice(r0, r0 + POST_BLOCK_ROWS) for r0 in range(0, tm, POST_BLOCK_ROWS)]

    @pl.when(i == 0)
    def _first_step():
        _load_post_weights(w_in_hbm, w_ro_hbm, w_so_hbm, w_o_hbm, w_up_hbm, w_dn_hbm,
                           wg_ref, wro_ref, wso_ref, wo_ref, wup_ref, wdn_ref, stage_ref, sem_ref)

    def mixing_block(rows, x1_ref):
        x = x_ref[rows, :]
        y_r = jnp.dot(mixed_ref[rows, X_RET:X_RET + RET_V], wro_ref[...], preferred_element_type=f32)
        y_s = jnp.dot(mixed_ref[rows, X_SWA:X_SWA + SWA_Q], wso_ref[...], preferred_element_type=f32)
        h = (x * _rms_scale(x) * gpre_mix_ref[...]).astype(bf16)
        gate_r = jnp.dot(h, wg_ref[:, 0:D_MODEL], preferred_element_type=f32)
        gate_s = jnp.dot(h, wg_ref[:, D_MODEL:D_GATES], preferred_element_type=f32)
        yield
        merged = _sigmoid(gate_r) * y_r + _sigmoid(gate_s) * y_s
        z = jnp.dot(merged.astype(bf16), wo_ref[...], preferred_element_type=f32)
        yield
        x1_ref[rows, :] = x + z * _rms_scale(z) * gpost_mix_ref[...]

    def mlp_block(rows, x1_ref):
        x1 = x1_ref[rows, :]
        h1 = (x1 * _rms_scale(x1) * gpre_mlp_ref[...]).astype(bf16)

        def up(c0):
            return jnp.dot(h1, wup_ref[:, c0:c0 + FF_CHUNK], preferred_element_type=f32)

        acc = None
        u_next = up(ff_starts[0])
        yield
        for n, c0 in enumerate(ff_starts):
            u = u_next
            if n + 1 < len(ff_starts):
                u_next = up(ff_starts[n + 1])
            u = jnp.maximum(u, 0.0)
            part = jnp.dot((u * u).astype(bf16), wdn_ref[c0:c0 + FF_CHUNK, :], preferred_element_type=f32)
            acc = part if acc is None else acc + part
            yield
        o_ref[rows, :] = x1 + acc * _rms_scale(acc) * gpost_mlp_ref[...]

    def step(x1_write_ref, x1_read_ref):
        active = []
        if x1_write_ref is not None:
            active += [mixing_block(rows, x1_write_ref) for rows in blocks]
        if x1_read_ref is not None:
            active += [mlp_block(rows, x1_read_ref) for rows in blocks]
        while active:
            for gen in list(active):
                if next(gen, "done") == "done":
                    active.remove(gen)

    last_read = x1_odd_ref if (num_tiles - 1) % 2 else x1_even_ref

    @pl.when(i == 0)
    def _first_tile():
        step(x1_even_ref, None)

    @pl.when((i > 0) & (i < num_tiles) & (lax.rem(i, 2) == 0))
    def _even_step():
        step(x1_even_ref, x1_odd_ref)

    @pl.when((i > 0) & (i < num_tiles) & (lax.rem(i, 2) == 1))
    def _odd_step():
        step(x1_odd_ref, x1_even_ref)

    @pl.when(i == num_tiles)
    def _last_tile():
        step(None, last_read)


def _post(x2d, mixed, gpre_mix, w_in, w_ro, w_so, w_o, gpost_mix, gpre_mlp, w_up, w_dn, gpost_mlp):
    n = x2d.shape[0]
    nt = n // POST_TILE
    const = lambda i: (0, 0)
    tile = lambda width: pl.BlockSpec((POST_TILE, width), lambda i: (jnp.minimum(i, nt - 1), 0))
    in_hbm = pl.BlockSpec(memory_space=pl.ANY)
    gain = pl.BlockSpec((1, D_MODEL), const)
    return pl.pallas_call(
        functools.partial(_post_kernel, num_tiles=nt),
        out_shape=jax.ShapeDtypeStruct((n, D_MODEL), f32),
        grid=(nt + 1,),
        in_specs=[
            tile(D_MODEL), tile(D_MIXED),
            gain, in_hbm,
            in_hbm, in_hbm, in_hbm, gain,
            gain, in_hbm, in_hbm, gain,
        ],
        out_specs=pl.BlockSpec((POST_TILE, D_MODEL), lambda i: (jnp.maximum(i - 1, 0), 0)),
        scratch_shapes=[
            pltpu.VMEM((D_MODEL, D_GATES), bf16),
            pltpu.VMEM((RET_V, D_MODEL), bf16),
            pltpu.VMEM((SWA_Q, D_MODEL), bf16),
            pltpu.VMEM((D_MODEL, D_MODEL), bf16),
            pltpu.VMEM((D_MODEL, D_FF), bf16),
            pltpu.VMEM((D_FF, D_MODEL), bf16),
            pltpu.VMEM((POST_STAGE_SLOTS, STAGE_ROWS, STAGE_COLS), f32),
            pltpu.SemaphoreType.DMA((POST_STAGE_SLOTS,)),
            pltpu.VMEM((POST_TILE, D_MODEL), f32),
            pltpu.VMEM((POST_TILE, D_MODEL), f32),
        ],
        compiler_params=pltpu.CompilerParams(
            dimension_semantics=("arbitrary",), vmem_limit_bytes=VMEM_LIMIT_BYTES),
        name="post",
    )(x2d, mixed, gpre_mix, w_in, w_ro, w_so, w_o, gpost_mix, gpre_mlp, w_up, w_dn, gpost_mlp)


def kernel(x, pre_mix_norm, w_in, w_ret_out, w_swa_out, w_out, sinks, post_mix_norm, pre_mlp_norm, w_up,
           w_down, post_mlp_norm):
    batch, seq, d = x.shape
    depth = w_in.shape[0]
    assert d == D_MODEL and seq % TOK_TILE == 0 and seq // TOK_TILE > 1 and (batch * seq) % POST_TILE == 0
    x2d = x.reshape(batch * seq, d)
    for l in range(depth):
        pre_mix_gain = pre_mix_norm[l][None, :]
        mixed = _proj_mixer(x2d, pre_mix_gain, w_in[l], sinks[l].astype(f32), seq)
        x2d = _post(x2d, mixed, pre_mix_gain, w_in[l], w_ret_out[l], w_swa_out[l], w_out[l],
                    post_mix_norm[l][None, :], pre_mlp_norm[l][None, :], w_up[l], w_down[l],
                    post_mlp_norm[l][None, :])
    return x2d.reshape(batch, seq, d)
```

```python
import functools

import numpy as np
import jax
import jax.numpy as jnp
from jax import lax
from jax.experimental import pallas as pl
from jax.experimental.pallas import tpu as pltpu

D_MODEL = 1024
RET_HEADS = 4
RET_DK = 128
RET_DV = 256
CHUNK = 128
SWA_HEADS = 8
SWA_GROUPS = 2
SWA_REP = SWA_HEADS // SWA_GROUPS
SWA_DH = 64
D_FF = 4 * D_MODEL
EPS = 1e-6

RET_QK = RET_HEADS * RET_DK
RET_V = RET_HEADS * RET_DV
SWA_Q = SWA_HEADS * SWA_DH
SWA_KV = SWA_GROUPS * SWA_DH

W_QR = 0
W_KR = W_QR + RET_QK
W_VR = W_KR + RET_QK
W_GR = W_VR + RET_V
W_QS = W_GR + RET_V
W_KS = W_QS + SWA_Q
W_VS = W_KS + SWA_KV
D_MIXW = W_VS + SWA_KV
D_GATES = 2 * D_MODEL

M_QR = 0
M_KR = M_QR + RET_QK
M_VR = M_KR + RET_QK
M_SG = M_VR + RET_V
M_QD = M_SG + RET_V
M_QS = M_QD + RET_QK
M_KG = M_QS + SWA_Q
M_VS = M_KG + SWA_GROUPS * SWA_KV
D_MIXIN = M_VS + SWA_KV

X_RET = 0
X_SWA = X_RET + RET_V
D_MIXED = X_SWA + SWA_Q

TOK_TILE = 512
CHUNKS_PER_TILE = TOK_TILE // CHUNK
PROJ_PIECE_COLS = 256
MLP_WEIGHT_BLOCK_STEPS = 4
MLP_WEIGHT_PARTS = 4
POST_TILE = 512
POST_BLOCK_ROWS = 256
FF_CHUNK = 1024
STAGE_ROWS, STAGE_COLS = 512, 1024
STAGE_SLOTS = 3
BF16_TILE_ROWS = 16

VMEM_LIMIT_BYTES = 56 * 1024 * 1024

LOG_GAMMA = [float(np.log(1.0 - 2.0 ** (-5.0 - h))) for h in range(RET_HEADS)]
ALIBI_SLOPES = [float(2.0 ** (-8.0 / SWA_HEADS * (h + 1))) for h in range(SWA_HEADS)]
K_SCALE = float(RET_DK ** -0.5)
S_SCALE = float(SWA_DH ** -0.5)
LOG2E = float(np.log2(np.e))

f32 = jnp.float32
bf16 = jnp.bfloat16


def _swa_out_row_order():
    return np.arange(SWA_Q).reshape(SWA_GROUPS, SWA_REP, SWA_DH).transpose(1, 0, 2).reshape(-1)


def _rms_scale(v):
    return lax.rsqrt(jnp.mean(v * v, axis=-1, keepdims=True) + EPS)


def _sigmoid(v):
    return 0.5 * jnp.tanh(0.5 * v) + 0.5


def _cols(offset, index, width):
    return slice(offset + index * width, offset + (index + 1) * width)


def _store_bf16(dst_ref, chunk):
    dst_ref[...] = chunk[0:dst_ref.shape[0], 0:dst_ref.shape[1]].astype(bf16)


def _plain_weight_jobs(src_hbm, dst_ref, col0=0):
    n_rows, n_cols = dst_ref.shape
    jobs = []
    for r0 in range(0, n_rows, STAGE_ROWS):
        for c0 in range(0, n_cols, STAGE_COLS):
            rows, cols = min(STAGE_ROWS, n_rows - r0), min(STAGE_COLS, n_cols - c0)
            piece = (src_hbm.at[r0:r0 + rows, col0 + c0:col0 + c0 + cols], slice(0, rows), slice(0, cols))
            jobs.append(([piece], functools.partial(_store_bf16, dst_ref.at[r0:r0 + rows, c0:c0 + cols])))
    return jobs


def _stream_weight_chunks(jobs, stage_ref, sem_ref):
    slots = stage_ref.shape[0]

    def copies(k):
        slot = k % slots
        return [pltpu.make_async_copy(src, stage_ref.at[slot, rows, cols], sem_ref.at[slot])
                for src, rows, cols in jobs[k][0]]

    for k in range(min(slots - 1, len(jobs))):
        for copy in copies(k):
            copy.start()
    for k, job in enumerate(jobs):
        ahead = k + slots - 1
        if ahead < len(jobs):
            for copy in copies(ahead):
                copy.start()
        for copy in copies(k):
            copy.wait()
        job[1](stage_ref[k % slots])


def _init_tables(sink_ref, qdec_ref, kdec_ref, dmat_ref, bias_ref):
    C = CHUNK
    pos = (lax.broadcasted_iota(jnp.int32, (TOK_TILE, RET_DK), 0) % C).astype(f32)
    i = lax.broadcasted_iota(jnp.int32, (C, C), 0).astype(f32)
    j = lax.broadcasted_iota(jnp.int32, (C, C), 1).astype(f32)
    diff = i - j
    for h in range(RET_HEADS):
        qdec_ref[:, _cols(0, h, RET_DK)] = jnp.exp(LOG_GAMMA[h] * (pos + 1.0))
        kdec_ref[:, _cols(0, h, RET_DK)] = jnp.exp(LOG_GAMMA[h] * (C - 1.0 - pos)) * K_SCALE
        dmat_ref[h] = jnp.where(diff >= 0, jnp.exp(LOG_GAMMA[h] * jnp.maximum(diff, 0.0)), 0.0) * K_SCALE
    row = lax.broadcasted_iota(jnp.int32, (SWA_REP * C, 2 * C), 0)
    kpos = lax.broadcasted_iota(jnp.int32, (SWA_REP * C, 2 * C), 1)
    rep = row // C
    dist = (row - rep * C + C) - kpos
    valid = (dist >= 0) & (dist < C)
    distf = dist.astype(f32)
    for g in range(SWA_GROUPS):
        slope = jnp.zeros((SWA_REP * C, 2 * C), f32)
        sink = jnp.zeros((SWA_REP * C, 2 * C), f32)
        for r in range(SWA_REP):
            slope = jnp.where(rep == r, ALIBI_SLOPES[g * SWA_REP + r], slope)
            sink = jnp.where(rep == r, sink_ref[g * SWA_REP + r], sink)
        alibi = -slope * distf
        bias_ref[g] = jnp.where(kpos == 0, sink, jnp.where(valid, alibi, -jnp.inf)) * LOG2E
        bias_ref[SWA_GROUPS + g] = jnp.where(
            kpos == 0, sink, jnp.where(valid & (kpos >= C), alibi, -jnp.inf)) * LOG2E


def _projection_pieces(x_ref, g_ref, w_ref, qdec_ref, kdec_ref, mix_ref, kdt_ref):
    C = CHUNK
    shared = {}

    def prepare():
        x = x_ref[...]
        shared["h"] = (x * _rms_scale(x) * g_ref[...]).astype(bf16)

    def proj(c0, width):
        return jnp.dot(shared["h"], w_ref[:, c0:c0 + width], preferred_element_type=f32)

    width = PROJ_PIECE_COLS

    def swish_gate(part):
        def run():
            g = proj(W_GR + part * width, width)
            mix_ref[:, _cols(M_SG, part, width)] = (g * _sigmoid(g)).astype(bf16)
        return run

    def queries(part):
        def run():
            q = proj(W_QR + part * width, width)
            mix_ref[:, _cols(M_QR, part, width)] = q.astype(bf16)
            mix_ref[:, _cols(M_QD, part, width)] = (q * qdec_ref[:, _cols(0, part, width)]).astype(bf16)
        return run

    def keys(part):
        def run():
            k = proj(W_KR + part * width, width)
            mix_ref[:, _cols(M_KR, part, width)] = k.astype(bf16)
            kd = k * kdec_ref[:, _cols(0, part, width)]
            heads_per_piece = width // RET_DK
            for ci in range(CHUNKS_PER_TILE):
                for hd in range(heads_per_piece):
                    kdt_ref[ci, _cols(0, part * heads_per_piece + hd, RET_DK), :] = (
                        kd[ci * C:(ci + 1) * C, _cols(0, hd, RET_DK)].T.astype(bf16))
        return run

    def attention_queries(part):
        def run():
            mix_ref[:, _cols(M_QS, part, width)] = proj(W_QS + part * width, width).astype(bf16)
        return run

    def attention_keys_values():
        kv_s = proj(W_KS, 2 * SWA_KV)
        ks = kv_s[:, 0:SWA_KV] * (S_SCALE * LOG2E)
        lane = lax.broadcasted_iota(jnp.int32, (TOK_TILE, SWA_KV), 1)
        for grp in range(SWA_GROUPS):
            in_group = (lane >= grp * SWA_DH) & (lane < (grp + 1) * SWA_DH)
            mix_ref[:, _cols(M_KG, grp, SWA_KV)] = jnp.where(in_group, ks, 0.0).astype(bf16)
        mix_ref[:, M_VS:M_VS + SWA_KV] = kv_s[:, SWA_KV:2 * SWA_KV].astype(bf16)

    def values(part):
        def run():
            mix_ref[:, _cols(M_VR, part, width)] = proj(W_VR + part * width, width).astype(bf16)
        return run

    light = [queries(n) for n in range(RET_QK // width)]
    light += [keys(n) for n in range(RET_QK // width)]
    light += [attention_queries(n) for n in range(SWA_Q // width)]
    light += [attention_keys_values]
    light += [values(n) for n in range(RET_V // width)]
    heavy = [swish_gate(n) for n in range(RET_V // width)]
    pieces = []
    per_heavy = -(-len(light) // len(heavy))
    for n, piece in enumerate(heavy):
        pieces.append(piece)
        pieces += light[n * per_heavy:(n + 1) * per_heavy]
    return prepare, pieces


def _mixer_phases(mix_ref, kdt_ref, first_tile_of_seq, mixed_ref,
                  state_ref, kprev_ref, vprev_ref, dmat_ref, bias_ref):
    C = CHUNK
    first_group = lax.broadcasted_iota(jnp.int32, (C, SWA_KV), 1) < SWA_DH
    ones_cols = jnp.ones((2 * C, SWA_KV), bf16)
    units = [(j, g) for j in range(SWA_REP) for g in range(SWA_GROUPS)]
    live = [dict() for _ in range(CHUNKS_PER_TILE)]

    def rows(t):
        return slice(t * C, (t + 1) * C)

    def zero_first_row(block):
        top = block[0:BF16_TILE_ROWS, :]
        is_first = lax.broadcasted_iota(jnp.int32, top.shape, 0) == 0
        return jnp.concatenate([jnp.where(is_first, jnp.zeros_like(top), top), block[BF16_TILE_ROWS:, :]], axis=0)

    def first_matmuls(t):
        st = live[t]
        st["qk"] = [lax.dot_general(mix_ref[rows(t), _cols(M_QR, h, RET_DK)], mix_ref[rows(t), _cols(M_KR, h, RET_DK)],
                                    (((1,), (1,)), ((), ())), preferred_element_type=f32)
                    for h in range(RET_HEADS)]
        st["kv"] = [jnp.dot(kdt_ref[t, _cols(0, h, RET_DK), :], mix_ref[rows(t), _cols(M_VR, h, RET_DV)],
                            preferred_element_type=f32) for h in range(RET_HEADS)]
        if t == 0:
            keys = [jnp.concatenate([kprev_ref[g], mix_ref[0:C, _cols(M_KG, g, SWA_KV)]], axis=0)
                    for g in range(SWA_GROUPS)]
            vals = jnp.concatenate([vprev_ref[...], mix_ref[0:C, M_VS:M_VS + SWA_KV]], axis=0)
            table0 = jnp.where(first_tile_of_seq, SWA_GROUPS, 0)
        else:
            prev_and_current = slice((t - 1) * C, (t + 1) * C)
            keys = [mix_ref[prev_and_current, _cols(M_KG, g, SWA_KV)] for g in range(SWA_GROUPS)]
            vals = mix_ref[prev_and_current, M_VS:M_VS + SWA_KV]
            table0 = 0
        keys = [zero_first_row(k) for k in keys]
        st["vv_ones"] = jnp.concatenate([zero_first_row(vals), ones_cols], axis=1)
        st["s"] = {}
        for j, g in units:
            s = lax.dot_general(mix_ref[rows(t), _cols(M_QS, j, C)], keys[g], (((1,), (1,)), ((), ())),
                                preferred_element_type=f32)
            st["s"][j, g] = s + bias_ref[table0 + g, _cols(0, j, C), :]

    def retention_vector_work(t):
        st = live[t]
        st["a"], st["state_bf16"] = [], []
        for h in range(RET_HEADS):
            s_prev = state_ref[h]
            st["a"].append((st["qk"][h] * dmat_ref[h]).astype(bf16))
            st["state_bf16"].append(s_prev.astype(bf16))
            state_ref[h] = float(np.exp(LOG_GAMMA[h] * C)) * s_prev + st["kv"][h]
        del st["qk"], st["kv"]

    def softmax_numerators(t):
        st = live[t]
        st["p"] = {}
        for u in units:
            s = st["s"][u]
            st["p"][u] = jnp.exp2(s - jnp.max(s, axis=-1, keepdims=True)).astype(bf16)
        del st["s"]

    def second_matmuls(t):
        st = live[t]
        st["o"] = [jnp.dot(jnp.concatenate([st["a"][h], mix_ref[rows(t), _cols(M_QD, h, RET_DK)]], axis=1),
                           jnp.concatenate([mix_ref[rows(t), _cols(M_VR, h, RET_DV)], st["state_bf16"][h]], axis=0),
                           preferred_element_type=f32) for h in range(RET_HEADS)]
        st["r"] = {u: jnp.dot(st["p"][u], st["vv_ones"], preferred_element_type=f32) for u in units}
        del st["a"], st["state_bf16"], st["p"], st["vv_ones"]

    def retention_output(t):
        st = live[t]
        for h in range(RET_HEADS):
            o = st["o"][h]
            mu = jnp.mean(o, axis=-1, keepdims=True)
            d = o - mu
            var = jnp.mean(d * d, axis=-1, keepdims=True)
            mixed_ref[rows(t), _cols(X_RET, h, RET_DV)] = (
                mix_ref[rows(t), _cols(M_SG, h, RET_DV)].astype(f32) * (d * lax.rsqrt(var + EPS))).astype(bf16)
        del st["o"]

    def attention_output(t):
        st = live[t]
        for j in range(SWA_REP):
            num = [st["r"][j, g][:, 0:SWA_KV] for g in range(SWA_GROUPS)]
            den = [st["r"][j, g][:, SWA_KV:2 * SWA_KV] for g in range(SWA_GROUPS)]
            o_pair = jnp.where(first_group, num[0], num[1]) / jnp.where(first_group, den[0], den[1])
            mixed_ref[rows(t), _cols(X_SWA, j, C)] = o_pair.astype(bf16)
        st.clear()

    def carry_last_block():
        last = slice(TOK_TILE - C, TOK_TILE)
        for g in range(SWA_GROUPS):
            kprev_ref[g] = mix_ref[last, _cols(M_KG, g, SWA_KV)]
        vprev_ref[...] = mix_ref[last, M_VS:M_VS + SWA_KV]

    phases = []
    for t in range(CHUNKS_PER_TILE + 1):
        if t < CHUNKS_PER_TILE:
            phases.append(functools.partial(first_matmuls, t))
        if t >= 1:
            phases += [functools.partial(fn, t - 1) for fn in (
                retention_vector_work, softmax_numerators, second_matmuls, retention_output, attention_output)]
    phases.append(carry_last_block)
    return phases


def _store_attention_columns(dst_ref, chunk):
    lanes = 2 * SWA_DH
    first_half = lax.broadcasted_iota(jnp.int32, (chunk.shape[0], lanes), 1) < SWA_DH
    for j in range(SWA_REP):
        a = chunk[:, _cols(0, j // 2, lanes)]
        b = chunk[:, _cols(0, (SWA_REP + j) // 2, lanes)]
        if j % 2 == 0:
            pair = jnp.where(first_half, a, pltpu.roll(b, SWA_DH, 1))
        else:
            pair = jnp.where(first_half, pltpu.roll(a, SWA_DH, 1), b)
        dst_ref[:, _cols(0, j, lanes)] = pair.astype(bf16)
    rest = slice(SWA_Q, SWA_Q + 2 * SWA_KV)
    dst_ref[:, rest] = chunk[:, rest].astype(bf16)


def _load_mixer_weights(w_in_hbm, w_ref, stage_ref, sem_ref):
    jobs = _plain_weight_jobs(w_in_hbm, w_ref.at[:, 0:W_QS])
    for r0 in range(0, D_MODEL, STAGE_ROWS):
        rows = slice(r0, r0 + STAGE_ROWS)
        piece = (w_in_hbm.at[rows, W_QS:D_MIXW], slice(0, STAGE_ROWS), slice(0, D_MIXW - W_QS))
        jobs.append(([piece], functools.partial(_store_attention_columns, w_ref.at[rows, W_QS:D_MIXW])))
    _stream_weight_chunks(jobs, stage_ref, sem_ref)


def _mlp_weight_conversions(s, w_up_ref, w_dn_ref, w_up_bf16_ref, w_dn_bf16_ref):
    sub = lax.rem(s, MLP_WEIGHT_BLOCK_STEPS)
    thunks = []
    for src_ref, dst_ref in ((w_up_ref, w_up_bf16_ref), (w_dn_ref, w_dn_bf16_ref)):
        rows = src_ref.shape[0] // MLP_WEIGHT_BLOCK_STEPS
        row0 = pl.multiple_of(sub * rows, BF16_TILE_ROWS)
        if rows >= MLP_WEIGHT_PARTS * BF16_TILE_ROWS:
            part = rows // MLP_WEIGHT_PARTS
            windows = [(pl.ds(row0 + p * part, part), slice(None)) for p in range(MLP_WEIGHT_PARTS)]
        else:
            part = src_ref.shape[1] // MLP_WEIGHT_PARTS
            windows = [(pl.ds(row0, rows), slice(p * part, (p + 1) * part)) for p in range(MLP_WEIGHT_PARTS)]
        for window in windows:
            def convert(src_ref=src_ref, dst_ref=dst_ref, window=window):
                dst_ref[window] = src_ref[window].astype(bf16)
            thunks.append(convert)
    return thunks


def _proj_mixer_kernel(x_ref, g_ref, w_in_hbm, sink_ref, w_up_ref, w_dn_ref,
                       mixed_ref, w_up_bf16_ref, w_dn_bf16_ref,
                       w_ref, stage_ref, sem_ref,
                       mix_even_ref, mix_odd_ref, kdt_even_ref, kdt_odd_ref, qdec_ref, kdec_ref,
                       state_ref, kprev_ref, vprev_ref, dmat_ref, bias_ref, *, tiles_per_seq, num_tiles):
    s = pl.program_id(0)
    first_tile_of_seq = lax.rem(s + tiles_per_seq - 1, tiles_per_seq) == 0

    @pl.when(s == 0)
    def _first_step():
        _load_mixer_weights(w_in_hbm, w_ref, stage_ref, sem_ref)
        _init_tables(sink_ref, qdec_ref, kdec_ref, dmat_ref, bias_ref)

    @pl.when(first_tile_of_seq)
    def _reset_sequence_state():
        state_ref[...] = jnp.zeros_like(state_ref)
        kprev_ref[...] = jnp.zeros_like(kprev_ref)
        vprev_ref[...] = jnp.zeros_like(vprev_ref)

    even = (mix_even_ref, kdt_even_ref)
    odd = (mix_odd_ref, kdt_odd_ref)

    def step(write, read):
        prepare, pieces = (lambda: None), []
        if write is not None:
            prepare, pieces = _projection_pieces(x_ref, g_ref, w_ref, qdec_ref, kdec_ref, *write)
            conversions = _mlp_weight_conversions(s, w_up_ref, w_dn_ref, w_up_bf16_ref, w_dn_bf16_ref)
            assert 2 * len(conversions) <= len(pieces) + 1
            for n, conversion in enumerate(conversions):
                pieces[2 * n] = functools.partial(lambda piece, conversion: (piece(), conversion()),
                                                  pieces[2 * n], conversion)
        phases = []
        if read is not None:
            phases = _mixer_phases(*read, first_tile_of_seq, mixed_ref,
                                   state_ref, kprev_ref, vprev_ref, dmat_ref, bias_ref)
        prepare()
        for phase in phases[:1]:
            phase()
        emitted = 0
        for n, phase in enumerate(phases[1:], start=1):
            while emitted < len(pieces) and emitted * (len(phases) - 1) < n * len(pieces):
                pieces[emitted]()
                emitted += 1
            phase()
        for piece in pieces[emitted:]:
            piece()

    last = num_tiles
    last_read = odd if (num_tiles - 1) % 2 else even

    @pl.when(s == 0)
    def _first_tile():
        step(even, None)

    @pl.when((s > 0) & (s < last) & (lax.rem(s, 2) == 0))
    def _even_step():
        step(even, odd)

    @pl.when((s > 0) & (s < last) & (lax.rem(s, 2) == 1))
    def _odd_step():
        step(odd, even)

    @pl.when(s == last)
    def _last_tile():
        step(None, last_read)


def _proj_mixer(x2d, gain, w_in, sinks, w_up, w_dn, seq):
    n = x2d.shape[0]
    nt = n // TOK_TILE
    C = CHUNK
    steps = MLP_WEIGHT_BLOCK_STEPS
    assert nt % steps == 0 and D_MODEL % (nt * BF16_TILE_ROWS) == 0
    const = lambda s: (0, 0)
    projected_tile = lambda s: (jnp.minimum(s, nt - 1), 0)
    mixed_tile = lambda s: (jnp.maximum(s - 1, 0), 0)
    weight_block = lambda s: (jnp.minimum(s, nt - 1) // steps, 0)
    up_block = pl.BlockSpec((steps * D_MODEL // nt, D_FF), weight_block)
    dn_block = pl.BlockSpec((steps * D_FF // nt, D_MODEL), weight_block)
    return pl.pallas_call(
        functools.partial(_proj_mixer_kernel, tiles_per_seq=seq // TOK_TILE, num_tiles=nt),
        out_shape=(jax.ShapeDtypeStruct((n, D_MIXED), bf16),
                   jax.ShapeDtypeStruct((D_MODEL, D_FF), bf16),
                   jax.ShapeDtypeStruct((D_FF, D_MODEL), bf16)),
        grid=(nt + 1,),
        in_specs=[
            pl.BlockSpec((TOK_TILE, D_MODEL), projected_tile),
            pl.BlockSpec((1, D_MODEL), const),
            pl.BlockSpec(memory_space=pl.ANY),
            pl.BlockSpec(memory_space=pltpu.SMEM),
            up_block, dn_block,
        ],
        out_specs=(pl.BlockSpec((TOK_TILE, D_MIXED), mixed_tile), up_block, dn_block),
        scratch_shapes=[
            pltpu.VMEM((D_MODEL, D_MIXW), bf16),
            pltpu.VMEM((STAGE_SLOTS, STAGE_ROWS, STAGE_COLS), f32),
            pltpu.SemaphoreType.DMA((STAGE_SLOTS,)),
            pltpu.VMEM((TOK_TILE, D_MIXIN), bf16),
            pltpu.VMEM((TOK_TILE, D_MIXIN), bf16),
            pltpu.VMEM((CHUNKS_PER_TILE, RET_QK, C), bf16),
            pltpu.VMEM((CHUNKS_PER_TILE, RET_QK, C), bf16),
            pltpu.VMEM((TOK_TILE, RET_QK), f32),
            pltpu.VMEM((TOK_TILE, RET_QK), f32),
            pltpu.VMEM((RET_HEADS, RET_DK, RET_DV), f32),
            pltpu.VMEM((SWA_GROUPS, C, SWA_KV), bf16),
            pltpu.VMEM((C, SWA_KV), bf16),
            pltpu.VMEM((RET_HEADS, C, C), f32),
            pltpu.VMEM((2 * SWA_GROUPS, SWA_REP * C, 2 * C), f32),
        ],
        compiler_params=pltpu.CompilerParams(
            dimension_semantics=("arbitrary",), vmem_limit_bytes=VMEM_LIMIT_BYTES),
        name="proj_mixer",
    )(x2d, gain, w_in, sinks, w_up, w_dn)


def _load_post_weights(w_in_hbm, w_ro_hbm, w_so_hbm, w_o_hbm,
                       wg_ref, wro_ref, wso_ref, wo_ref, stage_ref, sem_ref):
    jobs = _plain_weight_jobs(w_in_hbm, wg_ref, col0=D_MIXW)
    jobs += _plain_weight_jobs(w_ro_hbm, wro_ref)
    runs = _swa_out_row_order().reshape(-1, SWA_DH)[:, 0]
    pieces = [(w_so_hbm.at[int(r0):int(r0) + SWA_DH, :], slice(n * SWA_DH, (n + 1) * SWA_DH), slice(0, STAGE_COLS))
              for n, r0 in enumerate(runs)]
    jobs += [(pieces, functools.partial(_store_bf16, wso_ref))]
    jobs += _plain_weight_jobs(w_o_hbm, wo_ref)
    _stream_weight_chunks(jobs, stage_ref, sem_ref)


def _post_kernel(x_ref, mixed_ref, gpre_mix_ref, w_in_hbm, w_ro_hbm, w_so_hbm, w_o_hbm, gpost_mix_ref,
                 gpre_mlp_ref, wup_ref, wdn_ref, gpost_mlp_ref, o_ref,
                 wg_ref, wro_ref, wso_ref, wo_ref, stage_ref, sem_ref):
    tm = x_ref.shape[0]
    ff_starts = list(range(0, D_FF, FF_CHUNK))

    @pl.when(pl.program_id(0) == 0)
    def _first_step():
        _load_post_weights(w_in_hbm, w_ro_hbm, w_so_hbm, w_o_hbm,
                           wg_ref, wro_ref, wso_ref, wo_ref, stage_ref, sem_ref)

    def row_block(rows):
        x = x_ref[rows, :]
        y_r = jnp.dot(mixed_ref[rows, X_RET:X_RET + RET_V], wro_ref[...], preferred_element_type=f32)
        y_s = jnp.dot(mixed_ref[rows, X_SWA:X_SWA + SWA_Q], wso_ref[...], preferred_element_type=f32)
        h = (x * _rms_scale(x) * gpre_mix_ref[...]).astype(bf16)
        gate_r = jnp.dot(h, wg_ref[:, 0:D_MODEL], preferred_element_type=f32)
        gate_s = jnp.dot(h, wg_ref[:, D_MODEL:D_GATES], preferred_element_type=f32)
        yield
        merged = _sigmoid(gate_r) * y_r + _sigmoid(gate_s) * y_s
        z = jnp.dot(merged.astype(bf16), wo_ref[...], preferred_element_type=f32)
        yield
        x1 = x + z * _rms_scale(z) * gpost_mix_ref[...]
        h1 = (x1 * _rms_scale(x1) * gpre_mlp_ref[...]).astype(bf16)

        def up(c0):
            return jnp.dot(h1, wup_ref[:, c0:c0 + FF_CHUNK], preferred_element_type=f32)

        acc = None
        u_next = up(ff_starts[0])
        yield
        for n, c0 in enumerate(ff_starts):
            u = u_next
            if n + 1 < len(ff_starts):
                u_next = up(ff_starts[n + 1])
            u = jnp.maximum(u, 0.0)
            part = jnp.dot((u * u).astype(bf16), wdn_ref[c0:c0 + FF_CHUNK, :], preferred_element_type=f32)
            acc = part if acc is None else acc + part
            yield
        o_ref[rows, :] = x1 + acc * _rms_scale(acc) * gpost_mlp_ref[...]

    active = [row_block(slice(r0, r0 + POST_BLOCK_ROWS)) for r0 in range(0, tm, POST_BLOCK_ROWS)]
    while active:
        for gen in list(active):
            if next(gen, "done") == "done":
                active.remove(gen)


def _post(x2d, mixed, gpre_mix, w_in, w_ro, w_so, w_o, gpost_mix, gpre_mlp, w_up_bf16, w_dn_bf16, gpost_mlp):
    n = x2d.shape[0]
    const = lambda i: (0, 0)
    tile = lambda width: pl.BlockSpec((POST_TILE, width), lambda i: (i, 0))
    in_hbm = pl.BlockSpec(memory_space=pl.ANY)
    gain = pl.BlockSpec((1, D_MODEL), const)
    resident = lambda shape: pl.BlockSpec(shape, const, pipeline_mode=pl.Buffered(1))
    return pl.pallas_call(
        _post_kernel,
        out_shape=jax.ShapeDtypeStruct((n, D_MODEL), f32),
        grid=(n // POST_TILE,),
        in_specs=[
            tile(D_MODEL), tile(D_MIXED),
            gain, in_hbm,
            in_hbm, in_hbm, in_hbm, gain,
            gain, resident((D_MODEL, D_FF)), resident((D_FF, D_MODEL)), gain,
        ],
        out_specs=tile(D_MODEL),
        scratch_shapes=[
            pltpu.VMEM((D_MODEL, D_GATES), bf16),
            pltpu.VMEM((RET_V, D_MODEL), bf16),
            pltpu.VMEM((SWA_Q, D_MODEL), bf16),
            pltpu.VMEM((D_MODEL, D_MODEL), bf16),
            pltpu.VMEM((STAGE_SLOTS, STAGE_ROWS, STAGE_COLS), f32),
            pltpu.SemaphoreType.DMA((STAGE_SLOTS,)),
        ],
        compiler_params=pltpu.CompilerParams(
            dimension_semantics=("arbitrary",), vmem_limit_bytes=VMEM_LIMIT_BYTES),
        name="post",
    )(x2d, mixed, gpre_mix, w_in, w_ro, w_so, w_o, gpost_mix, gpre_mlp, w_up_bf16, w_dn_bf16, gpost_mlp)


def kernel(x, pre_mix_norm, w_in, w_ret_out, w_swa_out, w_out, sinks, post_mix_norm, pre_mlp_norm, w_up,
           w_down, post_mlp_norm):
    batch, seq, d = x.shape
    depth = w_in.shape[0]
    assert d == D_MODEL and seq % TOK_TILE == 0 and seq // TOK_TILE > 1 and (batch * seq) % POST_TILE == 0
    x2d = x.reshape(batch * seq, d)
    for l in range(depth):
        pre_mix_gain = pre_mix_norm[l][None, :]
        mixed, w_up_bf16, w_dn_bf16 = _proj_mixer(x2d, pre_mix_gain, w_in[l], sinks[l].astype(f32),
                                                  w_up[l], w_down[l], seq)
        x2d = _post(x2d, mixed, pre_mix_gain, w_in[l], w_ret_out[l], w_swa_out[l], w_out[l],
                    post_mix_norm[l][None, :], pre_mlp_norm[l][None, :], w_up_bf16, w_dn_bf16,
                    post_mlp_norm[l][None, :])
    return x2d.reshape(batch, seq, d)
```

```python
import functools

import numpy as np
import jax
import jax.numpy as jnp
from jax import lax
from jax.experimental import pallas as pl
from jax.experimental.pallas import tpu as pltpu

D_MODEL = 1024
RET_HEADS = 4
RET_DK = 128
RET_DV = 256
CHUNK = 128
SWA_HEADS = 8
SWA_GROUPS = 2
SWA_REP = SWA_HEADS // SWA_GROUPS
SWA_DH = 64
D_FF = 4 * D_MODEL
EPS = 1e-6

RET_QK = RET_HEADS * RET_DK
RET_V = RET_HEADS * RET_DV
SWA_Q = SWA_HEADS * SWA_DH
SWA_KV = SWA_GROUPS * SWA_DH

W_QR = 0
W_KR = W_QR + RET_QK
W_VR = W_KR + RET_QK
W_GR = W_VR + RET_V
W_QS = W_GR + RET_V
W_KS = W_QS + SWA_Q
W_VS = W_KS + SWA_KV
D_MIXW = W_VS + SWA_KV
D_GATES = 2 * D_MODEL

M_QR = 0
M_KR = M_QR + RET_QK
M_VR = M_KR + RET_QK
M_SG = M_VR + RET_V
M_QD = M_SG + RET_V
M_QS = M_QD + RET_QK
M_KG = M_QS + SWA_Q
M_VS = M_KG + SWA_GROUPS * SWA_KV
D_MIXIN = M_VS + SWA_KV

X_RET = 0
X_SWA = X_RET + RET_V
D_MIXED = X_SWA + SWA_Q

TOK_TILE = 512
CHUNKS_PER_TILE = TOK_TILE // CHUNK
PROJ_PIECE_COLS = 256
MLP_WEIGHT_BLOCK_STEPS = 4
MLP_WEIGHT_PARTS = 4
POST_TILE = 512
POST_BLOCK_ROWS = 256
FF_CHUNK = 1024
STAGE_ROWS, STAGE_COLS = 512, 1024
STAGE_SLOTS = 3
BF16_TILE_ROWS = 16

VMEM_LIMIT_BYTES = 56 * 1024 * 1024

LOG_GAMMA = [float(np.log(1.0 - 2.0 ** (-5.0 - h))) for h in range(RET_HEADS)]
ALIBI_SLOPES = [float(2.0 ** (-8.0 / SWA_HEADS * (h + 1))) for h in range(SWA_HEADS)]
K_SCALE = float(RET_DK ** -0.5)
S_SCALE = float(SWA_DH ** -0.5)
LOG2E = float(np.log2(np.e))

f32 = jnp.float32
bf16 = jnp.bfloat16


def _swa_out_row_order():
    return np.arange(SWA_Q).reshape(SWA_GROUPS, SWA_REP, SWA_DH).transpose(1, 0, 2).reshape(-1)


def _rms_scale(v):
    return lax.rsqrt(jnp.mean(v * v, axis=-1, keepdims=True) + EPS)


def _sigmoid(v):
    return 0.5 * jnp.tanh(0.5 * v) + 0.5


def _cols(offset, index, width):
    return slice(offset + index * width, offset + (index + 1) * width)


def _store_bf16(dst_ref, chunk):
    dst_ref[...] = chunk[0:dst_ref.shape[0], 0:dst_ref.shape[1]].astype(bf16)


def _plain_weight_jobs(src_hbm, dst_ref, col0=0):
    n_rows, n_cols = dst_ref.shape
    jobs = []
    for r0 in range(0, n_rows, STAGE_ROWS):
        for c0 in range(0, n_cols, STAGE_COLS):
            rows, cols = min(STAGE_ROWS, n_rows - r0), min(STAGE_COLS, n_cols - c0)
            piece = (src_hbm.at[r0:r0 + rows, col0 + c0:col0 + c0 + cols], slice(0, rows), slice(0, cols))
            jobs.append(([piece], functools.partial(_store_bf16, dst_ref.at[r0:r0 + rows, c0:c0 + cols])))
    return jobs


def _stream_weight_chunks(jobs, stage_ref, sem_ref):
    slots = stage_ref.shape[0]

    def copies(k):
        slot = k % slots
        return [pltpu.make_async_copy(src, stage_ref.at[slot, rows, cols], sem_ref.at[slot])
                for src, rows, cols in jobs[k][0]]

    for k in range(min(slots - 1, len(jobs))):
        for copy in copies(k):
            copy.start()
    for k, job in enumerate(jobs):
        ahead = k + slots - 1
        if ahead < len(jobs):
            for copy in copies(ahead):
                copy.start()
        for copy in copies(k):
            copy.wait()
        job[1](stage_ref[k % slots])


def _init_tables(sink_ref, qdec_ref, kdec_ref, dmat_ref, bias_ref):
    C = CHUNK
    pos = (lax.broadcasted_iota(jnp.int32, (TOK_TILE, RET_DK), 0) % C).astype(f32)
    i = lax.broadcasted_iota(jnp.int32, (C, C), 0).astype(f32)
    j = lax.broadcasted_iota(jnp.int32, (C, C), 1).astype(f32)
    diff = i - j
    for h in range(RET_HEADS):
        qdec_ref[:, _cols(0, h, RET_DK)] = jnp.exp(LOG_GAMMA[h] * (pos + 1.0))
        kdec_ref[:, _cols(0, h, RET_DK)] = jnp.exp(LOG_GAMMA[h] * (C - 1.0 - pos)) * K_SCALE
        dmat_ref[h] = jnp.where(diff >= 0, jnp.exp(LOG_GAMMA[h] * jnp.maximum(diff, 0.0)), 0.0) * K_SCALE
    kpos = lax.broadcasted_iota(jnp.int32, (2 * C, SWA_REP * C), 0)
    col = lax.broadcasted_iota(jnp.int32, (2 * C, SWA_REP * C), 1)
    rep = col // C
    dist = (col - rep * C + C) - kpos
    valid = (dist >= 0) & (dist < C)
    distf = dist.astype(f32)
    for g in range(SWA_GROUPS):
        slope = jnp.zeros((2 * C, SWA_REP * C), f32)
        sink = jnp.zeros((2 * C, SWA_REP * C), f32)
        for r in range(SWA_REP):
            slope = jnp.where(rep == r, ALIBI_SLOPES[g * SWA_REP + r], slope)
            sink = jnp.where(rep == r, sink_ref[g * SWA_REP + r], sink)
        alibi = -slope * distf
        bias_ref[g] = jnp.where(kpos == 0, sink, jnp.where(valid, alibi, -jnp.inf)) * LOG2E
        bias_ref[SWA_GROUPS + g] = jnp.where(
            kpos == 0, sink, jnp.where(valid & (kpos >= C), alibi, -jnp.inf)) * LOG2E


def _projection_pieces(x_ref, g_ref, w_ref, qdec_ref, kdec_ref, mix_ref, kdt_ref):
    C = CHUNK
    shared = {}

    def prepare():
        x = x_ref[...]
        shared["h"] = (x * _rms_scale(x) * g_ref[...]).astype(bf16)

    def proj(c0, width):
        return jnp.dot(shared["h"], w_ref[:, c0:c0 + width], preferred_element_type=f32)

    width = PROJ_PIECE_COLS

    def swish_gate(part):
        def run():
            g = proj(W_GR + part * width, width)
            mix_ref[:, _cols(M_SG, part, width)] = (g * _sigmoid(g)).astype(bf16)
        return run

    def queries(part):
        def run():
            q = proj(W_QR + part * width, width)
            mix_ref[:, _cols(M_QR, part, width)] = q.astype(bf16)
            mix_ref[:, _cols(M_QD, part, width)] = (q * qdec_ref[:, _cols(0, part, width)]).astype(bf16)
        return run

    def keys(part):
        def run():
            k = proj(W_KR + part * width, width)
            mix_ref[:, _cols(M_KR, part, width)] = k.astype(bf16)
            kd = k * kdec_ref[:, _cols(0, part, width)]
            heads_per_piece = width // RET_DK
            for ci in range(CHUNKS_PER_TILE):
                for hd in range(heads_per_piece):
                    kdt_ref[ci, _cols(0, part * heads_per_piece + hd, RET_DK), :] = (
                        kd[ci * C:(ci + 1) * C, _cols(0, hd, RET_DK)].T.astype(bf16))
        return run

    def attention_queries(part):
        def run():
            mix_ref[:, _cols(M_QS, part, width)] = proj(W_QS + part * width, width).astype(bf16)
        return run

    def attention_keys_values():
        kv_s = proj(W_KS, 2 * SWA_KV)
        ks = kv_s[:, 0:SWA_KV] * (S_SCALE * LOG2E)
        lane = lax.broadcasted_iota(jnp.int32, (TOK_TILE, SWA_KV), 1)
        for grp in range(SWA_GROUPS):
            in_group = (lane >= grp * SWA_DH) & (lane < (grp + 1) * SWA_DH)
            mix_ref[:, _cols(M_KG, grp, SWA_KV)] = jnp.where(in_group, ks, 0.0).astype(bf16)
        mix_ref[:, M_VS:M_VS + SWA_KV] = kv_s[:, SWA_KV:2 * SWA_KV].astype(bf16)

    def values(part):
        def run():
            mix_ref[:, _cols(M_VR, part, width)] = proj(W_VR + part * width, width).astype(bf16)
        return run

    light = [queries(n) for n in range(RET_QK // width)]
    light += [keys(n) for n in range(RET_QK // width)]
    light += [attention_queries(n) for n in range(SWA_Q // width)]
    light += [attention_keys_values]
    light += [values(n) for n in range(RET_V // width)]
    heavy = [swish_gate(n) for n in range(RET_V // width)]
    pieces = []
    per_heavy = -(-len(light) // len(heavy))
    for n, piece in enumerate(heavy):
        pieces.append(piece)
        pieces += light[n * per_heavy:(n + 1) * per_heavy]
    return prepare, pieces


def _mixer_phases(mix_ref, kdt_ref, first_tile_of_seq, mixed_ref,
                  state_ref, kprev_ref, vprev_ref, dmat_ref, bias_ref):
    C = CHUNK
    ones_rows = jnp.ones((BF16_TILE_ROWS, 2 * C), bf16)
    live = [dict() for _ in range(CHUNKS_PER_TILE)]

    def rows(t):
        return slice(t * C, (t + 1) * C)

    def zero_first_row(block):
        top = block[0:BF16_TILE_ROWS, :]
        is_first = lax.broadcasted_iota(jnp.int32, top.shape, 0) == 0
        return jnp.concatenate([jnp.where(is_first, jnp.zeros_like(top), top), block[BF16_TILE_ROWS:, :]], axis=0)

    def first_matmuls(t):
        st = live[t]
        st["qk"] = [lax.dot_general(mix_ref[rows(t), _cols(M_QR, h, RET_DK)], mix_ref[rows(t), _cols(M_KR, h, RET_DK)],
                                    (((1,), (1,)), ((), ())), preferred_element_type=f32)
                    for h in range(RET_HEADS)]
        st["kv"] = [jnp.dot(kdt_ref[t, _cols(0, h, RET_DK), :], mix_ref[rows(t), _cols(M_VR, h, RET_DV)],
                            preferred_element_type=f32) for h in range(RET_HEADS)]
        if t == 0:
            keys = [jnp.concatenate([kprev_ref[g], mix_ref[0:C, _cols(M_KG, g, SWA_KV)]], axis=0)
                    for g in range(SWA_GROUPS)]
            vals = jnp.concatenate([vprev_ref[...], mix_ref[0:C, M_VS:M_VS + SWA_KV]], axis=0)
            table0 = jnp.where(first_tile_of_seq, SWA_GROUPS, 0)
        else:
            prev_and_current = slice((t - 1) * C, (t + 1) * C)
            keys = [mix_ref[prev_and_current, _cols(M_KG, g, SWA_KV)] for g in range(SWA_GROUPS)]
            vals = mix_ref[prev_and_current, M_VS:M_VS + SWA_KV]
            table0 = 0
        keys = [zero_first_row(k) for k in keys]
        vals_t = zero_first_row(vals).astype(f32).T
        st["vt_ones"] = [jnp.concatenate([vals_t[g * SWA_DH:(g + 1) * SWA_DH, :].astype(bf16), ones_rows], axis=0)
                         for g in range(SWA_GROUPS)]
        q_all = jnp.concatenate([mix_ref[rows(t), _cols(M_QS, j, C)] for j in range(SWA_REP)], axis=0)
        st["s"] = [lax.dot_general(keys[g], q_all, (((1,), (1,)), ((), ())), preferred_element_type=f32)
                   + bias_ref[table0 + g] for g in range(SWA_GROUPS)]

    def retention_vector_work(t):
        st = live[t]
        st["a"], st["state_bf16"] = [], []
        for h in range(RET_HEADS):
            s_prev = state_ref[h]
            st["a"].append((st["qk"][h] * dmat_ref[h]).astype(bf16))
            st["state_bf16"].append(s_prev.astype(bf16))
            state_ref[h] = float(np.exp(LOG_GAMMA[h] * C)) * s_prev + st["kv"][h]
        del st["qk"], st["kv"]

    def softmax_numerators(t):
        st = live[t]
        st["p"] = [jnp.exp2(s - jnp.max(s, axis=0, keepdims=True)).astype(bf16) for s in st["s"]]
        del st["s"]

    def second_matmuls(t):
        st = live[t]
        st["o"] = [jnp.dot(jnp.concatenate([st["a"][h], mix_ref[rows(t), _cols(M_QD, h, RET_DK)]], axis=1),
                           jnp.concatenate([mix_ref[rows(t), _cols(M_VR, h, RET_DV)], st["state_bf16"][h]], axis=0),
                           preferred_element_type=f32) for h in range(RET_HEADS)]
        st["r"] = [jnp.dot(st["vt_ones"][g], st["p"][g], preferred_element_type=f32) for g in range(SWA_GROUPS)]
        del st["a"], st["state_bf16"], st["p"], st["vt_ones"]

    def retention_output(t):
        st = live[t]
        for h in range(RET_HEADS):
            o = st["o"][h]
            mu = jnp.mean(o, axis=-1, keepdims=True)
            d = o - mu
            var = jnp.mean(d * d, axis=-1, keepdims=True)
            mixed_ref[rows(t), _cols(X_RET, h, RET_DV)] = (
                mix_ref[rows(t), _cols(M_SG, h, RET_DV)].astype(f32) * (d * lax.rsqrt(var + EPS))).astype(bf16)
        del st["o"]

    def attention_output(t):
        st = live[t]
        o_t = [r[0:SWA_DH, :] / r[SWA_DH:SWA_DH + 1, :] for r in st["r"]]
        for j in range(SWA_REP):
            pair_t = jnp.concatenate([o[:, _cols(0, j, C)] for o in o_t], axis=0)
            mixed_ref[rows(t), _cols(X_SWA, j, C)] = pair_t.T.astype(bf16)
        st.clear()

    def carry_last_block():
        last = slice(TOK_TILE - C, TOK_TILE)
        for g in range(SWA_GROUPS):
            kprev_ref[g] = mix_ref[last, _cols(M_KG, g, SWA_KV)]
        vprev_ref[...] = mix_ref[last, M_VS:M_VS + SWA_KV]

    phases = []
    for t in range(CHUNKS_PER_TILE + 1):
        if t < CHUNKS_PER_TILE:
            phases.append(functools.partial(first_matmuls, t))
        if t >= 1:
            phases += [functools.partial(fn, t - 1) for fn in (
                retention_vector_work, softmax_numerators, second_matmuls, retention_output, attention_output)]
    phases.append(carry_last_block)
    return phases


def _store_attention_columns(dst_ref, chunk):
    lanes = 2 * SWA_DH
    first_half = lax.broadcasted_iota(jnp.int32, (chunk.shape[0], lanes), 1) < SWA_DH
    for j in range(SWA_REP):
        a = chunk[:, _cols(0, j // 2, lanes)]
        b = chunk[:, _cols(0, (SWA_REP + j) // 2, lanes)]
        if j % 2 == 0:
            pair = jnp.where(first_half, a, pltpu.roll(b, SWA_DH, 1))
        else:
            pair = jnp.where(first_half, pltpu.roll(a, SWA_DH, 1), b)
        dst_ref[:, _cols(0, j, lanes)] = pair.astype(bf16)
    rest = slice(SWA_Q, SWA_Q + 2 * SWA_KV)
    dst_ref[:, rest] = chunk[:, rest].astype(bf16)


def _load_mixer_weights(w_in_hbm, w_ref, stage_ref, sem_ref):
    jobs = _plain_weight_jobs(w_in_hbm, w_ref.at[:, 0:W_QS])
    for r0 in range(0, D_MODEL, STAGE_ROWS):
        rows = slice(r0, r0 + STAGE_ROWS)
        piece = (w_in_hbm.at[rows, W_QS:D_MIXW], slice(0, STAGE_ROWS), slice(0, D_MIXW - W_QS))
        jobs.append(([piece], functools.partial(_store_attention_columns, w_ref.at[rows, W_QS:D_MIXW])))
    _stream_weight_chunks(jobs, stage_ref, sem_ref)


def _mlp_weight_conversions(s, w_up_ref, w_dn_ref, w_up_bf16_ref, w_dn_bf16_ref):
    sub = lax.rem(s, MLP_WEIGHT_BLOCK_STEPS)
    thunks = []
    for src_ref, dst_ref in ((w_up_ref, w_up_bf16_ref), (w_dn_ref, w_dn_bf16_ref)):
        rows = src_ref.shape[0] // MLP_WEIGHT_BLOCK_STEPS
        row0 = pl.multiple_of(sub * rows, BF16_TILE_ROWS)
        if rows >= MLP_WEIGHT_PARTS * BF16_TILE_ROWS:
            part = rows // MLP_WEIGHT_PARTS
            windows = [(pl.ds(row0 + p * part, part), slice(None)) for p in range(MLP_WEIGHT_PARTS)]
        else:
            part = src_ref.shape[1] // MLP_WEIGHT_PARTS
            windows = [(pl.ds(row0, rows), slice(p * part, (p + 1) * part)) for p in range(MLP_WEIGHT_PARTS)]
        for window in windows:
            def convert(src_ref=src_ref, dst_ref=dst_ref, window=window):
                dst_ref[window] = src_ref[window].astype(bf16)
            thunks.append(convert)
    return thunks


def _proj_mixer_kernel(x_ref, g_ref, w_in_hbm, sink_ref, w_up_ref, w_dn_ref,
                       mixed_ref, w_up_bf16_ref, w_dn_bf16_ref,
                       w_ref, stage_ref, sem_ref,
                       mix_even_ref, mix_odd_ref, kdt_even_ref, kdt_odd_ref, qdec_ref, kdec_ref,
                       state_ref, kprev_ref, vprev_ref, dmat_ref, bias_ref, *, tiles_per_seq, num_tiles):
    s = pl.program_id(0)
    first_tile_of_seq = lax.rem(s + tiles_per_seq - 1, tiles_per_seq) == 0

    @pl.when(s == 0)
    def _first_step():
        _load_mixer_weights(w_in_hbm, w_ref, stage_ref, sem_ref)
        _init_tables(sink_ref, qdec_ref, kdec_ref, dmat_ref, bias_ref)

    @pl.when(first_tile_of_seq)
    def _reset_sequence_state():
        state_ref[...] = jnp.zeros_like(state_ref)
        kprev_ref[...] = jnp.zeros_like(kprev_ref)
        vprev_ref[...] = jnp.zeros_like(vprev_ref)

    even = (mix_even_ref, kdt_even_ref)
    odd = (mix_odd_ref, kdt_odd_ref)

    def step(write, read):
        prepare, pieces = (lambda: None), []
        if write is not None:
            prepare, pieces = _projection_pieces(x_ref, g_ref, w_ref, qdec_ref, kdec_ref, *write)
            conversions = _mlp_weight_conversions(s, w_up_ref, w_dn_ref, w_up_bf16_ref, w_dn_bf16_ref)
            assert 2 * len(conversions) <= len(pieces) + 1
            for n, conversion in enumerate(conversions):
                pieces[2 * n] = functools.partial(lambda piece, conversion: (piece(), conversion()),
                                                  pieces[2 * n], conversion)
        phases = []
        if read is not None:
            phases = _mixer_phases(*read, first_tile_of_seq, mixed_ref,
                                   state_ref, kprev_ref, vprev_ref, dmat_ref, bias_ref)
        prepare()
        for phase in phases[:1]:
            phase()
        emitted = 0
        for n, phase in enumerate(phases[1:], start=1):
            while emitted < len(pieces) and emitted * (len(phases) - 1) < n * len(pieces):
                pieces[emitted]()
                emitted += 1
            phase()
        for piece in pieces[emitted:]:
            piece()

    last = num_tiles
    last_read = odd if (num_tiles - 1) % 2 else even

    @pl.when(s == 0)
    def _first_tile():
        step(even, None)

    @pl.when((s > 0) & (s < last) & (lax.rem(s, 2) == 0))
    def _even_step():
        step(even, odd)

    @pl.when((s > 0) & (s < last) & (lax.rem(s, 2) == 1))
    def _odd_step():
        step(odd, even)

    @pl.when(s == last)
    def _last_tile():
        step(None, last_read)


def _proj_mixer(x2d, gain, w_in, sinks, w_up, w_dn, seq):
    n = x2d.shape[0]
    nt = n // TOK_TILE
    C = CHUNK
    steps = MLP_WEIGHT_BLOCK_STEPS
    assert nt % steps == 0 and D_MODEL % (nt * BF16_TILE_ROWS) == 0
    const = lambda s: (0, 0)
    projected_tile = lambda s: (jnp.minimum(s, nt - 1), 0)
    mixed_tile = lambda s: (jnp.maximum(s - 1, 0), 0)
    weight_block = lambda s: (jnp.minimum(s, nt - 1) // steps, 0)
    up_block = pl.BlockSpec((steps * D_MODEL // nt, D_FF), weight_block)
    dn_block = pl.BlockSpec((steps * D_FF // nt, D_MODEL), weight_block)
    return pl.pallas_call(
        functools.partial(_proj_mixer_kernel, tiles_per_seq=seq // TOK_TILE, num_tiles=nt),
        out_shape=(jax.ShapeDtypeStruct((n, D_MIXED), bf16),
                   jax.ShapeDtypeStruct((D_MODEL, D_FF), bf16),
                   jax.ShapeDtypeStruct((D_FF, D_MODEL), bf16)),
        grid=(nt + 1,),
        in_specs=[
            pl.BlockSpec((TOK_TILE, D_MODEL), projected_tile),
            pl.BlockSpec((1, D_MODEL), const),
            pl.BlockSpec(memory_space=pl.ANY),
            pl.BlockSpec(memory_space=pltpu.SMEM),
            up_block, dn_block,
        ],
        out_specs=(pl.BlockSpec((TOK_TILE, D_MIXED), mixed_tile), up_block, dn_block),
        scratch_shapes=[
            pltpu.VMEM((D_MODEL, D_MIXW), bf16),
            pltpu.VMEM((STAGE_SLOTS, STAGE_ROWS, STAGE_COLS), f32),
            pltpu.SemaphoreType.DMA((STAGE_SLOTS,)),
            pltpu.VMEM((TOK_TILE, D_MIXIN), bf16),
            pltpu.VMEM((TOK_TILE, D_MIXIN), bf16),
            pltpu.VMEM((CHUNKS_PER_TILE, RET_QK, C), bf16),
            pltpu.VMEM((CHUNKS_PER_TILE, RET_QK, C), bf16),
            pltpu.VMEM((TOK_TILE, RET_QK), f32),
            pltpu.VMEM((TOK_TILE, RET_QK), f32),
            pltpu.VMEM((RET_HEADS, RET_DK, RET_DV), f32),
            pltpu.VMEM((SWA_GROUPS, C, SWA_KV), bf16),
            pltpu.VMEM((C, SWA_KV), bf16),
            pltpu.VMEM((RET_HEADS, C, C), f32),
            pltpu.VMEM((2 * SWA_GROUPS, 2 * C, SWA_REP * C), f32),
        ],
        compiler_params=pltpu.CompilerParams(
            dimension_semantics=("arbitrary",), vmem_limit_bytes=VMEM_LIMIT_BYTES),
        name="proj_mixer",
    )(x2d, gain, w_in, sinks, w_up, w_dn)


def _load_post_weights(w_in_hbm, w_ro_hbm, w_so_hbm, w_o_hbm,
                       wg_ref, wro_ref, wso_ref, wo_ref, stage_ref, sem_ref):
    jobs = _plain_weight_jobs(w_in_hbm, wg_ref, col0=D_MIXW)
    jobs += _plain_weight_jobs(w_ro_hbm, wro_ref)
    runs = _swa_out_row_order().reshape(-1, SWA_DH)[:, 0]
    pieces = [(w_so_hbm.at[int(r0):int(r0) + SWA_DH, :], slice(n * SWA_DH, (n + 1) * SWA_DH), slice(0, STAGE_COLS))
              for n, r0 in enumerate(runs)]
    jobs += [(pieces, functools.partial(_store_bf16, wso_ref))]
    jobs += _plain_weight_jobs(w_o_hbm, wo_ref)
    _stream_weight_chunks(jobs, stage_ref, sem_ref)


def _post_kernel(x_ref, mixed_ref, gpre_mix_ref, w_in_hbm, w_ro_hbm, w_so_hbm, w_o_hbm, gpost_mix_ref,
                 gpre_mlp_ref, wup_ref, wdn_ref, gpost_mlp_ref, o_ref,
                 wg_ref, wro_ref, wso_ref, wo_ref, stage_ref, sem_ref):
    tm = x_ref.shape[0]
    ff_starts = list(range(0, D_FF, FF_CHUNK))

    @pl.when(pl.program_id(0) == 0)
    def _first_step():
        _load_post_weights(w_in_hbm, w_ro_hbm, w_so_hbm, w_o_hbm,
                           wg_ref, wro_ref, wso_ref, wo_ref, stage_ref, sem_ref)

    def row_block(rows):
        x = x_ref[rows, :]
        y_r = jnp.dot(mixed_ref[rows, X_RET:X_RET + RET_V], wro_ref[...], preferred_element_type=f32)
        y_s = jnp.dot(mixed_ref[rows, X_SWA:X_SWA + SWA_Q], wso_ref[...], preferred_element_type=f32)
        h = (x * _rms_scale(x) * gpre_mix_ref[...]).astype(bf16)
        gate_r = jnp.dot(h, wg_ref[:, 0:D_MODEL], preferred_element_type=f32)
        gate_s = jnp.dot(h, wg_ref[:, D_MODEL:D_GATES], preferred_element_type=f32)
        yield
        merged = _sigmoid(gate_r) * y_r + _sigmoid(gate_s) * y_s
        z = jnp.dot(merged.astype(bf16), wo_ref[...], preferred_element_type=f32)
        yield
        x1 = x + z * _rms_scale(z) * gpost_mix_ref[...]
        h1 = (x1 * _rms_scale(x1) * gpre_mlp_ref[...]).astype(bf16)

        def up(c0):
            return jnp.dot(h1, wup_ref[:, c0:c0 + FF_CHUNK], preferred_element_type=f32)

        acc = None
        u_next = up(ff_starts[0])
        yield
        for n, c0 in enumerate(ff_starts):
            u = u_next
            if n + 1 < len(ff_starts):
                u_next = up(ff_starts[n + 1])
            u = jnp.maximum(u, 0.0)
            part = jnp.dot((u * u).astype(bf16), wdn_ref[c0:c0 + FF_CHUNK, :], preferred_element_type=f32)
            acc = part if acc is None else acc + part
            yield
        o_ref[rows, :] = x1 + acc * _rms_scale(acc) * gpost_mlp_ref[...]

    active = [row_block(slice(r0, r0 + POST_BLOCK_ROWS)) for r0 in range(0, tm, POST_BLOCK_ROWS)]
    while active:
        for gen in list(active):
            if next(gen, "done") == "done":
                active.remove(gen)


def _post(x2d, mixed, gpre_mix, w_in, w_ro, w_so, w_o, gpost_mix, gpre_mlp, w_up_bf16, w_dn_bf16, gpost_mlp):
    n = x2d.shape[0]
    const = lambda i: (0, 0)
    tile = lambda width: pl.BlockSpec((POST_TILE, width), lambda i: (i, 0))
    in_hbm = pl.BlockSpec(memory_space=pl.ANY)
    gain = pl.BlockSpec((1, D_MODEL), const)
    resident = lambda shape: pl.BlockSpec(shape, const, pipeline_mode=pl.Buffered(1))
    return pl.pallas_call(
        _post_kernel,
        out_shape=jax.ShapeDtypeStruct((n, D_MODEL), f32),
        grid=(n // POST_TILE,),
        in_specs=[
            tile(D_MODEL), tile(D_MIXED),
            gain, in_hbm,
            in_hbm, in_hbm, in_hbm, gain,
            gain, resident((D_MODEL, D_FF)), resident((D_FF, D_MODEL)), gain,
        ],
        out_specs=tile(D_MODEL),
        scratch_shapes=[
            pltpu.VMEM((D_MODEL, D_GATES), bf16),
            pltpu.VMEM((RET_V, D_MODEL), bf16),
            pltpu.VMEM((SWA_Q, D_MODEL), bf16),
            pltpu.VMEM((D_MODEL, D_MODEL), bf16),
            pltpu.VMEM((STAGE_SLOTS, STAGE_ROWS, STAGE_COLS), f32),
            pltpu.SemaphoreType.DMA((STAGE_SLOTS,)),
        ],
        compiler_params=pltpu.CompilerParams(
            dimension_semantics=("arbitrary",), vmem_limit_bytes=VMEM_LIMIT_BYTES),
        name="post",
    )(x2d, mixed, gpre_mix, w_in, w_ro, w_so, w_o, gpost_mix, gpre_mlp, w_up_bf16, w_dn_bf16, gpost_mlp)


def kernel(x, pre_mix_norm, w_in, w_ret_out, w_swa_out, w_out, sinks, post_mix_norm, pre_mlp_norm, w_up,
           w_down, post_mlp_norm):
    batch, seq, d = x.shape
    depth = w_in.shape[0]
    assert d == D_MODEL and seq % TOK_TILE == 0 and seq // TOK_TILE > 1 and (batch * seq) % POST_TILE == 0
    x2d = x.reshape(batch * seq, d)
    for l in range(depth):
        pre_mix_gain = pre_mix_norm[l][None, :]
        mixed, w_up_bf16, w_dn_bf16 = _proj_mixer(x2d, pre_mix_gain, w_in[l], sinks[l].astype(f32),
                                                  w_up[l], w_down[l], seq)
        x2d = _post(x2d, mixed, pre_mix_gain, w_in[l], w_ret_out[l], w_swa_out[l], w_out[l],
                    post_mix_norm[l][None, :], pre_mlp_norm[l][None, :], w_up_bf16, w_dn_bf16,
                    post_mlp_norm[l][None, :])
    return x2d.reshape(batch, seq, d)
```

```python
import functools

import numpy as np
import jax
import jax.numpy as jnp
from jax import lax
from jax.experimental import pallas as pl
from jax.experimental.pallas import tpu as pltpu

D_MODEL = 1024
RET_HEADS = 4
RET_DK = 128
RET_DV = 256
CHUNK = 128
SWA_HEADS = 8
SWA_GROUPS = 2
SWA_REP = SWA_HEADS // SWA_GROUPS
SWA_DH = 64
D_FF = 4 * D_MODEL
EPS = 1e-6

RET_QK = RET_HEADS * RET_DK
RET_V = RET_HEADS * RET_DV
SWA_Q = SWA_HEADS * SWA_DH
SWA_KV = SWA_GROUPS * SWA_DH

W_QR = 0
W_KR = W_QR + RET_QK
W_VR = W_KR + RET_QK
W_GR = W_VR + RET_V
W_QS = W_GR + RET_V
W_KS = W_QS + SWA_Q
W_VS = W_KS + SWA_KV
D_MIXW = W_VS + SWA_KV
D_GATES = 2 * D_MODEL

M_QR = 0
M_KR = M_QR + RET_QK
M_VR = M_KR + RET_QK
M_SG = M_VR + RET_V
M_QD = M_SG + RET_V
M_QS = M_QD + RET_QK
M_KG = M_QS + SWA_Q
M_VS = M_KG + SWA_GROUPS * SWA_KV
D_MIXIN = M_VS + SWA_KV

X_RET = 0
X_SWA = X_RET + RET_V
D_MIXED = X_SWA + SWA_Q

TOK_TILE = 512
CHUNKS_PER_TILE = TOK_TILE // CHUNK
PROJ_PIECE_COLS = 256
MLP_WEIGHT_BLOCK_STEPS = 4
MLP_WEIGHT_PARTS = 4
POST_TILE = 512
POST_BLOCK_ROWS = 256
FF_CHUNK = 1024
STAGE_ROWS, STAGE_COLS = 512, 1024
STAGE_SLOTS = 3
BF16_TILE_ROWS = 16

VMEM_LIMIT_BYTES = 56 * 1024 * 1024

LOG_GAMMA = [float(np.log(1.0 - 2.0 ** (-5.0 - h))) for h in range(RET_HEADS)]
ALIBI_SLOPES = [float(2.0 ** (-8.0 / SWA_HEADS * (h + 1))) for h in range(SWA_HEADS)]
K_SCALE = float(RET_DK ** -0.5)
S_SCALE = float(SWA_DH ** -0.5)
LOG2E = float(np.log2(np.e))

f32 = jnp.float32
bf16 = jnp.bfloat16


def _swa_out_row_order():
    return np.arange(SWA_Q).reshape(SWA_GROUPS, SWA_REP, SWA_DH).transpose(1, 0, 2).reshape(-1)


def _rms_scale(v):
    return lax.rsqrt(jnp.mean(v * v, axis=-1, keepdims=True) + EPS)


def _sigmoid(v):
    return 0.5 * jnp.tanh(0.5 * v) + 0.5


def _cols(offset, index, width):
    return slice(offset + index * width, offset + (index + 1) * width)


def _store_bf16(dst_ref, chunk):
    dst_ref[...] = chunk[0:dst_ref.shape[0], 0:dst_ref.shape[1]].astype(bf16)


def _plain_weight_jobs(src_hbm, dst_ref, col0=0):
    n_rows, n_cols = dst_ref.shape
    jobs = []
    for r0 in range(0, n_rows, STAGE_ROWS):
        for c0 in range(0, n_cols, STAGE_COLS):
            rows, cols = min(STAGE_ROWS, n_rows - r0), min(STAGE_COLS, n_cols - c0)
            piece = (src_hbm.at[r0:r0 + rows, col0 + c0:col0 + c0 + cols], slice(0, rows), slice(0, cols))
            jobs.append(([piece], functools.partial(_store_bf16, dst_ref.at[r0:r0 + rows, c0:c0 + cols])))
    return jobs


def _stream_weight_chunks(jobs, stage_ref, sem_ref):
    slots = stage_ref.shape[0]

    def copies(k):
        slot = k % slots
        return [pltpu.make_async_copy(src, stage_ref.at[slot, rows, cols], sem_ref.at[slot])
                for src, rows, cols in jobs[k][0]]

    for k in range(min(slots - 1, len(jobs))):
        for copy in copies(k):
            copy.start()
    for k, job in enumerate(jobs):
        ahead = k + slots - 1
        if ahead < len(jobs):
            for copy in copies(ahead):
                copy.start()
        for copy in copies(k):
            copy.wait()
        job[1](stage_ref[k % slots])


def _init_tables(sink_ref, qdec_ref, kdec_ref, dmat_ref, bias_ref):
    C = CHUNK
    pos = (lax.broadcasted_iota(jnp.int32, (TOK_TILE, RET_DK), 0) % C).astype(f32)
    i = lax.broadcasted_iota(jnp.int32, (C, C), 0).astype(f32)
    j = lax.broadcasted_iota(jnp.int32, (C, C), 1).astype(f32)
    diff = i - j
    for h in range(RET_HEADS):
        qdec_ref[:, _cols(0, h, RET_DK)] = jnp.exp(LOG_GAMMA[h] * (pos + 1.0))
        kdec_ref[:, _cols(0, h, RET_DK)] = jnp.exp(LOG_GAMMA[h] * (C - 1.0 - pos)) * K_SCALE
        dmat_ref[h] = jnp.where(diff >= 0, jnp.exp(LOG_GAMMA[h] * jnp.maximum(diff, 0.0)), 0.0) * K_SCALE
    kpos = lax.broadcasted_iota(jnp.int32, (2 * C, SWA_REP * C), 0)
    col = lax.broadcasted_iota(jnp.int32, (2 * C, SWA_REP * C), 1)
    rep = col // C
    dist = (col - rep * C + C) - kpos
    valid = (dist >= 0) & (dist < C)
    distf = dist.astype(f32)
    for g in range(SWA_GROUPS):
        slope = jnp.zeros((2 * C, SWA_REP * C), f32)
        sink = jnp.zeros((2 * C, SWA_REP * C), f32)
        for r in range(SWA_REP):
            slope = jnp.where(rep == r, ALIBI_SLOPES[g * SWA_REP + r], slope)
            sink = jnp.where(rep == r, sink_ref[g * SWA_REP + r], sink)
        alibi = -slope * distf
        bias_ref[g] = jnp.where(kpos == 0, sink, jnp.where(valid, alibi, -jnp.inf)) * LOG2E
        bias_ref[SWA_GROUPS + g] = jnp.where(
            kpos == 0, sink, jnp.where(valid & (kpos >= C), alibi, -jnp.inf)) * LOG2E


def _projection_pieces(x_ref, g_ref, w_ref, qdec_ref, kdec_ref, mix_ref, kdt_ref):
    C = CHUNK
    shared = {}

    def prepare():
        x = x_ref[...]
        shared["h"] = (x * _rms_scale(x) * g_ref[...]).astype(bf16)

    def proj(c0, width):
        return jnp.dot(shared["h"], w_ref[:, c0:c0 + width], preferred_element_type=f32)

    width = PROJ_PIECE_COLS

    def swish_gate(part):
        def run():
            g = proj(W_GR + part * width, width)
            mix_ref[:, _cols(M_SG, part, width)] = (g * _sigmoid(g)).astype(bf16)
        return run

    def queries(part):
        def run():
            q = proj(W_QR + part * width, width)
            mix_ref[:, _cols(M_QR, part, width)] = q.astype(bf16)
            mix_ref[:, _cols(M_QD, part, width)] = (q * qdec_ref[:, _cols(0, part, width)]).astype(bf16)
        return run

    def keys(part):
        def run():
            k = proj(W_KR + part * width, width)
            mix_ref[:, _cols(M_KR, part, width)] = k.astype(bf16)
            kd = k * kdec_ref[:, _cols(0, part, width)]
            heads_per_piece = width // RET_DK
            for ci in range(CHUNKS_PER_TILE):
                for hd in range(heads_per_piece):
                    kdt_ref[ci, _cols(0, part * heads_per_piece + hd, RET_DK), :] = (
                        kd[ci * C:(ci + 1) * C, _cols(0, hd, RET_DK)].T.astype(bf16))
        return run

    def attention_queries(part):
        def run():
            mix_ref[:, _cols(M_QS, part, width)] = proj(W_QS + part * width, width).astype(bf16)
        return run

    def attention_keys_values():
        kv_s = proj(W_KS, 2 * SWA_KV)
        ks = kv_s[:, 0:SWA_KV] * (S_SCALE * LOG2E)
        lane = lax.broadcasted_iota(jnp.int32, (TOK_TILE, SWA_KV), 1)
        for grp in range(SWA_GROUPS):
            in_group = (lane >= grp * SWA_DH) & (lane < (grp + 1) * SWA_DH)
            mix_ref[:, _cols(M_KG, grp, SWA_KV)] = jnp.where(in_group, ks, 0.0).astype(bf16)
        mix_ref[:, M_VS:M_VS + SWA_KV] = kv_s[:, SWA_KV:2 * SWA_KV].astype(bf16)

    def values(part):
        def run():
            mix_ref[:, _cols(M_VR, part, width)] = proj(W_VR + part * width, width).astype(bf16)
        return run

    light = [queries(n) for n in range(RET_QK // width)]
    light += [keys(n) for n in range(RET_QK // width)]
    light += [attention_queries(n) for n in range(SWA_Q // width)]
    light += [attention_keys_values]
    light += [values(n) for n in range(RET_V // width)]
    heavy = [swish_gate(n) for n in range(RET_V // width)]
    pieces = []
    per_heavy = -(-len(light) // len(heavy))
    for n, piece in enumerate(heavy):
        pieces.append(piece)
        pieces += light[n * per_heavy:(n + 1) * per_heavy]
    return prepare, pieces


def _mixer_phases(mix_ref, kdt_ref, first_tile_of_seq, mixed_ref,
                  state_ref, kprev_ref, vprev_ref, dmat_ref, bias_ref):
    C = CHUNK
    ones_rows = jnp.ones((BF16_TILE_ROWS, 2 * C), bf16)
    live = [dict() for _ in range(CHUNKS_PER_TILE)]

    def rows(t):
        return slice(t * C, (t + 1) * C)

    def zero_first_row(block):
        top = block[0:BF16_TILE_ROWS, :]
        is_first = lax.broadcasted_iota(jnp.int32, top.shape, 0) == 0
        return jnp.concatenate([jnp.where(is_first, jnp.zeros_like(top), top), block[BF16_TILE_ROWS:, :]], axis=0)

    def first_matmuls(t):
        st = live[t]
        st["qk"] = []
        first_head = lax.broadcasted_iota(jnp.int32, (C, 2 * RET_DK), 1) < RET_DK
        for pair in range(RET_HEADS // 2):
            k_pair = mix_ref[rows(t), _cols(M_KR, pair, 2 * RET_DK)]
            k_rows = jnp.concatenate([jnp.where(first_head, k_pair, jnp.zeros_like(k_pair)),
                                      jnp.where(first_head, jnp.zeros_like(k_pair), k_pair)], axis=0)
            qk = lax.dot_general(mix_ref[rows(t), _cols(M_QR, pair, 2 * RET_DK)], k_rows,
                                 (((1,), (1,)), ((), ())), preferred_element_type=f32)
            st["qk"] += [qk[:, 0:C], qk[:, C:2 * C]]
        st["kv"] = [jnp.dot(kdt_ref[t, _cols(0, h, RET_DK), :], mix_ref[rows(t), _cols(M_VR, h, RET_DV)],
                            preferred_element_type=f32) for h in range(RET_HEADS)]
        if t == 0:
            keys = [jnp.concatenate([kprev_ref[g], mix_ref[0:C, _cols(M_KG, g, SWA_KV)]], axis=0)
                    for g in range(SWA_GROUPS)]
            vals = jnp.concatenate([vprev_ref[...], mix_ref[0:C, M_VS:M_VS + SWA_KV]], axis=0)
            table0 = jnp.where(first_tile_of_seq, SWA_GROUPS, 0)
        else:
            prev_and_current = slice((t - 1) * C, (t + 1) * C)
            keys = [mix_ref[prev_and_current, _cols(M_KG, g, SWA_KV)] for g in range(SWA_GROUPS)]
            vals = mix_ref[prev_and_current, M_VS:M_VS + SWA_KV]
            table0 = 0
        keys = [zero_first_row(k) for k in keys]
        vals_t = zero_first_row(vals).astype(f32).T
        st["vt_ones"] = [jnp.concatenate([vals_t[g * SWA_DH:(g + 1) * SWA_DH, :].astype(bf16), ones_rows], axis=0)
                         for g in range(SWA_GROUPS)]
        q_all = jnp.concatenate([mix_ref[rows(t), _cols(M_QS, j, C)] for j in range(SWA_REP)], axis=0)
        st["s"] = [lax.dot_general(keys[g], q_all, (((1,), (1,)), ((), ())), preferred_element_type=f32)
                   + bias_ref[table0 + g] for g in range(SWA_GROUPS)]

    def retention_vector_work(t):
        st = live[t]
        st["a"], st["state_bf16"] = [], []
        for h in range(RET_HEADS):
            s_prev = state_ref[h]
            st["a"].append((st["qk"][h] * dmat_ref[h]).astype(bf16))
            st["state_bf16"].append(s_prev.astype(bf16))
            state_ref[h] = float(np.exp(LOG_GAMMA[h] * C)) * s_prev + st["kv"][h]
        del st["qk"], st["kv"]

    def softmax_numerators(t):
        st = live[t]
        st["p"] = [jnp.exp2(s - jnp.max(s, axis=0, keepdims=True)).astype(bf16) for s in st["s"]]
        del st["s"]

    def second_matmuls(t):
        st = live[t]
        st["o"] = [jnp.dot(jnp.concatenate([st["a"][h], mix_ref[rows(t), _cols(M_QD, h, RET_DK)]], axis=1),
                           jnp.concatenate([mix_ref[rows(t), _cols(M_VR, h, RET_DV)], st["state_bf16"][h]], axis=0),
                           preferred_element_type=f32) for h in range(RET_HEADS)]
        st["r"] = [jnp.dot(st["vt_ones"][g], st["p"][g], preferred_element_type=f32) for g in range(SWA_GROUPS)]
        del st["a"], st["state_bf16"], st["p"], st["vt_ones"]

    def retention_output(t):
        st = live[t]
        for h in range(RET_HEADS):
            o = st["o"][h]
            mu = jnp.mean(o, axis=-1, keepdims=True)
            d = o - mu
            var = jnp.mean(d * d, axis=-1, keepdims=True)
            mixed_ref[rows(t), _cols(X_RET, h, RET_DV)] = (
                mix_ref[rows(t), _cols(M_SG, h, RET_DV)].astype(f32) * (d * lax.rsqrt(var + EPS))).astype(bf16)
        del st["o"]

    def attention_output(t):
        st = live[t]
        o_t = [r[0:SWA_DH, :] / r[SWA_DH:SWA_DH + 1, :] for r in st["r"]]
        for j in range(SWA_REP):
            pair_t = jnp.concatenate([o[:, _cols(0, j, C)] for o in o_t], axis=0)
            mixed_ref[rows(t), _cols(X_SWA, j, C)] = pair_t.T.astype(bf16)
        st.clear()

    def carry_last_block():
        last = slice(TOK_TILE - C, TOK_TILE)
        for g in range(SWA_GROUPS):
            kprev_ref[g] = mix_ref[last, _cols(M_KG, g, SWA_KV)]
        vprev_ref[...] = mix_ref[last, M_VS:M_VS + SWA_KV]

    phases = []
    for t in range(CHUNKS_PER_TILE + 1):
        if t < CHUNKS_PER_TILE:
            phases.append(functools.partial(first_matmuls, t))
        if t >= 1:
            phases += [functools.partial(fn, t - 1) for fn in (
                retention_vector_work, softmax_numerators, second_matmuls, retention_output, attention_output)]
    phases.append(carry_last_block)
    return phases


def _store_attention_columns(dst_ref, chunk):
    lanes = 2 * SWA_DH
    first_half = lax.broadcasted_iota(jnp.int32, (chunk.shape[0], lanes), 1) < SWA_DH
    for j in range(SWA_REP):
        a = chunk[:, _cols(0, j // 2, lanes)]
        b = chunk[:, _cols(0, (SWA_REP + j) // 2, lanes)]
        if j % 2 == 0:
            pair = jnp.where(first_half, a, pltpu.roll(b, SWA_DH, 1))
        else:
            pair = jnp.where(first_half, pltpu.roll(a, SWA_DH, 1), b)
        dst_ref[:, _cols(0, j, lanes)] = pair.astype(bf16)
    rest = slice(SWA_Q, SWA_Q + 2 * SWA_KV)
    dst_ref[:, rest] = chunk[:, rest].astype(bf16)


def _load_mixer_weights(w_in_hbm, w_ref, stage_ref, sem_ref):
    jobs = _plain_weight_jobs(w_in_hbm, w_ref.at[:, 0:W_QS])
    for r0 in range(0, D_MODEL, STAGE_ROWS):
        rows = slice(r0, r0 + STAGE_ROWS)
        piece = (w_in_hbm.at[rows, W_QS:D_MIXW], slice(0, STAGE_ROWS), slice(0, D_MIXW - W_QS))
        jobs.append(([piece], functools.partial(_store_attention_columns, w_ref.at[rows, W_QS:D_MIXW])))
    _stream_weight_chunks(jobs, stage_ref, sem_ref)


def _mlp_weight_conversions(s, w_up_ref, w_dn_ref, w_up_bf16_ref, w_dn_bf16_ref):
    sub = lax.rem(s, MLP_WEIGHT_BLOCK_STEPS)
    thunks = []
    for src_ref, dst_ref in ((w_up_ref, w_up_bf16_ref), (w_dn_ref, w_dn_bf16_ref)):
        rows = src_ref.shape[0] // MLP_WEIGHT_BLOCK_STEPS
        row0 = pl.multiple_of(sub * rows, BF16_TILE_ROWS)
        if rows >= MLP_WEIGHT_PARTS * BF16_TILE_ROWS:
            part = rows // MLP_WEIGHT_PARTS
            windows = [(pl.ds(row0 + p * part, part), slice(None)) for p in range(MLP_WEIGHT_PARTS)]
        else:
            part = src_ref.shape[1] // MLP_WEIGHT_PARTS
            windows = [(pl.ds(row0, rows), slice(p * part, (p + 1) * part)) for p in range(MLP_WEIGHT_PARTS)]
        for window in windows:
            def convert(src_ref=src_ref, dst_ref=dst_ref, window=window):
                dst_ref[window] = src_ref[window].astype(bf16)
            thunks.append(convert)
    return thunks


def _proj_mixer_kernel(x_ref, g_ref, w_in_hbm, sink_ref, w_up_ref, w_dn_ref,
                       mixed_ref, w_up_bf16_ref, w_dn_bf16_ref,
                       w_ref, stage_ref, sem_ref,
                       mix_even_ref, mix_odd_ref, kdt_even_ref, kdt_odd_ref, qdec_ref, kdec_ref,
                       state_ref, kprev_ref, vprev_ref, dmat_ref, bias_ref, *, tiles_per_seq, num_tiles):
    s = pl.program_id(0)
    first_tile_of_seq = lax.rem(s + tiles_per_seq - 1, tiles_per_seq) == 0

    @pl.when(s == 0)
    def _first_step():
        _load_mixer_weights(w_in_hbm, w_ref, stage_ref, sem_ref)
        _init_tables(sink_ref, qdec_ref, kdec_ref, dmat_ref, bias_ref)

    @pl.when(first_tile_of_seq)
    def _reset_sequence_state():
        state_ref[...] = jnp.zeros_like(state_ref)
        kprev_ref[...] = jnp.zeros_like(kprev_ref)
        vprev_ref[...] = jnp.zeros_like(vprev_ref)

    even = (mix_even_ref, kdt_even_ref)
    odd = (mix_odd_ref, kdt_odd_ref)

    def step(write, read):
        prepare, pieces = (lambda: None), []
        if write is not None:
            prepare, pieces = _projection_pieces(x_ref, g_ref, w_ref, qdec_ref, kdec_ref, *write)
            conversions = _mlp_weight_conversions(s, w_up_ref, w_dn_ref, w_up_bf16_ref, w_dn_bf16_ref)
            assert 2 * len(conversions) <= len(pieces) + 1
            for n, conversion in enumerate(conversions):
                pieces[2 * n] = functools.partial(lambda piece, conversion: (piece(), conversion()),
                                                  pieces[2 * n], conversion)
        phases = []
        if read is not None:
            phases = _mixer_phases(*read, first_tile_of_seq, mixed_ref,
                                   state_ref, kprev_ref, vprev_ref, dmat_ref, bias_ref)
        prepare()
        for phase in phases[:1]:
            phase()
        emitted = 0
        for n, phase in enumerate(phases[1:], start=1):
            while emitted < len(pieces) and emitted * (len(phases) - 1) < n * len(pieces):
                pieces[emitted]()
                emitted += 1
            phase()
        for piece in pieces[emitted:]:
            piece()

    last = num_tiles
    last_read = odd if (num_tiles - 1) % 2 else even

    @pl.when(s == 0)
    def _first_tile():
        step(even, None)

    @pl.when((s > 0) & (s < last) & (lax.rem(s, 2) == 0))
    def _even_step():
        step(even, odd)

    @pl.when((s > 0) & (s < last) & (lax.rem(s, 2) == 1))
    def _odd_step():
        step(odd, even)

    @pl.when(s == last)
    def _last_tile():
        step(None, last_read)


def _proj_mixer(x2d, gain, w_in, sinks, w_up, w_dn, seq):
    n = x2d.shape[0]
    nt = n // TOK_TILE
    C = CHUNK
    steps = MLP_WEIGHT_BLOCK_STEPS
    assert nt % steps == 0 and D_MODEL % (nt * BF16_TILE_ROWS) == 0
    const = lambda s: (0, 0)
    projected_tile = lambda s: (jnp.minimum(s, nt - 1), 0)
    mixed_tile = lambda s: (jnp.maximum(s - 1, 0), 0)
    weight_block = lambda s: (jnp.minimum(s, nt - 1) // steps, 0)
    up_block = pl.BlockSpec((steps * D_MODEL // nt, D_FF), weight_block)
    dn_block = pl.BlockSpec((steps * D_FF // nt, D_MODEL), weight_block)
    return pl.pallas_call(
        functools.partial(_proj_mixer_kernel, tiles_per_seq=seq // TOK_TILE, num_tiles=nt),
        out_shape=(jax.ShapeDtypeStruct((n, D_MIXED), bf16),
                   jax.ShapeDtypeStruct((D_MODEL, D_FF), bf16),
                   jax.ShapeDtypeStruct((D_FF, D_MODEL), bf16)),
        grid=(nt + 1,),
        in_specs=[
            pl.BlockSpec((TOK_TILE, D_MODEL), projected_tile),
            pl.BlockSpec((1, D_MODEL), const),
            pl.BlockSpec(memory_space=pl.ANY),
            pl.BlockSpec(memory_space=pltpu.SMEM),
            up_block, dn_block,
        ],
        out_specs=(pl.BlockSpec((TOK_TILE, D_MIXED), mixed_tile), up_block, dn_block),
        scratch_shapes=[
            pltpu.VMEM((D_MODEL, D_MIXW), bf16),
            pltpu.VMEM((STAGE_SLOTS, STAGE_ROWS, STAGE_COLS), f32),
            pltpu.SemaphoreType.DMA((STAGE_SLOTS,)),
            pltpu.VMEM((TOK_TILE, D_MIXIN), bf16),
            pltpu.VMEM((TOK_TILE, D_MIXIN), bf16),
            pltpu.VMEM((CHUNKS_PER_TILE, RET_QK, C), bf16),
            pltpu.VMEM((CHUNKS_PER_TILE, RET_QK, C), bf16),
            pltpu.VMEM((TOK_TILE, RET_QK), f32),
            pltpu.VMEM((TOK_TILE, RET_QK), f32),
            pltpu.VMEM((RET_HEADS, RET_DK, RET_DV), f32),
            pltpu.VMEM((SWA_GROUPS, C, SWA_KV), bf16),
            pltpu.VMEM((C, SWA_KV), bf16),
            pltpu.VMEM((RET_HEADS, C, C), f32),
            pltpu.VMEM((2 * SWA_GROUPS, 2 * C, SWA_REP * C), f32),
        ],
        compiler_params=pltpu.CompilerParams(
            dimension_semantics=("arbitrary",), vmem_limit_bytes=VMEM_LIMIT_BYTES),
        name="proj_mixer",
    )(x2d, gain, w_in, sinks, w_up, w_dn)


def _load_post_weights(w_in_hbm, w_ro_hbm, w_so_hbm, w_o_hbm,
                       wg_ref, wro_ref, wso_ref, wo_ref, stage_ref, sem_ref):
    jobs = _plain_weight_jobs(w_in_hbm, wg_ref, col0=D_MIXW)
    jobs += _plain_weight_jobs(w_ro_hbm, wro_ref)
    runs = _swa_out_row_order().reshape(-1, SWA_DH)[:, 0]
    pieces = [(w_so_hbm.at[int(r0):int(r0) + SWA_DH, :], slice(n * SWA_DH, (n + 1) * SWA_DH), slice(0, STAGE_COLS))
              for n, r0 in enumerate(runs)]
    jobs += [(pieces, functools.partial(_store_bf16, wso_ref))]
    jobs += _plain_weight_jobs(w_o_hbm, wo_ref)
    _stream_weight_chunks(jobs, stage_ref, sem_ref)


def _post_kernel(x_ref, mixed_ref, gpre_mix_ref, w_in_hbm, w_ro_hbm, w_so_hbm, w_o_hbm, gpost_mix_ref,
                 gpre_mlp_ref, wup_ref, wdn_ref, gpost_mlp_ref, o_ref,
                 wg_ref, wro_ref, wso_ref, wo_ref, stage_ref, sem_ref):
    tm = x_ref.shape[0]
    ff_starts = list(range(0, D_FF, FF_CHUNK))

    @pl.when(pl.program_id(0) == 0)
    def _first_step():
        _load_post_weights(w_in_hbm, w_ro_hbm, w_so_hbm, w_o_hbm,
                           wg_ref, wro_ref, wso_ref, wo_ref, stage_ref, sem_ref)

    def row_block(rows):
        x = x_ref[rows, :]
        y_r = jnp.dot(mixed_ref[rows, X_RET:X_RET + RET_V], wro_ref[...], preferred_element_type=f32)
        y_s = jnp.dot(mixed_ref[rows, X_SWA:X_SWA + SWA_Q], wso_ref[...], preferred_element_type=f32)
        h = (x * _rms_scale(x) * gpre_mix_ref[...]).astype(bf16)
        gate_r = jnp.dot(h, wg_ref[:, 0:D_MODEL], preferred_element_type=f32)
        gate_s = jnp.dot(h, wg_ref[:, D_MODEL:D_GATES], preferred_element_type=f32)
        yield
        merged = _sigmoid(gate_r) * y_r + _sigmoid(gate_s) * y_s
        z = jnp.dot(merged.astype(bf16), wo_ref[...], preferred_element_type=f32)
        yield
        x1 = x + z * _rms_scale(z) * gpost_mix_ref[...]
        h1 = (x1 * _rms_scale(x1) * gpre_mlp_ref[...]).astype(bf16)

        def up(c0):
            return jnp.dot(h1, wup_ref[:, c0:c0 + FF_CHUNK], preferred_element_type=f32)

        acc = None
        u_next = up(ff_starts[0])
        yield
        for n, c0 in enumerate(ff_starts):
            u = u_next
            if n + 1 < len(ff_starts):
                u_next = up(ff_starts[n + 1])
            u = jnp.maximum(u, 0.0)
            part = jnp.dot((u * u).astype(bf16), wdn_ref[c0:c0 + FF_CHUNK, :], preferred_element_type=f32)
            acc = part if acc is None else acc + part
            yield
        o_ref[rows, :] = x1 + acc * _rms_scale(acc) * gpost_mlp_ref[...]

    active = [row_block(slice(r0, r0 + POST_BLOCK_ROWS)) for r0 in range(0, tm, POST_BLOCK_ROWS)]
    while active:
        for gen in list(active):
            if next(gen, "done") == "done":
                active.remove(gen)


def _post(x2d, mixed, gpre_mix, w_in, w_ro, w_so, w_o, gpost_mix, gpre_mlp, w_up_bf16, w_dn_bf16, gpost_mlp):
    n = x2d.shape[0]
    const = lambda i: (0, 0)
    tile = lambda width: pl.BlockSpec((POST_TILE, width), lambda i: (i, 0))
    in_hbm = pl.BlockSpec(memory_space=pl.ANY)
    gain = pl.BlockSpec((1, D_MODEL), const)
    resident = lambda shape: pl.BlockSpec(shape, const, pipeline_mode=pl.Buffered(1))
    return pl.pallas_call(
        _post_kernel,
        out_shape=jax.ShapeDtypeStruct((n, D_MODEL), f32),
        grid=(n // POST_TILE,),
        in_specs=[
            tile(D_MODEL), tile(D_MIXED),
            gain, in_hbm,
            in_hbm, in_hbm, in_hbm, gain,
            gain, resident((D_MODEL, D_FF)), resident((D_FF, D_MODEL)), gain,
        ],
        out_specs=tile(D_MODEL),
        scratch_shapes=[
            pltpu.VMEM((D_MODEL, D_GATES), bf16),
            pltpu.VMEM((RET_V, D_MODEL), bf16),
            pltpu.VMEM((SWA_Q, D_MODEL), bf16),
            pltpu.VMEM((D_MODEL, D_MODEL), bf16),
            pltpu.VMEM((STAGE_SLOTS, STAGE_ROWS, STAGE_COLS), f32),
            pltpu.SemaphoreType.DMA((STAGE_SLOTS,)),
        ],
        compiler_params=pltpu.CompilerParams(
            dimension_semantics=("arbitrary",), vmem_limit_bytes=VMEM_LIMIT_BYTES),
        name="post",
    )(x2d, mixed, gpre_mix, w_in, w_ro, w_so, w_o, gpost_mix, gpre_mlp, w_up_bf16, w_dn_bf16, gpost_mlp)


def kernel(x, pre_mix_norm, w_in, w_ret_out, w_swa_out, w_out, sinks, post_mix_norm, pre_mlp_norm, w_up,
           w_down, post_mlp_norm):
    batch, seq, d = x.shape
    depth = w_in.shape[0]
    assert d == D_MODEL and seq % TOK_TILE == 0 and seq // TOK_TILE > 1 and (batch * seq) % POST_TILE == 0
    x2d = x.reshape(batch * seq, d)
    for l in range(depth):
        pre_mix_gain = pre_mix_norm[l][None, :]
        mixed, w_up_bf16, w_dn_bf16 = _proj_mixer(x2d, pre_mix_gain, w_in[l], sinks[l].astype(f32),
                                                  w_up[l], w_down[l], seq)
        x2d = _post(x2d, mixed, pre_mix_gain, w_in[l], w_ret_out[l], w_swa_out[l], w_out[l],
                    post_mix_norm[l][None, :], pre_mlp_norm[l][None, :], w_up_bf16, w_dn_bf16,
                    post_mlp_norm[l][None, :])
    return x2d.reshape(batch, seq, d)
```

```python
import functools

import numpy as np
import jax
import jax.numpy as jnp
from jax import lax
from jax.experimental import pallas as pl
from jax.experimental.pallas import tpu as pltpu

D_MODEL = 1024
RET_HEADS = 4
RET_DK = 128
RET_DV = 256
CHUNK = 128
SWA_HEADS = 8
SWA_GROUPS = 2
SWA_REP = SWA_HEADS // SWA_GROUPS
SWA_DH = 64
D_FF = 4 * D_MODEL
EPS = 1e-6

RET_QK = RET_HEADS * RET_DK
RET_V = RET_HEADS * RET_DV
SWA_Q = SWA_HEADS * SWA_DH
SWA_KV = SWA_GROUPS * SWA_DH

W_QR = 0
W_KR = W_QR + RET_QK
W_VR = W_KR + RET_QK
W_GR = W_VR + RET_V
W_QS = W_GR + RET_V
W_KS = W_QS + SWA_Q
W_VS = W_KS + SWA_KV
D_MIXW = W_VS + SWA_KV
D_GATES = 2 * D_MODEL

M_QR = 0
M_KR = M_QR + RET_QK
M_VR = M_KR + RET_QK
M_SG = M_VR + RET_V
M_QD = M_SG + RET_V
M_QS = M_QD + RET_QK
M_KG = M_QS + SWA_Q
M_VS = M_KG + SWA_GROUPS * SWA_KV
D_MIXIN = M_VS + SWA_KV

X_RET = 0
X_SWA = X_RET + RET_V
D_MIXED = X_SWA + SWA_Q

TOK_TILE = 512
CHUNKS_PER_TILE = TOK_TILE // CHUNK
PROJ_PIECE_COLS = 256
MLP_WEIGHT_BLOCK_STEPS = 4
MLP_WEIGHT_PARTS = 4
POST_TILE = 512
POST_BLOCK_ROWS = 256
FF_CHUNK = 1024
STAGE_ROWS, STAGE_COLS = 512, 1024
STAGE_SLOTS = 3
BF16_TILE_ROWS = 16

VMEM_LIMIT_BYTES = 56 * 1024 * 1024

LOG_GAMMA = [float(np.log(1.0 - 2.0 ** (-5.0 - h))) for h in range(RET_HEADS)]
ALIBI_SLOPES = [float(2.0 ** (-8.0 / SWA_HEADS * (h + 1))) for h in range(SWA_HEADS)]
K_SCALE = float(RET_DK ** -0.5)
S_SCALE = float(SWA_DH ** -0.5)
LOG2E = float(np.log2(np.e))

f32 = jnp.float32
bf16 = jnp.bfloat16


def _swa_out_row_order():
    return np.arange(SWA_Q).reshape(SWA_GROUPS, SWA_REP, SWA_DH).transpose(1, 0, 2).reshape(-1)


def _rms_scale(v):
    return lax.rsqrt(jnp.mean(v * v, axis=-1, keepdims=True) + EPS)


def _sigmoid(v):
    return 0.5 * jnp.tanh(0.5 * v) + 0.5


def _cols(offset, index, width):
    return slice(offset + index * width, offset + (index + 1) * width)


def _store_bf16(dst_ref, chunk):
    dst_ref[...] = chunk[0:dst_ref.shape[0], 0:dst_ref.shape[1]].astype(bf16)


def _plain_weight_jobs(src_hbm, dst_ref, col0=0):
    n_rows, n_cols = dst_ref.shape
    jobs = []
    for r0 in range(0, n_rows, STAGE_ROWS):
        for c0 in range(0, n_cols, STAGE_COLS):
            rows, cols = min(STAGE_ROWS, n_rows - r0), min(STAGE_COLS, n_cols - c0)
            piece = (src_hbm.at[r0:r0 + rows, col0 + c0:col0 + c0 + cols], slice(0, rows), slice(0, cols))
            jobs.append(([piece], functools.partial(_store_bf16, dst_ref.at[r0:r0 + rows, c0:c0 + cols])))
    return jobs


def _stream_weight_chunks(jobs, stage_ref, sem_ref):
    slots = stage_ref.shape[0]

    def copies(k):
        slot = k % slots
        return [pltpu.make_async_copy(src, stage_ref.at[slot, rows, cols], sem_ref.at[slot])
                for src, rows, cols in jobs[k][0]]

    for k in range(min(slots - 1, len(jobs))):
        for copy in copies(k):
            copy.start()
    for k, job in enumerate(jobs):
        ahead = k + slots - 1
        if ahead < len(jobs):
            for copy in copies(ahead):
                copy.start()
        for copy in copies(k):
            copy.wait()
        job[1](stage_ref[k % slots])


def _init_tables(sink_ref, qdec_ref, kdec_ref, dmat_ref, bias_ref):
    C = CHUNK
    pos = (lax.broadcasted_iota(jnp.int32, (TOK_TILE, RET_DK), 0) % C).astype(f32)
    i = lax.broadcasted_iota(jnp.int32, (C, C), 0).astype(f32)
    j = lax.broadcasted_iota(jnp.int32, (C, C), 1).astype(f32)
    diff = i - j
    for h in range(RET_HEADS):
        qdec_ref[:, _cols(0, h, RET_DK)] = jnp.exp(LOG_GAMMA[h] * (pos + 1.0))
        kdec_ref[:, _cols(0, h, RET_DK)] = jnp.exp(LOG_GAMMA[h] * (C - 1.0 - pos)) * K_SCALE
        dmat_ref[h] = jnp.where(diff >= 0, jnp.exp(LOG_GAMMA[h] * jnp.maximum(diff, 0.0)), 0.0) * K_SCALE
    kpos = lax.broadcasted_iota(jnp.int32, (2 * C, SWA_REP * C), 0)
    col = lax.broadcasted_iota(jnp.int32, (2 * C, SWA_REP * C), 1)
    rep = col // C
    dist = (col - rep * C + C) - kpos
    valid = (dist >= 0) & (dist < C)
    distf = dist.astype(f32)
    for g in range(SWA_GROUPS):
        slope = jnp.zeros((2 * C, SWA_REP * C), f32)
        sink = jnp.zeros((2 * C, SWA_REP * C), f32)
        for r in range(SWA_REP):
            slope = jnp.where(rep == r, ALIBI_SLOPES[g * SWA_REP + r], slope)
            sink = jnp.where(rep == r, sink_ref[g * SWA_REP + r], sink)
        alibi = -slope * distf
        bias_ref[g] = jnp.where(kpos == 0, sink, jnp.where(valid, alibi, -jnp.inf)) * LOG2E
        bias_ref[SWA_GROUPS + g] = jnp.where(
            kpos == 0, sink, jnp.where(valid & (kpos >= C), alibi, -jnp.inf)) * LOG2E


def _projection_pieces(x_ref, g_ref, w_ref, qdec_ref, kdec_ref, mix_ref, kdt_ref):
    C = CHUNK
    shared = {}

    def prepare():
        x = x_ref[...]
        shared["h"] = (x * _rms_scale(x) * g_ref[...]).astype(bf16)

    def proj(c0, width):
        return jnp.dot(shared["h"], w_ref[:, c0:c0 + width], preferred_element_type=f32)

    width = PROJ_PIECE_COLS

    def swish_gate(part):
        def run():
            g = proj(W_GR + part * width, width)
            mix_ref[:, _cols(M_SG, part, width)] = (g * _sigmoid(g)).astype(bf16)
        return run

    def queries(part):
        def run():
            q = proj(W_QR + part * width, width)
            mix_ref[:, _cols(M_QR, part, width)] = q.astype(bf16)
            mix_ref[:, _cols(M_QD, part, width)] = (q * qdec_ref[:, _cols(0, part, width)]).astype(bf16)
        return run

    def keys(part):
        def run():
            k = proj(W_KR + part * width, width)
            mix_ref[:, _cols(M_KR, part, width)] = k.astype(bf16)
            kd = k * kdec_ref[:, _cols(0, part, width)]
            heads_per_piece = width // RET_DK
            for ci in range(CHUNKS_PER_TILE):
                for hd in range(heads_per_piece):
                    kdt_ref[ci, _cols(0, part * heads_per_piece + hd, RET_DK), :] = (
                        kd[ci * C:(ci + 1) * C, _cols(0, hd, RET_DK)].T.astype(bf16))
        return run

    def attention_queries(part):
        def run():
            mix_ref[:, _cols(M_QS, part, width)] = proj(W_QS + part * width, width).astype(bf16)
        return run

    def attention_keys_values():
        kv_s = proj(W_KS, 2 * SWA_KV)
        ks = kv_s[:, 0:SWA_KV] * (S_SCALE * LOG2E)
        lane = lax.broadcasted_iota(jnp.int32, (TOK_TILE, SWA_KV), 1)
        for grp in range(SWA_GROUPS):
            in_group = (lane >= grp * SWA_DH) & (lane < (grp + 1) * SWA_DH)
            mix_ref[:, _cols(M_KG, grp, SWA_KV)] = jnp.where(in_group, ks, 0.0).astype(bf16)
        mix_ref[:, M_VS:M_VS + SWA_KV] = kv_s[:, SWA_KV:2 * SWA_KV].astype(bf16)

    def values(part):
        def run():
            mix_ref[:, _cols(M_VR, part, width)] = proj(W_VR + part * width, width).astype(bf16)
        return run

    light = [queries(n) for n in range(RET_QK // width)]
    light += [keys(n) for n in range(RET_QK // width)]
    light += [attention_queries(n) for n in range(SWA_Q // width)]
    light += [attention_keys_values]
    light += [values(n) for n in range(RET_V // width)]
    heavy = [swish_gate(n) for n in range(RET_V // width)]
    pieces = []
    per_heavy = -(-len(light) // len(heavy))
    for n, piece in enumerate(heavy):
        pieces.append(piece)
        pieces += light[n * per_heavy:(n + 1) * per_heavy]
    return prepare, pieces


def _mixer_phases(mix_ref, kdt_ref, first_tile_of_seq, mixed_ref,
                  state_ref, kprev_ref, vprev_ref, dmat_ref, bias_ref):
    C = CHUNK
    ones_rows = jnp.ones((BF16_TILE_ROWS, 2 * C), bf16)
    live = [dict() for _ in range(CHUNKS_PER_TILE)]

    def rows(t):
        return slice(t * C, (t + 1) * C)

    def zero_first_row(block):
        top = block[0:BF16_TILE_ROWS, :]
        is_first = lax.broadcasted_iota(jnp.int32, top.shape, 0) == 0
        return jnp.concatenate([jnp.where(is_first, jnp.zeros_like(top), top), block[BF16_TILE_ROWS:, :]], axis=0)

    def first_matmuls(t):
        st = live[t]
        st["qk"] = [lax.dot_general(mix_ref[rows(t), _cols(M_QR, h, RET_DK)], mix_ref[rows(t), _cols(M_KR, h, RET_DK)],
                                    (((1,), (1,)), ((), ())), preferred_element_type=f32)
                    for h in range(RET_HEADS)]
        st["kv"] = [jnp.dot(kdt_ref[t, _cols(0, h, RET_DK), :], mix_ref[rows(t), _cols(M_VR, h, RET_DV)],
                            preferred_element_type=f32) for h in range(RET_HEADS)]
        if t == 0:
            keys = [jnp.concatenate([kprev_ref[g], mix_ref[0:C, _cols(M_KG, g, SWA_KV)]], axis=0)
                    for g in range(SWA_GROUPS)]
            vals = jnp.concatenate([vprev_ref[...], mix_ref[0:C, M_VS:M_VS + SWA_KV]], axis=0)
            table0 = jnp.where(first_tile_of_seq, SWA_GROUPS, 0)
        else:
            prev_and_current = slice((t - 1) * C, (t + 1) * C)
            keys = [mix_ref[prev_and_current, _cols(M_KG, g, SWA_KV)] for g in range(SWA_GROUPS)]
            vals = mix_ref[prev_and_current, M_VS:M_VS + SWA_KV]
            table0 = 0
        keys = [zero_first_row(k) for k in keys]
        vals_t = zero_first_row(vals).astype(f32).T
        st["vt_ones"] = [jnp.concatenate([vals_t[g * SWA_DH:(g + 1) * SWA_DH, :].astype(bf16), ones_rows], axis=0)
                         for g in range(SWA_GROUPS)]
        q_all = jnp.concatenate([mix_ref[rows(t), _cols(M_QS, j, C)] for j in range(SWA_REP)], axis=0)
        st["s"] = [lax.dot_general(keys[g], q_all, (((1,), (1,)), ((), ())), preferred_element_type=f32)
                   + bias_ref[table0 + g] for g in range(SWA_GROUPS)]

    def retention_vector_work(t):
        st = live[t]
        st["a"], st["state_bf16"] = [], []
        for h in range(RET_HEADS):
            s_prev = state_ref[h]
            if t == 0:
                s_prev = jnp.where(first_tile_of_seq, 0.0, s_prev)
            st["a"].append((st["qk"][h] * dmat_ref[h]).astype(bf16))
            st["state_bf16"].append(s_prev.astype(bf16))
            state_ref[h] = float(np.exp(LOG_GAMMA[h] * C)) * s_prev + st["kv"][h]
        del st["qk"], st["kv"]

    def softmax_numerators(t):
        st = live[t]
        st["p"] = [jnp.exp2(s - jnp.max(s, axis=0, keepdims=True)).astype(bf16) for s in st["s"]]
        del st["s"]

    def second_matmuls(t):
        st = live[t]
        st["o"] = [jnp.dot(jnp.concatenate([st["a"][h], mix_ref[rows(t), _cols(M_QD, h, RET_DK)]], axis=1),
                           jnp.concatenate([mix_ref[rows(t), _cols(M_VR, h, RET_DV)], st["state_bf16"][h]], axis=0),
                           preferred_element_type=f32) for h in range(RET_HEADS)]
        st["r"] = [jnp.dot(st["vt_ones"][g], st["p"][g], preferred_element_type=f32) for g in range(SWA_GROUPS)]
        del st["a"], st["state_bf16"], st["p"], st["vt_ones"]

    def retention_output(t):
        st = live[t]
        for h in range(RET_HEADS):
            o = st["o"][h]
            mu = jnp.mean(o, axis=-1, keepdims=True)
            d = o - mu
            var = jnp.mean(d * d, axis=-1, keepdims=True)
            mixed_ref[rows(t), _cols(X_RET, h, RET_DV)] = (
                mix_ref[rows(t), _cols(M_SG, h, RET_DV)].astype(f32) * (d * lax.rsqrt(var + EPS))).astype(bf16)
        del st["o"]

    def attention_output(t):
        st = live[t]
        o_t = [r[0:SWA_DH, :] / r[SWA_DH:SWA_DH + 1, :] for r in st["r"]]
        for j in range(SWA_REP):
            pair_t = jnp.concatenate([o[:, _cols(0, j, C)] for o in o_t], axis=0)
            mixed_ref[rows(t), _cols(X_SWA, j, C)] = pair_t.T.astype(bf16)
        st.clear()

    def carry_last_block():
        last = slice(TOK_TILE - C, TOK_TILE)
        for g in range(SWA_GROUPS):
            kprev_ref[g] = mix_ref[last, _cols(M_KG, g, SWA_KV)]
        vprev_ref[...] = mix_ref[last, M_VS:M_VS + SWA_KV]

    phases = []
    for t in range(CHUNKS_PER_TILE + 1):
        if t < CHUNKS_PER_TILE:
            phases.append(functools.partial(first_matmuls, t))
        if t >= 1:
            phases += [functools.partial(fn, t - 1) for fn in (
                retention_vector_work, softmax_numerators, second_matmuls, retention_output, attention_output)]
    phases.append(carry_last_block)
    return phases


def _store_attention_columns(dst_ref, chunk):
    lanes = 2 * SWA_DH
    first_half = lax.broadcasted_iota(jnp.int32, (chunk.shape[0], lanes), 1) < SWA_DH
    for j in range(SWA_REP):
        a = chunk[:, _cols(0, j // 2, lanes)]
        b = chunk[:, _cols(0, (SWA_REP + j) // 2, lanes)]
        if j % 2 == 0:
            pair = jnp.where(first_half, a, pltpu.roll(b, SWA_DH, 1))
        else:
            pair = jnp.where(first_half, pltpu.roll(a, SWA_DH, 1), b)
        dst_ref[:, _cols(0, j, lanes)] = pair.astype(bf16)
    rest = slice(SWA_Q, SWA_Q + 2 * SWA_KV)
    dst_ref[:, rest] = chunk[:, rest].astype(bf16)


def _load_mixer_weights(w_in_hbm, w_ref, stage_ref, sem_ref):
    jobs = _plain_weight_jobs(w_in_hbm, w_ref.at[:, 0:W_QS])
    for r0 in range(0, D_MODEL, STAGE_ROWS):
        rows = slice(r0, r0 + STAGE_ROWS)
        piece = (w_in_hbm.at[rows, W_QS:D_MIXW], slice(0, STAGE_ROWS), slice(0, D_MIXW - W_QS))
        jobs.append(([piece], functools.partial(_store_attention_columns, w_ref.at[rows, W_QS:D_MIXW])))
    _stream_weight_chunks(jobs, stage_ref, sem_ref)


def _mlp_weight_conversions(s, w_up_ref, w_dn_ref, w_up_bf16_ref, w_dn_bf16_ref):
    sub = lax.rem(s, MLP_WEIGHT_BLOCK_STEPS)
    thunks = []
    for src_ref, dst_ref in ((w_up_ref, w_up_bf16_ref), (w_dn_ref, w_dn_bf16_ref)):
        rows = src_ref.shape[0] // MLP_WEIGHT_BLOCK_STEPS
        row0 = pl.multiple_of(sub * rows, BF16_TILE_ROWS)
        if rows >= MLP_WEIGHT_PARTS * BF16_TILE_ROWS:
            part = rows // MLP_WEIGHT_PARTS
            windows = [(pl.ds(row0 + p * part, part), slice(None)) for p in range(MLP_WEIGHT_PARTS)]
        else:
            part = src_ref.shape[1] // MLP_WEIGHT_PARTS
            windows = [(pl.ds(row0, rows), slice(p * part, (p + 1) * part)) for p in range(MLP_WEIGHT_PARTS)]
        for window in windows:
            def convert(src_ref=src_ref, dst_ref=dst_ref, window=window):
                dst_ref[window] = src_ref[window].astype(bf16)
            thunks.append(convert)
    return thunks


def _proj_mixer_kernel(x_ref, g_ref, w_in_hbm, sink_ref, w_up_ref, w_dn_ref,
                       mixed_ref, w_up_bf16_ref, w_dn_bf16_ref,
                       w_ref, stage_ref, sem_ref,
                       mix_even_ref, mix_odd_ref, kdt_even_ref, kdt_odd_ref, qdec_ref, kdec_ref,
                       state_ref, kprev_ref, vprev_ref, dmat_ref, bias_ref, *, tiles_per_seq, num_tiles):
    s = pl.program_id(0)
    first_tile_of_seq = lax.rem(s + tiles_per_seq - 1, tiles_per_seq) == 0

    @pl.when(s == 0)
    def _first_step():
        _load_mixer_weights(w_in_hbm, w_ref, stage_ref, sem_ref)
        _init_tables(sink_ref, qdec_ref, kdec_ref, dmat_ref, bias_ref)
        state_ref[...] = jnp.zeros_like(state_ref)
        kprev_ref[...] = jnp.zeros_like(kprev_ref)
        vprev_ref[...] = jnp.zeros_like(vprev_ref)

    even = (mix_even_ref, kdt_even_ref)
    odd = (mix_odd_ref, kdt_odd_ref)

    def step(write, read):
        prepare, pieces = (lambda: None), []
        if write is not None:
            prepare, pieces = _projection_pieces(x_ref, g_ref, w_ref, qdec_ref, kdec_ref, *write)
            conversions = _mlp_weight_conversions(s, w_up_ref, w_dn_ref, w_up_bf16_ref, w_dn_bf16_ref)
            assert 2 * len(conversions) <= len(pieces) + 1
            for n, conversion in enumerate(conversions):
                pieces[2 * n] = functools.partial(lambda piece, conversion: (piece(), conversion()),
                                                  pieces[2 * n], conversion)
        phases = []
        if read is not None:
            phases = _mixer_phases(*read, first_tile_of_seq, mixed_ref,
                                   state_ref, kprev_ref, vprev_ref, dmat_ref, bias_ref)
        prepare()
        for phase in phases[:1]:
            phase()
        emitted = 0
        for n, phase in enumerate(phases[1:], start=1):
            while emitted < len(pieces) and emitted * (len(phases) - 1) < n * len(pieces):
                pieces[emitted]()
                emitted += 1
            phase()
        for piece in pieces[emitted:]:
            piece()

    last = num_tiles
    last_read = odd if (num_tiles - 1) % 2 else even

    @pl.when(s == 0)
    def _first_tile():
        step(even, None)

    @pl.when((s > 0) & (s < last) & (lax.rem(s, 2) == 0))
    def _even_step():
        step(even, odd)

    @pl.when((s > 0) & (s < last) & (lax.rem(s, 2) == 1))
    def _odd_step():
        step(odd, even)

    @pl.when(s == last)
    def _last_tile():
        step(None, last_read)


def _proj_mixer(x2d, gain, w_in, sinks, w_up, w_dn, seq):
    n = x2d.shape[0]
    nt = n // TOK_TILE
    C = CHUNK
    steps = MLP_WEIGHT_BLOCK_STEPS
    assert nt % steps == 0 and D_MODEL % (nt * BF16_TILE_ROWS) == 0
    const = lambda s: (0, 0)
    projected_tile = lambda s: (jnp.minimum(s, nt - 1), 0)
    mixed_tile = lambda s: (jnp.maximum(s - 1, 0), 0)
    weight_block = lambda s: (jnp.minimum(s, nt - 1) // steps, 0)
    up_block = pl.BlockSpec((steps * D_MODEL // nt, D_FF), weight_block)
    dn_block = pl.BlockSpec((steps * D_FF // nt, D_MODEL), weight_block)
    return pl.pallas_call(
        functools.partial(_proj_mixer_kernel, tiles_per_seq=seq // TOK_TILE, num_tiles=nt),
        out_shape=(jax.ShapeDtypeStruct((n, D_MIXED), bf16),
                   jax.ShapeDtypeStruct((D_MODEL, D_FF), bf16),
                   jax.ShapeDtypeStruct((D_FF, D_MODEL), bf16)),
        grid=(nt + 1,),
        in_specs=[
            pl.BlockSpec((TOK_TILE, D_MODEL), projected_tile),
            pl.BlockSpec((1, D_MODEL), const),
            pl.BlockSpec(memory_space=pl.ANY),
            pl.BlockSpec(memory_space=pltpu.SMEM),
            up_block, dn_block,
        ],
        out_specs=(pl.BlockSpec((TOK_TILE, D_MIXED), mixed_tile), up_block, dn_block),
        scratch_shapes=[
            pltpu.VMEM((D_MODEL, D_MIXW), bf16),
            pltpu.VMEM((STAGE_SLOTS, STAGE_ROWS, STAGE_COLS), f32),
            pltpu.SemaphoreType.DMA((STAGE_SLOTS,)),
            pltpu.VMEM((TOK_TILE, D_MIXIN), bf16),
            pltpu.VMEM((TOK_TILE, D_MIXIN), bf16),
            pltpu.VMEM((CHUNKS_PER_TILE, RET_QK, C), bf16),
            pltpu.VMEM((CHUNKS_PER_TILE, RET_QK, C), bf16),
            pltpu.VMEM((TOK_TILE, RET_QK), f32),
            pltpu.VMEM((TOK_TILE, RET_QK), f32),
            pltpu.VMEM((RET_HEADS, RET_DK, RET_DV), f32),
            pltpu.VMEM((SWA_GROUPS, C, SWA_KV), bf16),
            pltpu.VMEM((C, SWA_KV), bf16),
            pltpu.VMEM((RET_HEADS, C, C), f32),
            pltpu.VMEM((2 * SWA_GROUPS, 2 * C, SWA_REP * C), f32),
        ],
        compiler_params=pltpu.CompilerParams(
            dimension_semantics=("arbitrary",), vmem_limit_bytes=VMEM_LIMIT_BYTES),
        name="proj_mixer",
    )(x2d, gain, w_in, sinks, w_up, w_dn)


def _load_post_weights(w_in_hbm, w_ro_hbm, w_so_hbm, w_o_hbm,
                       wg_ref, wro_ref, wso_ref, wo_ref, stage_ref, sem_ref):
    jobs = _plain_weight_jobs(w_in_hbm, wg_ref, col0=D_MIXW)
    jobs += _plain_weight_jobs(w_ro_hbm, wro_ref)
    runs = _swa_out_row_order().reshape(-1, SWA_DH)[:, 0]
    pieces = [(w_so_hbm.at[int(r0):int(r0) + SWA_DH, :], slice(n * SWA_DH, (n + 1) * SWA_DH), slice(0, STAGE_COLS))
              for n, r0 in enumerate(runs)]
    jobs += [(pieces, functools.partial(_store_bf16, wso_ref))]
    jobs += _plain_weight_jobs(w_o_hbm, wo_ref)
    _stream_weight_chunks(jobs, stage_ref, sem_ref)


def _post_kernel(x_ref, mixed_ref, gpre_mix_ref, w_in_hbm, w_ro_hbm, w_so_hbm, w_o_hbm, gpost_mix_ref,
                 gpre_mlp_ref, wup_ref, wdn_ref, gpost_mlp_ref, o_ref,
                 wg_ref, wro_ref, wso_ref, wo_ref, stage_ref, sem_ref):
    tm = x_ref.shape[0]
    ff_starts = list(range(0, D_FF, FF_CHUNK))

    @pl.when(pl.program_id(0) == 0)
    def _first_step():
        _load_post_weights(w_in_hbm, w_ro_hbm, w_so_hbm, w_o_hbm,
                           wg_ref, wro_ref, wso_ref, wo_ref, stage_ref, sem_ref)

    def row_block(rows):
        x = x_ref[rows, :]
        y_r = jnp.dot(mixed_ref[rows, X_RET:X_RET + RET_V], wro_ref[...], preferred_element_type=f32)
        y_s = jnp.dot(mixed_ref[rows, X_SWA:X_SWA + SWA_Q], wso_ref[...], preferred_element_type=f32)
        h = (x * _rms_scale(x) * gpre_mix_ref[...]).astype(bf16)
        gate_r = jnp.dot(h, wg_ref[:, 0:D_MODEL], preferred_element_type=f32)
        gate_s = jnp.dot(h, wg_ref[:, D_MODEL:D_GATES], preferred_element_type=f32)
        yield
        merged = _sigmoid(gate_r) * y_r + _sigmoid(gate_s) * y_s
        z = jnp.dot(merged.astype(bf16), wo_ref[...], preferred_element_type=f32)
        yield
        x1 = x + z * _rms_scale(z) * gpost_mix_ref[...]
        h1 = (x1 * _rms_scale(x1) * gpre_mlp_ref[...]).astype(bf16)

        def up(c0):
            return jnp.dot(h1, wup_ref[:, c0:c0 + FF_CHUNK], preferred_element_type=f32)

        acc = None
        u_next = up(ff_starts[0])
        yield
        for n, c0 in enumerate(ff_starts):
            u = u_next
            if n + 1 < len(ff_starts):
                u_next = up(ff_starts[n + 1])
            u = jnp.maximum(u, 0.0)
            part = jnp.dot((u * u).astype(bf16), wdn_ref[c0:c0 + FF_CHUNK, :], preferred_element_type=f32)
            acc = part if acc is None else acc + part
            yield
        o_ref[rows, :] = x1 + acc * _rms_scale(acc) * gpost_mlp_ref[...]

    active = [row_block(slice(r0, r0 + POST_BLOCK_ROWS)) for r0 in range(0, tm, POST_BLOCK_ROWS)]
    while active:
        for gen in list(active):
            if next(gen, "done") == "done":
                active.remove(gen)


def _post(x2d, mixed, gpre_mix, w_in, w_ro, w_so, w_o, gpost_mix, gpre_mlp, w_up_bf16, w_dn_bf16, gpost_mlp):
    n = x2d.shape[0]
    const = lambda i: (0, 0)
    tile = lambda width: pl.BlockSpec((POST_TILE, width), lambda i: (i, 0))
    in_hbm = pl.BlockSpec(memory_space=pl.ANY)
    gain = pl.BlockSpec((1, D_MODEL), const)
    resident = lambda shape: pl.BlockSpec(shape, const, pipeline_mode=pl.Buffered(1))
    return pl.pallas_call(
        _post_kernel,
        out_shape=jax.ShapeDtypeStruct((n, D_MODEL), f32),
        grid=(n // POST_TILE,),
        in_specs=[
            tile(D_MODEL), tile(D_MIXED),
            gain, in_hbm,
            in_hbm, in_hbm, in_hbm, gain,
            gain, resident((D_MODEL, D_FF)), resident((D_FF, D_MODEL)), gain,
        ],
        out_specs=tile(D_MODEL),
        scratch_shapes=[
            pltpu.VMEM((D_MODEL, D_GATES), bf16),
            pltpu.VMEM((RET_V, D_MODEL), bf16),
            pltpu.VMEM((SWA_Q, D_MODEL), bf16),
            pltpu.VMEM((D_MODEL, D_MODEL), bf16),
            pltpu.VMEM((STAGE_SLOTS, STAGE_ROWS, STAGE_COLS), f32),
            pltpu.SemaphoreType.DMA((STAGE_SLOTS,)),
        ],
        compiler_params=pltpu.CompilerParams(
            dimension_semantics=("arbitrary",), vmem_limit_bytes=VMEM_LIMIT_BYTES),
        name="post",
    )(x2d, mixed, gpre_mix, w_in, w_ro, w_so, w_o, gpost_mix, gpre_mlp, w_up_bf16, w_dn_bf16, gpost_mlp)


def kernel(x, pre_mix_norm, w_in, w_ret_out, w_swa_out, w_out, sinks, post_mix_norm, pre_mlp_norm, w_up,
           w_down, post_mlp_norm):
    batch, seq, d = x.shape
    depth = w_in.shape[0]
    assert d == D_MODEL and seq % TOK_TILE == 0 and seq // TOK_TILE > 1 and (batch * seq) % POST_TILE == 0
    x2d = x.reshape(batch * seq, d)
    for l in range(depth):
        pre_mix_gain = pre_mix_norm[l][None, :]
        mixed, w_up_bf16, w_dn_bf16 = _proj_mixer(x2d, pre_mix_gain, w_in[l], sinks[l].astype(f32),
                                                  w_up[l], w_down[l], seq)
        x2d = _post(x2d, mixed, pre_mix_gain, w_in[l], w_ret_out[l], w_swa_out[l], w_out[l],
                    post_mix_norm[l][None, :], pre_mlp_norm[l][None, :], w_up_bf16, w_dn_bf16,
                    post_mlp_norm[l][None, :])
    return x2d.reshape(batch, seq, d)
```

```python
import functools

import numpy as np
import jax
import jax.numpy as jnp
from jax import lax
from jax.experimental import pallas as pl
from jax.experimental.pallas import tpu as pltpu

D_MODEL = 1024
RET_HEADS = 4
RET_DK = 128
RET_DV = 256
CHUNK = 128
SWA_HEADS = 8
SWA_GROUPS = 2
SWA_REP = SWA_HEADS // SWA_GROUPS
SWA_DH = 64
D_FF = 4 * D_MODEL
EPS = 1e-6

RET_QK = RET_HEADS * RET_DK
RET_V = RET_HEADS * RET_DV
SWA_Q = SWA_HEADS * SWA_DH
SWA_KV = SWA_GROUPS * SWA_DH

W_QR = 0
W_KR = W_QR + RET_QK
W_VR = W_KR + RET_QK
W_GR = W_VR + RET_V
W_QS = W_GR + RET_V
W_KS = W_QS + SWA_Q
W_VS = W_KS + SWA_KV
D_MIXW = W_VS + SWA_KV
D_GATES = 2 * D_MODEL

M_QR = 0
M_KR = M_QR + RET_QK
M_VR = M_KR + RET_QK
M_SG = M_VR + RET_V
M_QD = M_SG + RET_V
M_QS = M_QD + RET_QK
M_KG = M_QS + SWA_Q
M_VS = M_KG + SWA_GROUPS * SWA_KV
D_MIXIN = M_VS + SWA_KV

X_RET = 0
X_SWA = X_RET + RET_V
D_MIXED = X_SWA + SWA_Q

TOK_TILE = 512
CHUNKS_PER_TILE = TOK_TILE // CHUNK
PROJ_PIECE_COLS = 256
MLP_WEIGHT_BLOCK_STEPS = 4
MLP_WEIGHT_PARTS = 4
POST_TILE = 512
POST_BLOCK_ROWS = 256
FF_CHUNK = 1024
STAGE_ROWS, STAGE_COLS = 512, 1024
STAGE_SLOTS = 3
BF16_TILE_ROWS = 16

VMEM_LIMIT_BYTES = 56 * 1024 * 1024

LOG_GAMMA = [float(np.log(1.0 - 2.0 ** (-5.0 - h))) for h in range(RET_HEADS)]
ALIBI_SLOPES = [float(2.0 ** (-8.0 / SWA_HEADS * (h + 1))) for h in range(SWA_HEADS)]
K_SCALE = float(RET_DK ** -0.5)
S_SCALE = float(SWA_DH ** -0.5)
LOG2E = float(np.log2(np.e))

f32 = jnp.float32
bf16 = jnp.bfloat16


def _swa_out_row_order():
    return np.arange(SWA_Q).reshape(SWA_GROUPS, SWA_REP, SWA_DH).transpose(1, 0, 2).reshape(-1)


def _rms_scale(v):
    return lax.rsqrt(jnp.mean(v * v, axis=-1, keepdims=True) + EPS)


def _sigmoid(v):
    return 0.5 * jnp.tanh(0.5 * v) + 0.5


def _cols(offset, index, width):
    return slice(offset + index * width, offset + (index + 1) * width)


def _store_bf16(dst_ref, chunk):
    dst_ref[...] = chunk[0:dst_ref.shape[0], 0:dst_ref.shape[1]].astype(bf16)


def _plain_weight_jobs(src_hbm, dst_ref, col0=0):
    n_rows, n_cols = dst_ref.shape
    jobs = []
    for r0 in range(0, n_rows, STAGE_ROWS):
        for c0 in range(0, n_cols, STAGE_COLS):
            rows, cols = min(STAGE_ROWS, n_rows - r0), min(STAGE_COLS, n_cols - c0)
            piece = (src_hbm.at[r0:r0 + rows, col0 + c0:col0 + c0 + cols], slice(0, rows), slice(0, cols))
            jobs.append(([piece], functools.partial(_store_bf16, dst_ref.at[r0:r0 + rows, c0:c0 + cols])))
    return jobs


def _stream_weight_chunks(jobs, stage_ref, sem_ref):
    slots = stage_ref.shape[0]

    def copies(k):
        slot = k % slots
        return [pltpu.make_async_copy(src, stage_ref.at[slot, rows, cols], sem_ref.at[slot])
                for src, rows, cols in jobs[k][0]]

    for k in range(min(slots - 1, len(jobs))):
        for copy in copies(k):
            copy.start()
    for k, job in enumerate(jobs):
        ahead = k + slots - 1
        if ahead < len(jobs):
            for copy in copies(ahead):
                copy.start()
        for copy in copies(k):
            copy.wait()
        job[1](stage_ref[k % slots])


def _init_tables(sink_ref, qdec_ref, kdec_ref, dmat_ref, bias_ref):
    C = CHUNK
    pos = (lax.broadcasted_iota(jnp.int32, (TOK_TILE, RET_DK), 0) % C).astype(f32)
    i = lax.broadcasted_iota(jnp.int32, (C, C), 0).astype(f32)
    j = lax.broadcasted_iota(jnp.int32, (C, C), 1).astype(f32)
    diff = i - j
    for h in range(RET_HEADS):
        qdec_ref[:, _cols(0, h, RET_DK)] = jnp.exp(LOG_GAMMA[h] * (pos + 1.0))
        kdec_ref[:, _cols(0, h, RET_DK)] = jnp.exp(LOG_GAMMA[h] * (C - 1.0 - pos)) * K_SCALE
        dmat_ref[h] = jnp.where(diff >= 0, jnp.exp(LOG_GAMMA[h] * jnp.maximum(diff, 0.0)), 0.0) * K_SCALE
    kpos = lax.broadcasted_iota(jnp.int32, (2 * C, SWA_REP * C), 0)
    col = lax.broadcasted_iota(jnp.int32, (2 * C, SWA_REP * C), 1)
    rep = col // C
    dist = (col - rep * C + C) - kpos
    valid = (dist >= 0) & (dist < C)
    distf = dist.astype(f32)
    for g in range(SWA_GROUPS):
        slope = jnp.zeros((2 * C, SWA_REP * C), f32)
        sink = jnp.zeros((2 * C, SWA_REP * C), f32)
        for r in range(SWA_REP):
            slope = jnp.where(rep == r, ALIBI_SLOPES[g * SWA_REP + r], slope)
            sink = jnp.where(rep == r, sink_ref[g * SWA_REP + r], sink)
        alibi = -slope * distf
        bias_ref[g] = jnp.where(kpos == 0, sink, jnp.where(valid, alibi, -jnp.inf)) * LOG2E
        bias_ref[SWA_GROUPS + g] = jnp.where(
            kpos == 0, sink, jnp.where(valid & (kpos >= C), alibi, -jnp.inf)) * LOG2E


def _projection_pieces(x_ref, g_ref, w_ref, qdec_ref, kdec_ref, mix_ref, kdt_ref):
    C = CHUNK
    shared = {}

    def prepare():
        x = x_ref[...]
        shared["h"] = (x * _rms_scale(x) * g_ref[...]).astype(bf16)

    def proj(c0, width):
        return jnp.dot(shared["h"], w_ref[:, c0:c0 + width], preferred_element_type=f32)

    width = PROJ_PIECE_COLS

    def swish_gate(part):
        def run():
            g = proj(W_GR + part * width, width)
            mix_ref[:, _cols(M_SG, part, width)] = (g * _sigmoid(g)).astype(bf16)
        return run

    def queries(part):
        def run():
            q = proj(W_QR + part * width, width)
            mix_ref[:, _cols(M_QR, part, width)] = q.astype(bf16)
            mix_ref[:, _cols(M_QD, part, width)] = (q * qdec_ref[:, _cols(0, part, width)]).astype(bf16)
        return run

    def keys(part):
        def run():
            k = proj(W_KR + part * width, width)
            mix_ref[:, _cols(M_KR, part, width)] = k.astype(bf16)
            kd = k * kdec_ref[:, _cols(0, part, width)]
            heads_per_piece = width // RET_DK
            for ci in range(CHUNKS_PER_TILE):
                for hd in range(heads_per_piece):
                    kdt_ref[ci, _cols(0, part * heads_per_piece + hd, RET_DK), :] = (
                        kd[ci * C:(ci + 1) * C, _cols(0, hd, RET_DK)].T.astype(bf16))
        return run

    def attention_queries(part):
        def run():
            mix_ref[:, _cols(M_QS, part, width)] = proj(W_QS + part * width, width).astype(bf16)
        return run

    def attention_keys_values():
        kv_s = proj(W_KS, 2 * SWA_KV)
        ks = kv_s[:, 0:SWA_KV] * (S_SCALE * LOG2E)
        lane = lax.broadcasted_iota(jnp.int32, (TOK_TILE, SWA_KV), 1)
        for grp in range(SWA_GROUPS):
            in_group = (lane >= grp * SWA_DH) & (lane < (grp + 1) * SWA_DH)
            mix_ref[:, _cols(M_KG, grp, SWA_KV)] = jnp.where(in_group, ks, 0.0).astype(bf16)
        mix_ref[:, M_VS:M_VS + SWA_KV] = kv_s[:, SWA_KV:2 * SWA_KV].astype(bf16)

    def values(part):
        def run():
            mix_ref[:, _cols(M_VR, part, width)] = proj(W_VR + part * width, width).astype(bf16)
        return run

    light = [queries(n) for n in range(RET_QK // width)]
    light += [keys(n) for n in range(RET_QK // width)]
    light += [attention_queries(n) for n in range(SWA_Q // width)]
    light += [attention_keys_values]
    light += [values(n) for n in range(RET_V // width)]
    heavy = [swish_gate(n) for n in range(RET_V // width)]
    pieces = []
    per_heavy = -(-len(light) // len(heavy))
    for n, piece in enumerate(heavy):
        pieces.append(piece)
        pieces += light[n * per_heavy:(n + 1) * per_heavy]
    return prepare, pieces


def _mixer_phases(mix_ref, kdt_ref, first_tile_of_seq, mixed_ref,
                  state_ref, kprev_ref, vprev_ref, dmat_ref, bias_ref):
    C = CHUNK
    ones_rows = jnp.ones((BF16_TILE_ROWS, 2 * C), bf16)
    live = [dict() for _ in range(CHUNKS_PER_TILE)]

    def rows(t):
        return slice(t * C, (t + 1) * C)

    def zero_first_row(block):
        top = block[0:BF16_TILE_ROWS, :]
        is_first = lax.broadcasted_iota(jnp.int32, top.shape, 0) == 0
        return jnp.concatenate([jnp.where(is_first, jnp.zeros_like(top), top), block[BF16_TILE_ROWS:, :]], axis=0)

    def first_matmuls(t):
        st = live[t]
        st["qk"] = [lax.dot_general(mix_ref[rows(t), _cols(M_QR, h, RET_DK)], mix_ref[rows(t), _cols(M_KR, h, RET_DK)],
                                    (((1,), (1,)), ((), ())), preferred_element_type=f32)
                    for h in range(RET_HEADS)]
        st["kv"] = [jnp.dot(kdt_ref[t, _cols(0, h, RET_DK), :], mix_ref[rows(t), _cols(M_VR, h, RET_DV)],
                            preferred_element_type=f32) for h in range(RET_HEADS)]
        if t == 0:
            keys = [jnp.concatenate([kprev_ref[g], mix_ref[0:C, _cols(M_KG, g, SWA_KV)]], axis=0)
                    for g in range(SWA_GROUPS)]
            vals = jnp.concatenate([vprev_ref[...], mix_ref[0:C, M_VS:M_VS + SWA_KV]], axis=0)
            table0 = jnp.where(first_tile_of_seq, SWA_GROUPS, 0)
        else:
            prev_and_current = slice((t - 1) * C, (t + 1) * C)
            keys = [mix_ref[prev_and_current, _cols(M_KG, g, SWA_KV)] for g in range(SWA_GROUPS)]
            vals = mix_ref[prev_and_current, M_VS:M_VS + SWA_KV]
            table0 = 0
        keys = [zero_first_row(k) for k in keys]
        vals_t = zero_first_row(vals).astype(f32).T
        st["vt_ones"] = [jnp.concatenate([vals_t[g * SWA_DH:(g + 1) * SWA_DH, :].astype(bf16), ones_rows], axis=0)
                         for g in range(SWA_GROUPS)]
        q_all = jnp.concatenate([mix_ref[rows(t), _cols(M_QS, j, C)] for j in range(SWA_REP)], axis=0)
        st["s"] = [lax.dot_general(keys[g], q_all, (((1,), (1,)), ((), ())), preferred_element_type=f32)
                   + bias_ref[table0 + g] for g in range(SWA_GROUPS)]

    def retention_vector_work(t):
        st = live[t]
        st["a"], st["state_bf16"] = [], []
        for h in range(RET_HEADS):
            s_prev = state_ref[h]
            st["a"].append((st["qk"][h] * dmat_ref[h]).astype(bf16))
            st["state_bf16"].append(s_prev.astype(bf16))
            state_ref[h] = float(np.exp(LOG_GAMMA[h] * C)) * s_prev + st["kv"][h]
        del st["qk"], st["kv"]

    def softmax_numerators(t):
        st = live[t]
        st["p"] = [jnp.exp2(s - jnp.max(s, axis=0, keepdims=True)).astype(bf16) for s in st["s"]]
        del st["s"]

    def second_matmuls(t):
        st = live[t]
        st["o"] = [jnp.dot(jnp.concatenate([st["a"][h], mix_ref[rows(t), _cols(M_QD, h, RET_DK)]], axis=1),
                           jnp.concatenate([mix_ref[rows(t), _cols(M_VR, h, RET_DV)], st["state_bf16"][h]], axis=0),
                           preferred_element_type=f32) for h in range(RET_HEADS)]
        st["r"] = [jnp.dot(st["vt_ones"][g], st["p"][g], preferred_element_type=f32) for g in range(SWA_GROUPS)]
        del st["a"], st["state_bf16"], st["p"], st["vt_ones"]

    def retention_output(t):
        st = live[t]
        for h in range(RET_HEADS):
            o = st["o"][h]
            mu = jnp.mean(o, axis=-1, keepdims=True)
            d = o - mu
            var = jnp.mean(d * d, axis=-1, keepdims=True)
            mixed_ref[rows(t), _cols(X_RET, h, RET_DV)] = (
                mix_ref[rows(t), _cols(M_SG, h, RET_DV)].astype(f32) * (d * lax.rsqrt(var + EPS))).astype(bf16)
        del st["o"]

    def attention_output(t):
        st = live[t]
        o_t = [r[0:SWA_DH, :] * (1.0 / r[SWA_DH:SWA_DH + 1, :]) for r in st["r"]]
        for j in range(SWA_REP):
            pair_t = jnp.concatenate([o[:, _cols(0, j, C)] for o in o_t], axis=0)
            mixed_ref[rows(t), _cols(X_SWA, j, C)] = pair_t.T.astype(bf16)
        st.clear()

    def carry_last_block():
        last = slice(TOK_TILE - C, TOK_TILE)
        for g in range(SWA_GROUPS):
            kprev_ref[g] = mix_ref[last, _cols(M_KG, g, SWA_KV)]
        vprev_ref[...] = mix_ref[last, M_VS:M_VS + SWA_KV]

    phases = []
    for t in range(CHUNKS_PER_TILE + 1):
        if t < CHUNKS_PER_TILE:
            phases.append(functools.partial(first_matmuls, t))
        if t >= 1:
            phases += [functools.partial(fn, t - 1) for fn in (
                retention_vector_work, softmax_numerators, second_matmuls, retention_output, attention_output)]
    phases.append(carry_last_block)
    return phases


def _store_attention_columns(dst_ref, chunk):
    lanes = 2 * SWA_DH
    first_half = lax.broadcasted_iota(jnp.int32, (chunk.shape[0], lanes), 1) < SWA_DH
    for j in range(SWA_REP):
        a = chunk[:, _cols(0, j // 2, lanes)]
        b = chunk[:, _cols(0, (SWA_REP + j) // 2, lanes)]
        if j % 2 == 0:
            pair = jnp.where(first_half, a, pltpu.roll(b, SWA_DH, 1))
        else:
            pair = jnp.where(first_half, pltpu.roll(a, SWA_DH, 1), b)
        dst_ref[:, _cols(0, j, lanes)] = pair.astype(bf16)
    rest = slice(SWA_Q, SWA_Q + 2 * SWA_KV)
    dst_ref[:, rest] = chunk[:, rest].astype(bf16)


def _load_mixer_weights(w_in_hbm, w_ref, stage_ref, sem_ref):
    jobs = _plain_weight_jobs(w_in_hbm, w_ref.at[:, 0:W_QS])
    for r0 in range(0, D_MODEL, STAGE_ROWS):
        rows = slice(r0, r0 + STAGE_ROWS)
        piece = (w_in_hbm.at[rows, W_QS:D_MIXW], slice(0, STAGE_ROWS), slice(0, D_MIXW - W_QS))
        jobs.append(([piece], functools.partial(_store_attention_columns, w_ref.at[rows, W_QS:D_MIXW])))
    _stream_weight_chunks(jobs, stage_ref, sem_ref)


def _mlp_weight_conversions(s, w_up_ref, w_dn_ref, w_up_bf16_ref, w_dn_bf16_ref):
    sub = lax.rem(s, MLP_WEIGHT_BLOCK_STEPS)
    thunks = []
    for src_ref, dst_ref in ((w_up_ref, w_up_bf16_ref), (w_dn_ref, w_dn_bf16_ref)):
        rows = src_ref.shape[0] // MLP_WEIGHT_BLOCK_STEPS
        row0 = pl.multiple_of(sub * rows, BF16_TILE_ROWS)
        if rows >= MLP_WEIGHT_PARTS * BF16_TILE_ROWS:
            part = rows // MLP_WEIGHT_PARTS
            windows = [(pl.ds(row0 + p * part, part), slice(None)) for p in range(MLP_WEIGHT_PARTS)]
        else:
            part = src_ref.shape[1] // MLP_WEIGHT_PARTS
            windows = [(pl.ds(row0, rows), slice(p * part, (p + 1) * part)) for p in range(MLP_WEIGHT_PARTS)]
        for window in windows:
            def convert(src_ref=src_ref, dst_ref=dst_ref, window=window):
                dst_ref[window] = src_ref[window].astype(bf16)
            thunks.append(convert)
    return thunks


def _proj_mixer_kernel(x_ref, g_ref, w_in_hbm, sink_ref, w_up_ref, w_dn_ref,
                       mixed_ref, w_up_bf16_ref, w_dn_bf16_ref,
                       w_ref, stage_ref, sem_ref,
                       mix_even_ref, mix_odd_ref, kdt_even_ref, kdt_odd_ref, qdec_ref, kdec_ref,
                       state_ref, kprev_ref, vprev_ref, dmat_ref, bias_ref, *, tiles_per_seq, num_tiles):
    s = pl.program_id(0)
    first_tile_of_seq = lax.rem(s + tiles_per_seq - 1, tiles_per_seq) == 0

    @pl.when(s == 0)
    def _first_step():
        _load_mixer_weights(w_in_hbm, w_ref, stage_ref, sem_ref)
        _init_tables(sink_ref, qdec_ref, kdec_ref, dmat_ref, bias_ref)

    @pl.when(first_tile_of_seq)
    def _reset_sequence_state():
        state_ref[...] = jnp.zeros_like(state_ref)
        kprev_ref[...] = jnp.zeros_like(kprev_ref)
        vprev_ref[...] = jnp.zeros_like(vprev_ref)

    even = (mix_even_ref, kdt_even_ref)
    odd = (mix_odd_ref, kdt_odd_ref)

    def step(write, read):
        prepare, pieces = (lambda: None), []
        if write is not None:
            prepare, pieces = _projection_pieces(x_ref, g_ref, w_ref, qdec_ref, kdec_ref, *write)
            conversions = _mlp_weight_conversions(s, w_up_ref, w_dn_ref, w_up_bf16_ref, w_dn_bf16_ref)
            assert 2 * len(conversions) <= len(pieces) + 1
            for n, conversion in enumerate(conversions):
                pieces[2 * n] = functools.partial(lambda piece, conversion: (piece(), conversion()),
                                                  pieces[2 * n], conversion)
        phases = []
        if read is not None:
            phases = _mixer_phases(*read, first_tile_of_seq, mixed_ref,
                                   state_ref, kprev_ref, vprev_ref, dmat_ref, bias_ref)
        prepare()
        for phase in phases[:1]:
            phase()
        emitted = 0
        for n, phase in enumerate(phases[1:], start=1):
            while emitted < len(pieces) and emitted * (len(phases) - 1) < n * len(pieces):
                pieces[emitted]()
                emitted += 1
            phase()
        for piece in pieces[emitted:]:
            piece()

    last = num_tiles
    last_read = odd if (num_tiles - 1) % 2 else even

    @pl.when(s == 0)
    def _first_tile():
        step(even, None)

    @pl.when((s > 0) & (s < last) & (lax.rem(s, 2) == 0))
    def _even_step():
        step(even, odd)

    @pl.when((s > 0) & (s < last) & (lax.rem(s, 2) == 1))
    def _odd_step():
        step(odd, even)

    @pl.when(s == last)
    def _last_tile():
        step(None, last_read)


def _proj_mixer(x2d, gain, w_in, sinks, w_up, w_dn, seq):
    n = x2d.shape[0]
    nt = n // TOK_TILE
    C = CHUNK
    steps = MLP_WEIGHT_BLOCK_STEPS
    assert nt % steps == 0 and D_MODEL % (nt * BF16_TILE_ROWS) == 0
    const = lambda s: (0, 0)
    projected_tile = lambda s: (jnp.minimum(s, nt - 1), 0)
    mixed_tile = lambda s: (jnp.maximum(s - 1, 0), 0)
    weight_block = lambda s: (jnp.minimum(s, nt - 1) // steps, 0)
    up_block = pl.BlockSpec((steps * D_MODEL // nt, D_FF), weight_block)
    dn_block = pl.BlockSpec((steps * D_FF // nt, D_MODEL), weight_block)
    return pl.pallas_call(
        functools.partial(_proj_mixer_kernel, tiles_per_seq=seq // TOK_TILE, num_tiles=nt),
        out_shape=(jax.ShapeDtypeStruct((n, D_MIXED), bf16),
                   jax.ShapeDtypeStruct((D_MODEL, D_FF), bf16),
                   jax.ShapeDtypeStruct((D_FF, D_MODEL), bf16)),
        grid=(nt + 1,),
        in_specs=[
            pl.BlockSpec((TOK_TILE, D_MODEL), projected_tile),
            pl.BlockSpec((1, D_MODEL), const),
            pl.BlockSpec(memory_space=pl.ANY),
            pl.BlockSpec(memory_space=pltpu.SMEM),
            up_block, dn_block,
        ],
        out_specs=(pl.BlockSpec((TOK_TILE, D_MIXED), mixed_tile), up_block, dn_block),
        scratch_shapes=[
            pltpu.VMEM((D_MODEL, D_MIXW), bf16),
            pltpu.VMEM((STAGE_SLOTS, STAGE_ROWS, STAGE_COLS), f32),
            pltpu.SemaphoreType.DMA((STAGE_SLOTS,)),
            pltpu.VMEM((TOK_TILE, D_MIXIN), bf16),
            pltpu.VMEM((TOK_TILE, D_MIXIN), bf16),
            pltpu.VMEM((CHUNKS_PER_TILE, RET_QK, C), bf16),
            pltpu.VMEM((CHUNKS_PER_TILE, RET_QK, C), bf16),
            pltpu.VMEM((TOK_TILE, RET_QK), f32),
            pltpu.VMEM((TOK_TILE, RET_QK), f32),
            pltpu.VMEM((RET_HEADS, RET_DK, RET_DV), f32),
            pltpu.VMEM((SWA_GROUPS, C, SWA_KV), bf16),
            pltpu.VMEM((C, SWA_KV), bf16),
            pltpu.VMEM((RET_HEADS, C, C), f32),
            pltpu.VMEM((2 * SWA_GROUPS, 2 * C, SWA_REP * C), f32),
        ],
        compiler_params=pltpu.CompilerParams(
            dimension_semantics=("arbitrary",), vmem_limit_bytes=VMEM_LIMIT_BYTES),
        name="proj_mixer",
    )(x2d, gain, w_in, sinks, w_up, w_dn)


def _load_post_weights(w_in_hbm, w_ro_hbm, w_so_hbm, w_o_hbm,
                       wg_ref, wro_ref, wso_ref, wo_ref, stage_ref, sem_ref):
    jobs = _plain_weight_jobs(w_in_hbm, wg_ref, col0=D_MIXW)
    jobs += _plain_weight_jobs(w_ro_hbm, wro_ref)
    runs = _swa_out_row_order().reshape(-1, SWA_DH)[:, 0]
    pieces = [(w_so_hbm.at[int(r0):int(r0) + SWA_DH, :], slice(n * SWA_DH, (n + 1) * SWA_DH), slice(0, STAGE_COLS))
              for n, r0 in enumerate(runs)]
    jobs += [(pieces, functools.partial(_store_bf16, wso_ref))]
    jobs += _plain_weight_jobs(w_o_hbm, wo_ref)
    _stream_weight_chunks(jobs, stage_ref, sem_ref)


def _post_kernel(x_ref, mixed_ref, gpre_mix_ref, w_in_hbm, w_ro_hbm, w_so_hbm, w_o_hbm, gpost_mix_ref,
                 gpre_mlp_ref, wup_ref, wdn_ref, gpost_mlp_ref, o_ref,
                 wg_ref, wro_ref, wso_ref, wo_ref, stage_ref, sem_ref):
    tm = x_ref.shape[0]
    ff_starts = list(range(0, D_FF, FF_CHUNK))

    @pl.when(pl.program_id(0) == 0)
    def _first_step():
        _load_post_weights(w_in_hbm, w_ro_hbm, w_so_hbm, w_o_hbm,
                           wg_ref, wro_ref, wso_ref, wo_ref, stage_ref, sem_ref)

    def row_block(rows):
        x = x_ref[rows, :]
        y_r = jnp.dot(mixed_ref[rows, X_RET:X_RET + RET_V], wro_ref[...], preferred_element_type=f32)
        y_s = jnp.dot(mixed_ref[rows, X_SWA:X_SWA + SWA_Q], wso_ref[...], preferred_element_type=f32)
        h = (x * _rms_scale(x) * gpre_mix_ref[...]).astype(bf16)
        gate_r = jnp.dot(h, wg_ref[:, 0:D_MODEL], preferred_element_type=f32)
        gate_s = jnp.dot(h, wg_ref[:, D_MODEL:D_GATES], preferred_element_type=f32)
        yield
        merged = _sigmoid(gate_r) * y_r + _sigmoid(gate_s) * y_s
        z = jnp.dot(merged.astype(bf16), wo_ref[...], preferred_element_type=f32)
        yield
        x1 = x + z * _rms_scale(z) * gpost_mix_ref[...]
        h1 = (x1 * _rms_scale(x1) * gpre_mlp_ref[...]).astype(bf16)

        def up(c0):
            return jnp.dot(h1, wup_ref[:, c0:c0 + FF_CHUNK], preferred_element_type=f32)

        acc = None
        u_next = up(ff_starts[0])
        yield
        for n, c0 in enumerate(ff_starts):
            u = u_next
            if n + 1 < len(ff_starts):
                u_next = up(ff_starts[n + 1])
            u = jnp.maximum(u, 0.0)
            part = jnp.dot((u * u).astype(bf16), wdn_ref[c0:c0 + FF_CHUNK, :], preferred_element_type=f32)
            acc = part if acc is None else acc + part
            yield
        o_ref[rows, :] = x1 + acc * _rms_scale(acc) * gpost_mlp_ref[...]

    active = [row_block(slice(r0, r0 + POST_BLOCK_ROWS)) for r0 in range(0, tm, POST_BLOCK_ROWS)]
    while active:
        for gen in list(active):
            if next(gen, "done") == "done":
                active.remove(gen)


def _post(x2d, mixed, gpre_mix, w_in, w_ro, w_so, w_o, gpost_mix, gpre_mlp, w_up_bf16, w_dn_bf16, gpost_mlp):
    n = x2d.shape[0]
    const = lambda i: (0, 0)
    tile = lambda width: pl.BlockSpec((POST_TILE, width), lambda i: (i, 0))
    in_hbm = pl.BlockSpec(memory_space=pl.ANY)
    gain = pl.BlockSpec((1, D_MODEL), const)
    resident = lambda shape: pl.BlockSpec(shape, const, pipeline_mode=pl.Buffered(1))
    return pl.pallas_call(
        _post_kernel,
        out_shape=jax.ShapeDtypeStruct((n, D_MODEL), f32),
        grid=(n // POST_TILE,),
        in_specs=[
            tile(D_MODEL), tile(D_MIXED),
            gain, in_hbm,
            in_hbm, in_hbm, in_hbm, gain,
            gain, resident((D_MODEL, D_FF)), resident((D_FF, D_MODEL)), gain,
        ],
        out_specs=tile(D_MODEL),
        scratch_shapes=[
            pltpu.VMEM((D_MODEL, D_GATES), bf16),
            pltpu.VMEM((RET_V, D_MODEL), bf16),
            pltpu.VMEM((SWA_Q, D_MODEL), bf16),
            pltpu.VMEM((D_MODEL, D_MODEL), bf16),
            pltpu.VMEM((STAGE_SLOTS, STAGE_ROWS, STAGE_COLS), f32),
            pltpu.SemaphoreType.DMA((STAGE_SLOTS,)),
        ],
        compiler_params=pltpu.CompilerParams(
            dimension_semantics=("arbitrary",), vmem_limit_bytes=VMEM_LIMIT_BYTES),
        name="post",
    )(x2d, mixed, gpre_mix, w_in, w_ro, w_so, w_o, gpost_mix, gpre_mlp, w_up_bf16, w_dn_bf16, gpost_mlp)


def kernel(x, pre_mix_norm, w_in, w_ret_out, w_swa_out, w_out, sinks, post_mix_norm, pre_mlp_norm, w_up,
           w_down, post_mlp_norm):
    batch, seq, d = x.shape
    depth = w_in.shape[0]
    assert d == D_MODEL and seq % TOK_TILE == 0 and seq // TOK_TILE > 1 and (batch * seq) % POST_TILE == 0
    x2d = x.reshape(batch * seq, d)
    for l in range(depth):
        pre_mix_gain = pre_mix_norm[l][None, :]
        mixed, w_up_bf16, w_dn_bf16 = _proj_mixer(x2d, pre_mix_gain, w_in[l], sinks[l].astype(f32),
                                                  w_up[l], w_down[l], seq)
        x2d = _post(x2d, mixed, pre_mix_gain, w_in[l], w_ret_out[l], w_swa_out[l], w_out[l],
                    post_mix_norm[l][None, :], pre_mlp_norm[l][None, :], w_up_bf16, w_dn_bf16,
                    post_mlp_norm[l][None, :])
    return x2d.reshape(batch, seq, d)
```
